```python
import jax
import jax.numpy as jnp
from jax import lax
import numpy as np

D_MODEL = 4096
BATCH = 2
SEQ = 4096
DEPTH = 2

CTX_LEN = 256
GRID_W = 64
NORM_EPS = 1e-6

MLA_HEADS = 16
Q_LORA = 1024
KV_LORA = 512
QK_NOPE = 128
QK_ROPE = 64
QK_HEAD = QK_NOPE + QK_ROPE
V_HEAD = 128
MLA_WIDTH = MLA_HEADS * V_HEAD
ROPE_AXIS_DIM = QK_ROPE // 2
ROPE_PAIRS = ROPE_AXIS_DIM // 2
ROPE_BASE = 10000.0
Q_BLOCK = 128

FNET_GROUPS = 4
FNET_GROUP_DIM = 256
FNET_WIDTH = FNET_GROUPS * FNET_GROUP_DIM

RWKV_HEADS = 16
RWKV_HEAD_DIM = 64
RWKV_WIDTH = RWKV_HEADS * RWKV_HEAD_DIM
DECAY_LORA = 128
ICLR_LORA = 128
VRES_LORA = 96
GATE_LORA = 480
LNX_EPS = 64e-5

N_BRANCHES = 3
MLA_IN = Q_LORA + KV_LORA + QK_ROPE
RWKV_IN = 3 * RWKV_WIDTH + 2 * DECAY_LORA + 2 * ICLR_LORA + GATE_LORA
W_IN = N_BRANCHES * D_MODEL + MLA_IN + FNET_WIDTH + RWKV_IN

D_FF = 11008
N_EXPERTS = 8
TOP_K = 2
D_FF_EXPERT = 3072

kernel_name = 'hybrid_mla_fnet_rwkv7_moe_dit_block'


def _split(t, sizes):
    return jnp.split(t, [int(s) for s in np.cumsum(sizes)], axis=-1)


def rmsnorm(x, g, eps=NORM_EPS):
    xf = x.astype(jnp.float32)
    y = xf * lax.rsqrt(jnp.mean(xf * xf, axis=-1, keepdims=True) + eps)
    return (y * g.astype(jnp.float32)).astype(x.dtype)


def adaln(cond, ada_w, ada_b):
    mod = jax.nn.silu(cond) @ ada_w + ada_b
    return jnp.split(mod[:, None, :], 6, axis=-1)


def modulate(h, shift, scale):
    return h * (1 + scale) + shift


def rope_tables(rows):
    row = jnp.repeat(jnp.arange(rows, dtype=jnp.float32), GRID_W)
    col = jnp.tile(jnp.arange(GRID_W, dtype=jnp.float32), rows)
    inv_freq = ROPE_BASE ** (-2.0 * jnp.arange(ROPE_PAIRS, dtype=jnp.float32) / ROPE_AXIS_DIM)
    ang = jnp.stack([row[:, None] * inv_freq, col[:, None] * inv_freq], axis=1)
    return jnp.cos(ang), jnp.sin(ang)


def rope2d(x, cos, sin):
    B, T, H, _ = x.shape
    xa = x.reshape(B, T, H, 2, 2, ROPE_PAIRS)
    x1, x2 = xa[..., 0, :], xa[..., 1, :]
    c, s = cos[None, :, None], sin[None, :, None]
    out = jnp.stack([x1 * c - x2 * s, x2 * c + x1 * s], axis=-2)
    return out.reshape(B, T, H, QK_ROPE).astype(x.dtype)


def apply_pos(t, cos, sin):
    if cos is None:
        return t
    return jnp.concatenate([t[..., :QK_NOPE], rope2d(t[..., QK_NOPE:], cos, sin)], axis=-1)


def mla_q(cq, q_norm, w_uq, q_gain, cos, sin):
    B, T, _ = cq.shape
    q = (rmsnorm(cq, q_norm) @ w_uq).reshape(B, T, MLA_HEADS, QK_HEAD)
    return apply_pos(rmsnorm(q, q_gain), cos, sin)


def mla_kv(ckv, kr, kv_norm, w_ukv, k_gain, cos, sin):
    B, T, _ = ckv.shape
    kv = (rmsnorm(ckv, kv_norm) @ w_ukv).reshape(B, T, MLA_HEADS, QK_NOPE + V_HEAD)
    k_rope = jnp.broadcast_to(kr[:, :, None, :], (B, T, MLA_HEADS, QK_ROPE))
    k = jnp.concatenate([kv[..., :QK_NOPE], k_rope], axis=-1)
    return apply_pos(rmsnorm(k, k_gain), cos, sin), kv[..., QK_NOPE:]


def attention(q, k, v):
    B, T, H, Dq = q.shape
    nb = T // Q_BLOCK
    qb = jnp.moveaxis(q.reshape(B, nb, Q_BLOCK, H, Dq), 1, 0)
    scale = QK_HEAD ** -0.5

    def one_block(qi):
        s = jnp.einsum('bqhd,bkhd->bhqk', qi, k).astype(jnp.float32) * scale
        p = jax.nn.softmax(s, axis=-1).astype(v.dtype)
        return jnp.einsum('bhqk,bkhd->bqhd', p, v)

    o = lax.map(one_block, qb)
    return jnp.moveaxis(o, 0, 1).reshape(B, T, H * v.shape[-1])


def fourier_mix(f):
    B, T, _ = f.shape
    z = f.astype(jnp.float32).reshape(B, T, FNET_GROUPS, FNET_GROUP_DIM)
    y = jnp.fft.fft2(z, axes=(1, 3), norm='ortho').real
    return y.reshape(B, T, FNET_WIDTH).astype(f.dtype)


def centred_shift(z):
    zp = jnp.pad(z, ((0, 0), (1, 1), (0, 0)))
    return 0.5 * (zp[:, :-2] + zp[:, 2:])


def rwkv_features(z, mu, w0, w2, a0, a2, k_k, k_a, vres):
    B, T, _ = z.shape
    zf = (z + mu * (centred_shift(z) - z)).astype(jnp.float32)
    r, k, v, wd_f, wd_b, ad_f, ad_b, rest = _split(
        zf, [RWKV_WIDTH] * 3 + [DECAY_LORA] * 2 + [ICLR_LORA] * 2)
    gd = rest[..., :GATE_LORA]
    v_raw = v
    if vres is not None:
        v0, v2, v_first = vres
        v = v + (v_first - v) * jax.nn.sigmoid(v0 + rest[..., GATE_LORA:] @ v2)
    heads = lambda t: t.reshape(B, T, RWKV_HEADS, RWKV_HEAD_DIM)
    kk = heads(k * k_k)
    kk = kk * lax.rsqrt(jnp.sum(kk * kk, axis=-1, keepdims=True) + 1e-12)
    dirs = []
    for d, (wd, ad) in enumerate(((wd_f, ad_f), (wd_b, ad_b))):
        w_log = -jax.nn.softplus(-(w0[d] + jnp.tanh(wd) @ w2[d])) - 0.5
        a = jax.nn.sigmoid(a0[d] + ad @ a2[d])
        dirs.append((heads(k * (1 + (a - 1) * k_a)), heads(jnp.exp(-jnp.exp(w_log))), heads(a)))
    return heads(r), heads(v), kk, dirs, gd, v_raw


def rwkv_scan(s0, r, k, v, decay, kk, a, reverse, emit):
    xs = (k, v, decay, kk, a) + ((r,) if emit else ())

    def step(S, inp):
        k_t, v_t, w_t, kk_t, a_t = inp[:5]
        S = (S * w_t[:, :, None, :]
             - jnp.einsum('bhij,bhj->bhi', S, kk_t)[..., None] * (kk_t * a_t)[:, :, None, :]
             + v_t[..., None] * k_t[:, :, None, :])
        y = jnp.einsum('bhij,bhj->bhi', S, inp[5]) if emit else None
        return S, y

    S, ys = lax.scan(step, s0, tuple(jnp.moveaxis(t, 1, 0) for t in xs), reverse=reverse)
    return S, (jnp.moveaxis(ys, 0, 1) if emit else None)


def rwkv_stream(feat, s_f, s_b, emit):
    r, v, kk, dirs, _, _ = feat
    (k_f, dec_f, a_f), (k_b, dec_b, a_b) = dirs
    S_f, y_f = rwkv_scan(s_f, r, k_f, v, dec_f, kk, a_f, False, emit)
    S_b, y_b = rwkv_scan(s_b, r, k_b, v, dec_b, kk, a_b, True, emit)
    return S_f, S_b, y_f, y_b


def rwkv_readout(feat, y_f, y_b, r_k, lnx_w, lnx_b, g2, dtype):
    r, v, _, dirs, gd, _ = feat
    B, T = r.shape[:2]
    y = y_f + y_b
    mean = jnp.mean(y, axis=-1, keepdims=True)
    var = jnp.mean(jnp.square(y - mean), axis=-1, keepdims=True)
    y = ((y - mean) * lax.rsqrt(var + LNX_EPS)).reshape(B, T, RWKV_WIDTH) * lnx_w + lnx_b
    (k_f, _, _), (k_b, _, _) = dirs
    bonus = (jnp.sum(r * k_f * r_k, axis=-1, keepdims=True)
             + jnp.sum(r * k_b * r_k, axis=-1, keepdims=True)) * v
    g = jax.nn.sigmoid(gd) @ g2
    return ((y + bonus.reshape(B, T, RWKV_WIDTH)) * g).astype(dtype)


def merge(gates, o_a, o_b, o_c, w_a, w_b, w_c, w_out):
    g_a, g_b, g_c = jnp.split(jax.nn.sigmoid(gates), N_BRANCHES, axis=-1)
    return (g_a * (o_a @ w_a) + g_b * (o_b @ w_b) + g_c * (o_c @ w_c)) @ w_out


def channel_mixer(i, u, ffn_w1, ffn_w3, ffn_w2, router, moe_w1, moe_w3, moe_w2):
    j = i // 2
    if i % 2 == 0:
        return (jax.nn.silu(u @ ffn_w1[j]) * (u @ ffn_w3[j])) @ ffn_w2[j]
    logits = (u @ router[j]).astype(jnp.float32)
    top_logit, top_idx = lax.top_k(logits, TOP_K)
    top_w = jax.nn.softmax(top_logit, axis=-1)
    combine = jnp.sum(jax.nn.one_hot(top_idx, N_EXPERTS, dtype=jnp.float32) * top_w[..., None],
                      axis=-2).astype(u.dtype)
    y = jnp.zeros_like(u)
    for e in range(N_EXPERTS):
        h = jax.nn.silu(u @ moe_w1[j, e]) * (u @ moe_w3[j, e])
        y = y + combine[..., e:e + 1] * (h @ moe_w2[j, e])
    return y


def setup_inputs(seed: int = 0) -> dict:
    key = jax.random.key(seed)
    keys = iter(jax.random.split(key, 64))
    f32 = jnp.float32

    def nrm(shape, scale=1.0):
        return scale * jax.random.normal(next(keys), shape, f32)

    def gain(shape):
        return 1.0 + 0.05 * jax.random.normal(next(keys), shape, f32)

    def unif(shape):
        return jax.random.uniform(next(keys), shape, f32)

    D = D_MODEL
    L = DEPTH
    n_dense = (DEPTH + 1) // 2
    n_moe = DEPTH // 2
    n_vres = DEPTH - 1
    return {
        'x': nrm((BATCH, SEQ, D)),
        'c': nrm((BATCH, D)),
        'ctx': nrm((BATCH, CTX_LEN, D)),
        'c_ctx': nrm((D,)),
        'ada_w': nrm((L, D, 6 * D), D ** -0.5),
        'ada_b': nrm((L, 6 * D), 0.02),
        'norm1': gain((L, D)),
        'w_in': nrm((L, D, W_IN), D ** -0.5),
        'w_vres_down': nrm((n_vres, D, VRES_LORA), D ** -0.5),
        'q_norm': gain((L, Q_LORA)),
        'w_uq': nrm((L, Q_LORA, MLA_HEADS * QK_HEAD), Q_LORA ** -0.5),
        'kv_norm': gain((L, KV_LORA)),
        'w_ukv': nrm((L, KV_LORA, MLA_HEADS * (QK_NOPE + V_HEAD)), KV_LORA ** -0.5),
        'q_gain': gain((L, QK_HEAD)),
        'k_gain': gain((L, QK_HEAD)),
        'rwkv_mu': unif((L, RWKV_IN)),
        'vres_mu': unif((n_vres, VRES_LORA)),
        'w0': nrm((L, 2, RWKV_WIDTH), 0.5) - 1.0,
        'w2': nrm((L, 2, DECAY_LORA, RWKV_WIDTH), 0.5 * DECAY_LORA ** -0.5),
        'a0': nrm((L, 2, RWKV_WIDTH), 0.1),
        'a2': nrm((L, 2, ICLR_LORA, RWKV_WIDTH), 0.5 * ICLR_LORA ** -0.5),
        'k_k': 0.85 + nrm((L, RWKV_WIDTH), 0.05),
        'k_a': gain((L, RWKV_WIDTH)),
        'v0': nrm((n_vres, RWKV_WIDTH), 0.1),
        'v2': nrm((n_vres, VRES_LORA, RWKV_WIDTH), 0.5 * VRES_LORA ** -0.5),
        'r_k': nrm((L, RWKV_HEADS, RWKV_HEAD_DIM), 0.1),
        'lnx_w': gain((L, RWKV_WIDTH)),
        'lnx_b': nrm((L, RWKV_WIDTH), 0.02),
        'g2': nrm((L, GATE_LORA, RWKV_WIDTH), GATE_LORA ** -0.5),
        'w_br_a': nrm((L, MLA_WIDTH, D), MLA_WIDTH ** -0.5),
        'w_br_b': nrm((L, FNET_WIDTH, D), FNET_WIDTH ** -0.5),
        'w_br_c': nrm((L, RWKV_WIDTH, D), RWKV_WIDTH ** -0.5),
        'w_out': nrm((L, D, D), D ** -0.5),
        'norm2': gain((L, D)),
        'ffn_w1': nrm((n_dense, D, D_FF), D ** -0.5),
        'ffn_w3': nrm((n_dense, D, D_FF), D ** -0.5),
        'ffn_w2': nrm((n_dense, D_FF, D), D_FF ** -0.5),
        'router': nrm((n_moe, D, N_EXPERTS), D ** -0.5),
        'moe_w1': nrm((n_moe, N_EXPERTS, D, D_FF_EXPERT), D ** -0.5),
        'moe_w3': nrm((n_moe, N_EXPERTS, D, D_FF_EXPERT), D ** -0.5),
        'moe_w2': nrm((n_moe, N_EXPERTS, D_FF_EXPERT, D), D_FF_EXPERT ** -0.5),
    }


def reference(x, c, ctx, c_ctx, ada_w, ada_b, norm1, w_in, w_vres_down, q_norm, w_uq, kv_norm, w_ukv,
              q_gain, k_gain, rwkv_mu, vres_mu, w0, w2, a0, a2, k_k, k_a, v0, v2, r_k, lnx_w, lnx_b, g2,
              w_br_a, w_br_b, w_br_c, w_out, norm2, ffn_w1, ffn_w3, ffn_w2, router, moe_w1, moe_w3, moe_w2):
    B, n_lat, _ = x.shape
    rows = n_lat // GRID_W
    cos, sin = rope_tables(rows)
    s_zero = jnp.zeros((B, RWKV_HEADS, RWKV_HEAD_DIM, RWKV_HEAD_DIM), jnp.float32)
    col_sizes = [N_BRANCHES * D_MODEL, Q_LORA, KV_LORA, QK_ROPE, FNET_WIDTH]
    x_lat, x_ctx = x, ctx
    vf_lat = vf_ctx = None
    for i in range(DEPTH):
        last = i == DEPTH - 1
        sh1_l, sc1_l, gm_l, sh2_l, sc2_l, gf_l = adaln(c, ada_w[i], ada_b[i])
        sh1_c, sc1_c, gm_c, sh2_c, sc2_c, gf_c = adaln(c_ctx[None], ada_w[i], ada_b[i])
        if i == 0:
            w_comb, mu, vres_p = w_in[0], rwkv_mu[0], None
        else:
            w_comb = jnp.concatenate([w_in[i], w_vres_down[i - 1]], axis=1)
            mu = jnp.concatenate([rwkv_mu[i], vres_mu[i - 1]])
            vres_p = (v0[i - 1], v2[i - 1])

        u_l = modulate(rmsnorm(x_lat, norm1[i]), sh1_l, sc1_l)
        u_c = modulate(rmsnorm(x_ctx, norm1[i]), sh1_c, sc1_c)
        gate_l, cq_l, ckv_l, kr_l, f_l, z_l = _split(u_l @ w_comb, col_sizes)
        gate_c, cq_c, ckv_c, kr_c, f_c, z_c = _split(u_c @ w_comb, col_sizes)

        k_c, v_c = mla_kv(ckv_c, kr_c, kv_norm[i], w_ukv[i], k_gain[i], None, None)
        k_l, v_l = mla_kv(ckv_l, kr_l, kv_norm[i], w_ukv[i], k_gain[i], cos, sin)
        q_l = mla_q(cq_l, q_norm[i], w_uq[i], q_gain[i], cos, sin)
        o_a_l = attention(q_l, jnp.concatenate([k_l, k_c], axis=1), jnp.concatenate([v_l, v_c], axis=1))

        o_b_l = fourier_mix(f_l)

        rw = (mu, w0[i], w2[i], a0[i], a2[i], k_k[i], k_a[i])
        feat_c = rwkv_features(z_c, *rw, None if vres_p is None else (vres_p[0], vres_p[1], vf_ctx))
        feat_l = rwkv_features(z_l, *rw, None if vres_p is None else (vres_p[0], vres_p[1], vf_lat))
        sf_c, sb_c, yf_c, yb_c = rwkv_stream(feat_c, s_zero, s_zero, not last)
        _, _, yf_l, yb_l = rwkv_stream(feat_l, sf_c, sb_c, True)
        o_c_l = rwkv_readout(feat_l, yf_l, yb_l, r_k[i], lnx_w[i], lnx_b[i], g2[i], x.dtype)

        mix_w = (w_br_a[i], w_br_b[i], w_br_c[i], w_out[i])
        y_l = merge(gate_l, o_a_l, o_b_l, o_c_l, *mix_w)
        if i == 0:
            vf_lat, vf_ctx = feat_l[5], feat_c[5]
        x_lat = x_lat + gm_l * y_l
        if not last:
            q_c = mla_q(cq_c, q_norm[i], w_uq[i], q_gain[i], None, None)
            o_a_c = attention(q_c, k_c, v_c)
            o_c_c = rwkv_readout(feat_c, yf_c, yb_c, r_k[i], lnx_w[i], lnx_b[i], g2[i], x.dtype)
            y_c = merge(gate_c, o_a_c, fourier_mix(f_c), o_c_c, *mix_w)
            x_ctx = x_ctx + gm_c * y_c

        ffn_p = (ffn_w1, ffn_w3, ffn_w2, router, moe_w1, moe_w3, moe_w2)
        u_l = modulate(rmsnorm(x_lat, norm2[i]), sh2_l, sc2_l)
        x_lat = x_lat + gf_l * channel_mixer(i, u_l, *ffn_p)
        if not last:
            u_c = modulate(rmsnorm(x_ctx, norm2[i]), sh2_c, sc2_c)
            x_ctx = x_ctx + gf_c * channel_mixer(i, u_c, *ffn_p)
    return x_lat
```

```python
import functools
import math

import jax
import jax.numpy as jnp
import numpy as np
from jax import lax
from jax.experimental import pallas as pl
from jax.experimental.pallas import tpu as pltpu

F32 = jnp.float32
BF16 = jnp.bfloat16

D_MODEL = 4096
GRID_W = 64
NORM_EPS = 1e-6
MLA_HEADS = 16
Q_LORA = 1024
KV_LORA = 512
QK_NOPE = 128
QK_ROPE = 64
QK_HEAD = QK_NOPE + QK_ROPE
V_HEAD = 128
HEAD_PAD = 256
ROPE_AXIS_DIM = QK_ROPE // 2
ROPE_PAIRS = ROPE_AXIS_DIM // 2
ROPE_BASE = 10000.0
FNET_GROUPS = 4
FNET_GROUP_DIM = 256
FNET_WIDTH = FNET_GROUPS * FNET_GROUP_DIM
RWKV_HEADS = 16
RWKV_HEAD_DIM = 64
RWKV_WIDTH = RWKV_HEADS * RWKV_HEAD_DIM
DECAY_LORA = 128
ICLR_LORA = 128
VRES_LORA = 96
GATE_LORA = 480
GATE_PAD = 512
VRES_PAD = 128
LNX_EPS = 64e-5
N_BRANCHES = 3
MLA_IN = Q_LORA + KV_LORA + QK_ROPE
MLA_IN_PAD = Q_LORA + KV_LORA + 128
Z_MAIN = 3 * RWKV_WIDTH + 2 * DECAY_LORA + 2 * ICLR_LORA
Z_PAD = Z_MAIN + GATE_PAD + VRES_PAD
D_FF = 11008
D_FF_PAD = 11264
N_EXPERTS = 8
TOP_K = 2
D_FF_EXPERT = 3072

LANE = 128
CHUNK = 64
SCAN_BLOCK = 256
VMEM_LIMIT = 52 * 1024 * 1024


def _cparams(sem):
    return pltpu.CompilerParams(dimension_semantics=sem, vmem_limit_bytes=VMEM_LIMIT)


def _pick(n, cands):
    for c in cands:
        if n % c == 0:
            return c
    raise ValueError(f"no tile for {n} in {cands}")


def _silu(x):
    return x * (1.0 / (1.0 + jnp.exp(-x)))


def _sigmoid(x):
    return 1.0 / (1.0 + jnp.exp(-x))


def _bdot(a, b):
    return jnp.dot(a.astype(BF16), b.astype(BF16), preferred_element_type=F32)


def _split2(x):
    hi = x.astype(BF16)
    lo = (x - hi.astype(F32)).astype(BF16)
    return hi, lo


def _split3(x):
    hi = x.astype(BF16)
    r1 = x - hi.astype(F32)
    mid = r1.astype(BF16)
    lo = (r1 - mid.astype(F32)).astype(BF16)
    return hi, mid, lo


def _dot_x3(a, b):
    ah, al = _split2(a)
    bh, bl = _split2(b)
    d = functools.partial(jnp.dot, preferred_element_type=F32)
    return d(ah, bh) + (d(ah, bl) + d(al, bh))


def _dot_exact_rhs(a, b_bf16):
    h, m, l = _split3(a)
    d = functools.partial(jnp.dot, preferred_element_type=F32)
    return d(h, b_bf16) + (d(m, b_bf16) + d(l, b_bf16))


def _fmm_kernel(*refs, n_prod, n_extra, n_out, nk, kaxis, epilogue):
    a = refs[:n_prod]
    b = refs[n_prod:2 * n_prod]
    ex = refs[2 * n_prod:2 * n_prod + n_extra]
    outs = refs[2 * n_prod + n_extra:2 * n_prod + n_extra + n_out]
    accs = refs[2 * n_prod + n_extra + n_out:]
    pids = [pl.program_id(ax) for ax in range(kaxis + 1)]
    prods = [jnp.dot(a[i][...], b[i][...], preferred_element_type=F32) for i in range(n_prod)]
    if nk == 1:
        epilogue(prods, ex, outs, pids)
        return
    k = pids[kaxis]

    @pl.when(k == 0)
    def _():
        for i in range(n_prod):
            accs[i][...] = prods[i]

    @pl.when(k > 0)
    def _():
        for i in range(n_prod):
            accs[i][...] += prods[i]

    @pl.when(k == nk - 1)
    def _():
        epilogue([acc[...] for acc in accs], ex, outs, pids)


def _fmm(grid, a_list, a_specs, b_list, b_specs, extras, extra_specs, out_shapes, out_specs,
         acc_shape, nk, epilogue, aliases=None):
    n_prod = len(a_list)
    kern = functools.partial(_fmm_kernel, n_prod=n_prod, n_extra=len(extras), n_out=len(out_shapes),
                             nk=nk, kaxis=len(grid) - 1, epilogue=epilogue)
    scratch = [pltpu.VMEM(acc_shape, F32) for _ in range(n_prod)] if nk > 1 else []
    sem = ("parallel",) * (len(grid) - 1) + ("arbitrary",)
    return pl.pallas_call(
        kern,
        out_shape=out_shapes,
        grid=grid,
        in_specs=list(a_specs) + list(b_specs) + list(extra_specs),
        out_specs=out_specs,
        scratch_shapes=scratch,
        input_output_aliases=aliases or {},
        compiler_params=_cparams(sem),
    )(*a_list, *b_list, *extras)


def _flat_mm(a, b, out_dtype, tm, tn, tk, epilogue=None, extras=(), extra_specs=(), n_b=1, bs=None,
             out_shapes=None, out_specs=None, aliases=None):
    m, kdim = a.shape
    bs = bs if bs is not None else [b]
    n = bs[0].shape[1]
    nk = kdim // tk
    grid = (m // tm, n // tn, nk)
    a_specs = [pl.BlockSpec((tm, tk), lambda i, j, k: (i, k))] * len(bs)
    b_specs = [pl.BlockSpec((tk, tn), lambda i, j, k: (k, j))] * len(bs)
    if epilogue is None:
        def epilogue(p, ex, outs, pids):
            outs[0][...] = p[0].astype(outs[0].dtype)
    if out_shapes is None:
        out_shapes = [jax.ShapeDtypeStruct((m, n), out_dtype)]
        out_specs = [pl.BlockSpec((tm, tn), lambda i, j, k: (i, j))]
    return _fmm(grid, [a] * len(bs), a_specs, bs, b_specs, list(extras), list(extra_specs), out_shapes,
                out_specs, (tm, tn), nk, epilogue, aliases)


def _adaln_kernel(c_ref, w_ref, b_ref, o_ref):
    c = _silu(c_ref[...]).astype(BF16)
    o_ref[...] = jnp.dot(c, w_ref[...].astype(BF16), preferred_element_type=F32) + b_ref[...]


def _adaln(cond8, w, bias):
    d, n = w.shape
    tn = 512
    return pl.pallas_call(
        _adaln_kernel,
        out_shape=jax.ShapeDtypeStruct((8, n), F32),
        grid=(n // tn,),
        in_specs=[pl.BlockSpec((8, d), lambda j: (0, 0)),
                  pl.BlockSpec((d, tn), lambda j: (0, j)),
                  pl.BlockSpec((1, tn), lambda j: (0, j))],
        out_specs=pl.BlockSpec((8, tn), lambda j: (0, j)),
        compiler_params=_cparams(("parallel",)),
    )(cond8, w, bias.reshape(1, n))


def _ctx_rows(tile_idx, tm, tiles_per_batch, seq):
    rows = (tile_idx % tiles_per_batch) * tm + lax.broadcasted_iota(jnp.int32, (tm, 1), 0)
    return rows >= seq


def _norm_mod_kernel(x_ref, g_ref, shl_ref, scl_ref, shc_ref, scc_ref, o_ref, *, tm, tpb, seq):
    x = x_ref[...]
    y = x * lax.rsqrt(jnp.mean(x * x, axis=-1, keepdims=True) + NORM_EPS) * g_ref[...]
    ctx = _ctx_rows(pl.program_id(0), tm, tpb, seq)
    sh = jnp.where(ctx, shc_ref[...], shl_ref[...])
    sc = jnp.where(ctx, scc_ref[...], scl_ref[...])
    o_ref[...] = (y * (1.0 + sc) + sh).astype(o_ref.dtype)


def _norm_mod(x, g, sh, sc, nb, tt, seq):
    m, d = x.shape
    tm = _pick(tt, (272, 256, 128))
    tpb = tt // tm
    lat = pl.BlockSpec((None, 1, d), lambda i: (i // tpb, 0, 0))
    ctx = pl.BlockSpec((None, 1, d), lambda i: (nb, 0, 0))
    return pl.pallas_call(
        functools.partial(_norm_mod_kernel, tm=tm, tpb=tpb, seq=seq),
        out_shape=jax.ShapeDtypeStruct((m, d), BF16),
        grid=(m // tm,),
        in_specs=[pl.BlockSpec((tm, d), lambda i: (i, 0)), pl.BlockSpec((1, d), lambda i: (0, 0)),
                  lat, lat, ctx, ctx],
        out_specs=pl.BlockSpec((tm, d), lambda i: (i, 0)),
        compiler_params=_cparams(("parallel",)),
    )(x, g.reshape(1, d), sh, sc, sh, sc)


def _rope128(x, cos, sin):
    lane = lax.broadcasted_iota(jnp.int32, x.shape, 1)
    first = (lane % (2 * ROPE_PAIRS)) < ROPE_PAIRS
    swapped = jnp.where(first, pltpu.roll(x, LANE - ROPE_PAIRS, 1), pltpu.roll(x, ROPE_PAIRS, 1))
    return x * cos + swapped * sin


def _head_norm_rope(x_lo, x_hi, gain_lo, gain_hi, cos, sin, scale):
    ss = jnp.sum(x_lo * x_lo, axis=-1, keepdims=True) + jnp.sum(x_hi * x_hi, axis=-1, keepdims=True)
    inv = lax.rsqrt(ss * (1.0 / QK_HEAD) + NORM_EPS)
    lo = x_lo * inv * gain_lo
    hi = _rope128(x_hi * inv * gain_hi, cos, sin)
    return lo * scale, hi * scale


def _mla_q_kernel(a_ref, g_ref, w_ref, gain_ref, cos_ref, sin_ref, q_ref, *, heads):
    a = a_ref[...]
    an = a * lax.rsqrt(jnp.mean(a * a, axis=-1, keepdims=True) + NORM_EPS) * g_ref[...]
    acc = jnp.dot(an.astype(BF16), w_ref[...], preferred_element_type=F32)
    cos, sin = cos_ref[...], sin_ref[...]
    gain = gain_ref[...]
    for h in range(heads):
        c0 = h * HEAD_PAD
        lo, hi = _head_norm_rope(acc[:, c0:c0 + LANE], acc[:, c0 + LANE:c0 + HEAD_PAD], gain[:, :LANE],
                                 gain[:, LANE:], cos, sin, QK_HEAD ** -0.5)
        q_ref[:, c0:c0 + LANE] = lo.astype(q_ref.dtype)
        q_ref[:, c0 + LANE:c0 + HEAD_PAD] = hi.astype(q_ref.dtype)


def _mla_kv_kernel(a_ref, g_ref, w_ref, kr_ref, gain_ref, cos_ref, sin_ref, k_ref, v_ref, *, heads):
    a = a_ref[...]
    an = a * lax.rsqrt(jnp.mean(a * a, axis=-1, keepdims=True) + NORM_EPS) * g_ref[...]
    acc = jnp.dot(an.astype(BF16), w_ref[...], preferred_element_type=F32)
    cos, sin = cos_ref[...], sin_ref[...]
    gain = gain_ref[...]
    kr = kr_ref[...]
    for h in range(heads):
        c0 = h * HEAD_PAD
        lo, hi = _head_norm_rope(acc[:, c0:c0 + LANE], kr, gain[:, :LANE], gain[:, LANE:], cos, sin, 1.0)
        k_ref[:, c0:c0 + LANE] = lo.astype(k_ref.dtype)
        k_ref[:, c0 + LANE:c0 + HEAD_PAD] = hi.astype(k_ref.dtype)
        v_ref[:, h * V_HEAD:(h + 1) * V_HEAD] = acc[:, c0 + LANE:c0 + HEAD_PAD].astype(v_ref.dtype)


def _mla_up(mla_in, q_norm, w_uq, q_gain, kv_norm, w_ukv, k_gain, cos, sin):
    m = mla_in.shape[0]
    tm = _pick(m, (544, 512, 256))
    hpt = 4
    tn = hpt * HEAD_PAD
    grid = (m // tm, MLA_HEADS // hpt)
    row = lambda w: pl.BlockSpec((tm, w), lambda i, j: (i, 0))
    q = pl.pallas_call(
        functools.partial(_mla_q_kernel, heads=hpt),
        out_shape=jax.ShapeDtypeStruct((m, MLA_HEADS * HEAD_PAD), BF16),
        grid=grid,
        in_specs=[pl.BlockSpec((tm, Q_LORA), lambda i, j: (i, 0)),
                  pl.BlockSpec((1, Q_LORA), lambda i, j: (0, 0)),
                  pl.BlockSpec((Q_LORA, tn), lambda i, j: (0, j)),
                  pl.BlockSpec((1, HEAD_PAD), lambda i, j: (0, 0)),
                  row(LANE), row(LANE)],
        out_specs=pl.BlockSpec((tm, tn), lambda i, j: (i, j)),
        compiler_params=_cparams(("parallel", "parallel")),
    )(mla_in, q_norm.reshape(1, -1), w_uq, q_gain, cos, sin)
    k, v = pl.pallas_call(
        functools.partial(_mla_kv_kernel, heads=hpt),
        out_shape=[jax.ShapeDtypeStruct((m, MLA_HEADS * HEAD_PAD), BF16),
                   jax.ShapeDtypeStruct((m, MLA_HEADS * V_HEAD), BF16)],
        grid=grid,
        in_specs=[pl.BlockSpec((tm, KV_LORA), lambda i, j: (i, Q_LORA // KV_LORA)),
                  pl.BlockSpec((1, KV_LORA), lambda i, j: (0, 0)),
                  pl.BlockSpec((KV_LORA, tn), lambda i, j: (0, j)),
                  pl.BlockSpec((tm, LANE), lambda i, j: (i, (Q_LORA + KV_LORA) // LANE)),
                  pl.BlockSpec((1, HEAD_PAD), lambda i, j: (0, 0)),
                  row(LANE), row(LANE)],
        out_specs=[pl.BlockSpec((tm, tn), lambda i, j: (i, j)),
                   pl.BlockSpec((tm, hpt * V_HEAD), lambda i, j: (i, j))],
        compiler_params=_cparams(("parallel", "parallel")),
    )(mla_in, kv_norm.reshape(1, -1), w_ukv, mla_in, k_gain, cos, sin)
    return q, k, v


def _attn_kernel(q_ref, k_ref, v_ref, o_ref, *, n_lat_blocks, seq):
    s = lax.dot_general(q_ref[...], k_ref[...], (((1,), (1,)), ((), ())), preferred_element_type=F32)
    is_ctx = pl.program_id(2) >= n_lat_blocks
    col = lax.broadcasted_iota(jnp.int32, s.shape, 1)
    s = jnp.where(jnp.logical_and(is_ctx, col < seq), -1e30, s)
    m = jnp.max(s, axis=-1, keepdims=True)
    p = jnp.exp(s - m)
    l = jnp.sum(p, axis=-1, keepdims=True)
    o = jnp.dot(p.astype(BF16), v_ref[...], preferred_element_type=F32)
    o_ref[...] = (o * (1.0 / l)).astype(o_ref.dtype)


def _attention(q, k, v, nb, tt, seq):
    tq = 256
    q3 = q.reshape(nb, tt, MLA_HEADS * HEAD_PAD)
    k3 = k.reshape(nb, tt, MLA_HEADS * HEAD_PAD)
    v3 = v.reshape(nb, tt, MLA_HEADS * V_HEAD)
    out = pl.pallas_call(
        functools.partial(_attn_kernel, n_lat_blocks=seq // tq, seq=seq),
        out_shape=jax.ShapeDtypeStruct((nb, tt, MLA_HEADS * V_HEAD), BF16),
        grid=(nb, MLA_HEADS, tt // tq),
        in_specs=[pl.BlockSpec((None, tq, HEAD_PAD), lambda b, h, i: (b, i, h)),
                  pl.BlockSpec((None, tt, HEAD_PAD), lambda b, h, i: (b, 0, h)),
                  pl.BlockSpec((None, tt, V_HEAD), lambda b, h, i: (b, 0, h))],
        out_specs=pl.BlockSpec((None, tq, V_HEAD), lambda b, h, i: (b, i, h)),
        compiler_params=_cparams(("parallel", "parallel", "arbitrary")),
    )(q3, k3, v3)
    return out.reshape(nb * tt, MLA_HEADS * V_HEAD)


def _dft_mats(n, scale):
    idx = jnp.arange(n, dtype=jnp.int32)
    ang = ((idx[:, None] * idx[None, :]) % n).astype(F32) * (2.0 * math.pi / n)
    return (jnp.cos(ang) * scale).astype(BF16), (jnp.sin(ang) * scale).astype(BF16)


def _fourier(f, nb, tt, seq, with_ctx, dft):
    c_t, sneg_t, c_c, s_c, c_x, s_x = dft
    m = f.shape[0]
    fz = f.reshape(m * FNET_GROUPS, FNET_GROUP_DIM)
    rows = fz.shape[0]
    tmz = _pick(rows, (4352, 2048, 1024))

    def ep2(p, ex, outs, pids):
        outs[0][...] = p[0].astype(BF16)
        outs[1][...] = p[1].astype(BF16)

    shp = jax.ShapeDtypeStruct((rows, FNET_GROUP_DIM), BF16)
    ospec = pl.BlockSpec((tmz, FNET_GROUP_DIM), lambda i, j, k: (i, j))
    zc, zs = _flat_mm(fz, None, BF16, tmz, FNET_GROUP_DIM, FNET_GROUP_DIM, epilogue=ep2, bs=[c_c, s_c],
                      out_shapes=[shp, shp], out_specs=[ospec, ospec])
    zc = zc.reshape(nb, tt, FNET_WIDTH)
    zs = zs.reshape(nb, tt, FNET_WIDTH)

    def ep_sum(p, ex, outs, pids):
        outs[0][...] = (p[0] + p[1]).astype(BF16)

    tm = _pick(seq, (1024, 512, 256))
    tk = tm
    tn = FNET_WIDTH
    nk = seq // tk
    aspec = pl.BlockSpec((tm, tk), lambda b, i, j, k: (i, k))
    bspec = pl.BlockSpec((None, tk, tn), lambda b, i, j, k: (b, k, j))
    y_lat = _fmm((nb, seq // tm, FNET_WIDTH // tn, nk), [c_t, sneg_t], [aspec, aspec], [zc, zs], [bspec, bspec],
                 [], [], [jax.ShapeDtypeStruct((nb, seq, FNET_WIDTH), BF16)],
                 [pl.BlockSpec((None, tm, tn), lambda b, i, j, k: (b, i, j))], (tm, tn), nk, ep_sum)[0]
    nctx = tt - seq
    if with_ctx:
        def ep_diff(p, ex, outs, pids):
            outs[0][...] = (p[0] - p[1]).astype(BF16)

        cb = seq // nctx
        aspec = pl.BlockSpec((nctx, nctx), lambda b, k: (0, 0))
        bspec = pl.BlockSpec((None, nctx, FNET_WIDTH), lambda b, k: (b, cb, 0))
        y_ctx = _fmm((nb, 1), [c_x, s_x], [aspec, aspec], [zc, zs], [bspec, bspec], [], [],
                     [jax.ShapeDtypeStruct((nb, nctx, FNET_WIDTH), BF16)],
                     [pl.BlockSpec((None, nctx, FNET_WIDTH), lambda b, k: (b, 0, 0))],
                     (nctx, FNET_WIDTH), 1, ep_diff)[0]
    else:
        y_ctx = jnp.zeros((nb, nctx, FNET_WIDTH), BF16)
    return jnp.concatenate([y_lat, y_ctx], axis=1).reshape(nb * tt, FNET_WIDTH)


def _head_sum(x, ones_bd):
    parts = []
    for s in range(x.shape[1] // LANE):
        hi, lo = _split2(x[:, s * LANE:(s + 1) * LANE])
        parts.append(jnp.dot(hi, ones_bd, preferred_element_type=F32)
                     + jnp.dot(lo, ones_bd, preferred_element_type=F32))
    return jnp.concatenate(parts, axis=-1)


def _feat_kernel(*refs, tf, tpb, lat_tiles, has_vres):
    (z_ref, zp_ref, zn_ref, mu_ref, w0_ref, w2_ref, a0_ref, a2_ref, kk_ref_p, ka_ref, bd_ref) = refs[:11]
    pos = 11
    if has_vres:
        vf_ref, v0_ref, v2_ref = refs[pos:pos + 3]
        pos += 3
    outs = refs[pos:]
    r_o, v_o, kk_o, kf_o, kb_o, lwf_o, lwb_o, af_o, ab_o, sg_o = outs[:10]
    j = pl.program_id(1)
    has_prev = jnp.logical_and(j != 0, j != lat_tiles)
    has_next = jnp.logical_and(j != lat_tiles - 1, j != tpb - 1)
    row = lax.broadcasted_iota(jnp.int32, (tf, 1), 0)

    def zf(lo, hi):
        z = z_ref[:, lo:hi]
        prev_row = jnp.where(has_prev, zp_ref[7:8, lo:hi], 0.0)
        next_row = jnp.where(has_next, zn_ref[0:1, lo:hi], 0.0)
        zp = jnp.where(row == 0, prev_row, pltpu.roll(z, 1, 0))
        zn = jnp.where(row == tf - 1, next_row, pltpu.roll(z, tf - 1, 0))
        return z + mu_ref[:, lo:hi] * (0.5 * (zp + zn) - z)

    w = RWKV_WIDTH
    r = zf(0, w)
    k = zf(w, 2 * w)
    v = zf(2 * w, 3 * w)
    r_o[...] = r
    if has_vres:
        vl = zf(Z_MAIN + GATE_PAD, Z_PAD)
        gate = _sigmoid(v0_ref[...] + _bdot(vl, v2_ref[...]))
        v_o[...] = v + (vf_ref[...] - v) * gate
    else:
        v_o[...] = v
        outs[10][...] = v
    kk = k * kk_ref_p[...]
    ss = _head_sum(kk * kk, bd_ref[...])
    kk_o[...] = kk * lax.rsqrt(ss + 1e-12)
    base = 3 * w
    for d, (lw_o, a_o, k_o) in enumerate(((lwf_o, af_o, kf_o), (lwb_o, ab_o, kb_o))):
        wd = zf(base + d * DECAY_LORA, base + (d + 1) * DECAY_LORA)
        ad = zf(base + 2 * DECAY_LORA + d * ICLR_LORA, base + 2 * DECAY_LORA + (d + 1) * ICLR_LORA)
        xw = w0_ref[d:d + 1, :] + _bdot(jnp.tanh(wd), w2_ref[d])
        lw_o[...] = -_sigmoid(xw) * math.exp(-0.5)
        a = _sigmoid(a0_ref[d:d + 1, :] + _bdot(ad, a2_ref[d]))
        a_o[...] = a
        k_o[...] = k * (1.0 + (a - 1.0) * ka_ref[...])
    sg_o[...] = _sigmoid(zf(Z_MAIN, Z_MAIN + GATE_PAD)).astype(sg_o.dtype)


def _rwkv_features(z, nb, tt, seq, mu, w0, w2, a0, a2, k_k, k_a, ones_bd, vres):
    tf = 128
    tpb = tt // tf
    lat_tiles = seq // tf
    z3 = z.reshape(nb, tt, Z_PAD)
    hb = tf // 8
    last8 = tt // 8 - 1
    w = RWKV_WIDTH
    full = lambda shape: pl.BlockSpec(shape, lambda b, j: (0,) * len(shape))
    tile = lambda width: pl.BlockSpec((None, tf, width), lambda b, j: (b, j, 0))
    in_specs = [tile(Z_PAD),
                pl.BlockSpec((None, 8, Z_PAD), lambda b, j: (b, jnp.maximum(j * hb - 1, 0), 0)),
                pl.BlockSpec((None, 8, Z_PAD), lambda b, j: (b, jnp.minimum((j + 1) * hb, last8), 0)),
                full((1, Z_PAD)), full((2, w)), full((2, DECAY_LORA, w)), full((2, w)), full((2, ICLR_LORA, w)),
                full((1, w)), full((1, w)), full((LANE, LANE))]
    args = [z3, z3, z3, mu.reshape(1, Z_PAD), w0, w2, a0, a2, k_k.reshape(1, w), k_a.reshape(1, w), ones_bd]
    has_vres = vres is not None
    if has_vres:
        v_first, v0, v2 = vres
        in_specs += [tile(w), full((1, w)), full((VRES_PAD, w))]
        args += [v_first.reshape(nb, tt, w), v0.reshape(1, w), v2]
    f3 = jax.ShapeDtypeStruct((nb, tt, w), F32)
    out_shape = [f3] * 9 + [jax.ShapeDtypeStruct((nb, tt, GATE_PAD), BF16)]
    out_specs = [tile(w)] * 9 + [tile(GATE_PAD)]
    if not has_vres:
        out_shape.append(f3)
        out_specs.append(tile(w))
    return pl.pallas_call(
        functools.partial(_feat_kernel, tf=tf, tpb=tpb, lat_tiles=lat_tiles, has_vres=has_vres),
        out_shape=out_shape,
        grid=(nb, tpb),
        in_specs=in_specs,
        out_specs=out_specs,
        compiler_params=_cparams(("parallel", "arbitrary")),
    )(*args)


def _stack2(x, lo_mask):
    return jnp.concatenate([jnp.where(lo_mask, x, 0.0), jnp.where(lo_mask, 0.0, x)], axis=0)


def _scan_chunk(r, k, v, lw, kk, a, h0, rev, consts):
    tri_cum, strict, incl, lo_mask, eye = consts
    c = r.shape[0]
    cum = _dot_exact_rhs_left(tri_cum, lw)
    tot = cum[0:1, :] if rev else cum[c - 1:c, :]
    g_in = jnp.exp(cum)
    g_ex = jnp.exp(cum - lw)
    g_inv = jnp.exp(-cum)
    g_tail = jnp.exp(tot - cum)
    b = kk * a
    s2 = lambda x: _stack2(x, lo_mask)
    kkt = s2(kk * g_ex)
    rt = s2(r * g_in)
    bh = s2(b * g_inv)
    kh = s2(k * g_inv)
    kb = s2(k * g_tail)
    bb = s2(b * g_tail)
    v2 = s2(v)
    nt = (((1,), (1,)), ((), ()))
    tn = (((0,), (0,)), ((), ()))
    lhs = jnp.concatenate([kkt, rt], axis=0)
    rhs = jnp.concatenate([bh, kh], axis=0)
    big = _dg_x3(lhs, rhs, nt)
    n2c = 2 * c
    m_b = jnp.where(strict, big[:n2c, :n2c], 0.0)
    m_k = jnp.where(strict, big[:n2c, n2c:], 0.0)
    a_qb = jnp.where(incl, big[n2c:, :n2c], 0.0)
    a_qk = jnp.where(incl, big[n2c:, n2c:], 0.0)
    n = -m_b
    tinv = eye + n
    steps = int(math.log2(c)) - 1
    for _ in range(steps):
        n = _dot_x3(n, n)
        tinv = tinv + _dot_x3(tinv, n)
    mkv = _dot_x3(m_k, v2)
    wu = _dot_x3(tinv, jnp.concatenate([kkt, mkv], axis=1))
    pg = _dg_x3(bb, wu, tn)
    p_mat = eye * jnp.exp(tot) - pg[:, :LANE]
    g_mat = _dg_x3(kb, v2, tn) - pg[:, LANE:]
    qy = _dot_x3(a_qb, wu)
    q_mat = rt - qy[:, :LANE]
    y0 = _dot_x3(a_qk, v2) - qy[:, LANE:]
    y_st = _dot_x3(q_mat, h0) + y0
    h1 = _dot_x3(p_mat, h0) + g_mat
    return y_st[:c] + y_st[c:], h1


def _dg_x3(a, b, dims):
    ah, al = _split2(a)
    bh, bl = _split2(b)
    d = functools.partial(lax.dot_general, dimension_numbers=dims, preferred_element_type=F32)
    return d(ah, bh) + (d(ah, bl) + d(al, bh))


def _dot_exact_rhs_left(tri_bf16, x):
    h, m, l = _split3(x)
    d = functools.partial(jnp.dot, preferred_element_type=F32)
    return d(tri_bf16, h) + (d(tri_bf16, m) + d(tri_bf16, l))


def _scan_kernel(r_ref, k_ref, v_ref, lw_ref, kk_ref, a_ref, y_ref, h_ref, *, rev, n_chunks):
    @pl.when(pl.program_id(2) == 0)
    def _():
        h_ref[...] = jnp.zeros_like(h_ref)

    c = CHUNK
    ri = lax.broadcasted_iota(jnp.int32, (c, c), 0)
    ci = lax.broadcasted_iota(jnp.int32, (c, c), 1)
    tri_cum = (ci >= ri if rev else ci <= ri).astype(BF16)
    r2 = lax.broadcasted_iota(jnp.int32, (2 * c, 2 * c), 0)
    c2 = lax.broadcasted_iota(jnp.int32, (2 * c, 2 * c), 1)
    strict = c2 > r2 if rev else c2 < r2
    incl = c2 >= r2 if rev else c2 <= r2
    eye = (r2 == c2).astype(F32)
    lo_mask = lax.broadcasted_iota(jnp.int32, (c, LANE), 1) < RWKV_HEAD_DIM
    consts = (tri_cum, strict, incl, lo_mask, eye)
    h = h_ref[...]
    order = range(n_chunks - 1, -1, -1) if rev else range(n_chunks)
    for ch in order:
        sl = slice(ch * c, (ch + 1) * c)
        y, h = _scan_chunk(r_ref[sl, :], k_ref[sl, :], v_ref[sl, :], lw_ref[sl, :], kk_ref[sl, :], a_ref[sl, :],
                           h, rev, consts)
        y_ref[sl, :] = y
    h_ref[...] = h


def _rwkv_scan(r, k, v, lw, kk, a, rev, seq):
    nb, tt, w = r.shape
    nblk = tt // SCAN_BLOCK
    lat_blk = seq // SCAN_BLOCK
    if rev:
        def blk(j):
            return nblk - 1 - j
    else:
        def blk(j):
            return jnp.where(j < nblk - lat_blk, lat_blk + j, j - (nblk - lat_blk))
    spec = pl.BlockSpec((None, SCAN_BLOCK, LANE), lambda b, p, j: (b, blk(j), p))
    return pl.pallas_call(
        functools.partial(_scan_kernel, rev=rev, n_chunks=SCAN_BLOCK // CHUNK),
        out_shape=jax.ShapeDtypeStruct((nb, tt, w), F32),
        grid=(nb, w // LANE, nblk),
        in_specs=[spec] * 6,
        out_specs=spec,
        scratch_shapes=[pltpu.VMEM((LANE, LANE), F32)],
        compiler_params=_cparams(("parallel", "parallel", "arbitrary")),
    )(r, k, v, lw, kk, a)


def _readout_kernel(yf_ref, yb_ref, r_ref, kf_ref, kb_ref, v_ref, sg_ref, g2_ref, rk_ref, lw_ref, lb_ref, bd_ref,
                    o_ref):
    bd = bd_ref[...]
    inv = 1.0 / RWKV_HEAD_DIM
    y = yf_ref[...] + yb_ref[...]
    mean = _head_sum(y, bd) * inv
    dlt = y - mean
    var = _head_sum(dlt * dlt, bd) * inv
    yn = dlt * lax.rsqrt(var + LNX_EPS) * lw_ref[...] + lb_ref[...]
    bonus = _head_sum(r_ref[...] * (kf_ref[...] + kb_ref[...]) * rk_ref[...], bd) * v_ref[...]
    g = jnp.dot(sg_ref[...], g2_ref[...], preferred_element_type=F32)
    o_ref[...] = ((yn + bonus) * g).astype(o_ref.dtype)


def _rwkv_readout(yf, yb, r, kf, kb, v, sg, g2, r_k, lnx_w, lnx_b, ones_bd):
    m, w = yf.shape
    tm = _pick(m, (272, 256, 128))
    row = lambda width: pl.BlockSpec((tm, width), lambda i: (i, 0))
    full = lambda shape: pl.BlockSpec(shape, lambda i: (0,) * len(shape))
    return pl.pallas_call(
        _readout_kernel,
        out_shape=jax.ShapeDtypeStruct((m, w), BF16),
        grid=(m // tm,),
        in_specs=[row(w)] * 6 + [row(GATE_PAD), full((GATE_PAD, w)), full((1, w)), full((1, w)), full((1, w)),
                                 full((LANE, LANE))],
        out_specs=row(w),
        compiler_params=_cparams(("parallel",)),
    )(yf, yb, r, kf, kb, v, sg, g2, r_k.reshape(1, w), lnx_w.reshape(1, w), lnx_b.reshape(1, w), ones_bd)


def _merge(gates, o_a, o_b, o_c, w_a, w_b, w_c):
    m = o_a.shape[0]
    d = w_a.shape[1]
    tm = _pick(m, (1088, 512, 256))
    tn = 512

    def ep(p, ex, outs, pids):
        y = (_sigmoid(ex[0][...].astype(F32)) * p[0] + _sigmoid(ex[1][...].astype(F32)) * p[1]
             + _sigmoid(ex[2][...].astype(F32)) * p[2])
        outs[0][...] = y.astype(BF16)

    nbn = d // tn
    a_specs = [pl.BlockSpec((tm, o.shape[1]), lambda i, j, k: (i, 0)) for o in (o_a, o_b, o_c)]
    b_specs = [pl.BlockSpec((wt.shape[0], tn), lambda i, j, k: (0, j)) for wt in (w_a, w_b, w_c)]
    g_specs = [pl.BlockSpec((tm, tn), functools.partial(lambda i, j, k, off: (i, off + j), off=br * nbn))
               for br in range(N_BRANCHES)]
    return _fmm((m // tm, nbn, 1), [o_a, o_b, o_c], a_specs, [w_a, w_b, w_c], b_specs, [gates] * 3, g_specs,
                [jax.ShapeDtypeStruct((m, d), BF16)], [pl.BlockSpec((tm, tn), lambda i, j, k: (i, j))],
                (tm, tn), 1, ep)[0]


def _mm_resid(a, w, x, gate, nb, tt, seq, tk):
    m, kdim = a.shape
    d = w.shape[1]
    tm = _pick(tt, (1088, 512, 256))
    tpb = tt // tm
    tn = 1024
    nk = kdim // tk

    def ep(p, ex, outs, pids):
        ctx = _ctx_rows(pids[0], tm, tpb, seq)
        g = jnp.where(ctx, ex[2][...], ex[1][...])
        outs[0][...] = ex[0][...] + g * p[0]

    xspec = pl.BlockSpec((tm, tn), lambda i, j, k: (i, j))
    extra_specs = [xspec,
                   pl.BlockSpec((None, 1, tn), lambda i, j, k: (i // tpb, 0, j)),
                   pl.BlockSpec((None, 1, tn), lambda i, j, k: (nb, 0, j))]
    return _fmm((m // tm, d // tn, nk), [a], [pl.BlockSpec((tm, tk), lambda i, j, k: (i, k))],
                [w], [pl.BlockSpec((tk, tn), lambda i, j, k: (k, j))], [x, gate, gate], extra_specs,
                [jax.ShapeDtypeStruct((m, d), F32)], [xspec], (tm, tn), nk, ep, aliases={2: 0})[0]


def _swiglu(u, w1, w3, tn):
    m = u.shape[0]
    tm = _pick(m, (1088, 512, 256))

    def ep(p, ex, outs, pids):
        outs[0][...] = (_silu(p[0]) * p[1]).astype(BF16)

    return _flat_mm(u, None, BF16, tm, tn, 1024, epilogue=ep, bs=[w1, w3])[0]


def _router_kernel(u_ref, w_ref, o_ref):
    logits = _dot_x3(u_ref[...].astype(F32), w_ref[...])
    lane = lax.broadcasted_iota(jnp.int32, logits.shape, 1).astype(F32)
    valid = lane < N_EXPERTS
    neg = -1e30
    lg = jnp.where(valid, logits, neg)
    m1 = jnp.max(lg, axis=-1, keepdims=True)
    i1 = jnp.min(jnp.where(lg == m1, lane, float(LANE)), axis=-1, keepdims=True)
    lg2 = jnp.where(lane == i1, neg, lg)
    m2 = jnp.max(lg2, axis=-1, keepdims=True)
    i2 = jnp.min(jnp.where(lg2 == m2, lane, float(LANE)), axis=-1, keepdims=True)
    e2 = jnp.exp(m2 - m1)
    w1 = 1.0 / (1.0 + e2)
    w2 = e2 / (1.0 + e2)
    o_ref[...] = jnp.where(lane == i1, w1, 0.0) + jnp.where(lane == i2, w2, 0.0)


def _router(u, router_pad):
    m, d = u.shape
    tm = _pick(m, (544, 512, 256))
    return pl.pallas_call(
        _router_kernel,
        out_shape=jax.ShapeDtypeStruct((m, LANE), F32),
        grid=(m // tm,),
        in_specs=[pl.BlockSpec((tm, d), lambda i: (i, 0)), pl.BlockSpec((d, LANE), lambda i: (0, 0))],
        out_specs=pl.BlockSpec((tm, LANE), lambda i: (i, 0)),
        compiler_params=_cparams(("parallel",)),
    )(u, router_pad)


def _moe_down(h, w2, x, gate, comb, e, nb, tt, seq):
    m, kdim = h.shape
    d = w2.shape[1]
    tm = _pick(tt, (1088, 512, 256))
    tpb = tt // tm
    tn = 1024
    tk = 1024
    nk = kdim // tk

    def ep(p, ex, outs, pids):
        ctx = _ctx_rows(pids[0], tm, tpb, seq)
        g = jnp.where(ctx, ex[2][...], ex[1][...])
        cw = ex[3][...]
        lane = lax.broadcasted_iota(jnp.int32, cw.shape, 1)
        ce = jnp.sum(jnp.where(lane == e, cw, 0.0), axis=-1, keepdims=True)
        outs[0][...] = ex[0][...] + (g * ce) * p[0]

    xspec = pl.BlockSpec((tm, tn), lambda i, j, k: (i, j))
    extra_specs = [xspec,
                   pl.BlockSpec((None, 1, tn), lambda i, j, k: (i // tpb, 0, j)),
                   pl.BlockSpec((None, 1, tn), lambda i, j, k: (nb, 0, j)),
                   pl.BlockSpec((tm, LANE), lambda i, j, k: (i, 0))]
    return _fmm((m // tm, d // tn, nk), [h], [pl.BlockSpec((tm, tk), lambda i, j, k: (i, k))],
                [w2], [pl.BlockSpec((tk, tn), lambda i, j, k: (k, j))], [x, gate, gate, comb], extra_specs,
                [jax.ShapeDtypeStruct((m, d), F32)], [xspec], (tm, tn), nk, ep, aliases={2: 0})[0]


def _rope_lane_tables(seq, nctx, nb):
    rows = seq // GRID_W
    row = jnp.repeat(jnp.arange(rows, dtype=F32), GRID_W)
    col = jnp.tile(jnp.arange(GRID_W, dtype=F32), rows)
    inv_freq = ROPE_BASE ** (-2.0 * jnp.arange(ROPE_PAIRS, dtype=F32) / ROPE_AXIS_DIM)
    ar = row[:, None] * inv_freq
    ac = col[:, None] * inv_freq
    ones = jnp.ones((seq, LANE - QK_ROPE), F32)
    zeros = jnp.zeros((seq, LANE - QK_ROPE), F32)
    cos = jnp.concatenate([jnp.cos(ar), jnp.cos(ar), jnp.cos(ac), jnp.cos(ac), ones], axis=1)
    sin = jnp.concatenate([-jnp.sin(ar), jnp.sin(ar), -jnp.sin(ac), jnp.sin(ac), zeros], axis=1)
    cos = jnp.concatenate([cos, jnp.ones((nctx, LANE), F32)], axis=0)
    sin = jnp.concatenate([sin, jnp.zeros((nctx, LANE), F32)], axis=0)
    return jnp.tile(cos, (nb, 1)), jnp.tile(sin, (nb, 1))


def _pad_cols(w, n):
    return jnp.pad(w, ((0, 0), (0, n - w.shape[1])))


def _pad_rows(w, n):
    return jnp.pad(w, ((0, n - w.shape[0]), (0, 0)))


def _z_layout(zsrc, vres_src):
    lead = zsrc.shape[:-1]
    parts = [zsrc[..., :Z_MAIN], zsrc[..., Z_MAIN:], jnp.zeros(lead + (GATE_PAD - GATE_LORA,), zsrc.dtype)]
    if vres_src is None:
        parts.append(jnp.zeros(lead + (VRES_PAD,), zsrc.dtype))
    else:
        parts += [vres_src, jnp.zeros(lead + (VRES_PAD - VRES_LORA,), zsrc.dtype)]
    return jnp.concatenate(parts, axis=-1)


def _pad_head(g):
    return jnp.pad(g, (0, HEAD_PAD - QK_HEAD)).reshape(1, HEAD_PAD)


def kernel(x, c, ctx, c_ctx, ada_w, ada_b, norm1, w_in, w_vres_down, q_norm, w_uq, kv_norm, w_ukv, q_gain, k_gain,
           rwkv_mu, vres_mu, w0, w2, a0, a2, k_k, k_a, v0, v2, r_k, lnx_w, lnx_b, g2, w_br_a, w_br_b, w_br_c,
           w_out, norm2, ffn_w1, ffn_w3, ffn_w2, router, moe_w1, moe_w3, moe_w2):
    nb, seq, d = x.shape
    nctx = ctx.shape[1]
    tt = seq + nctx
    m = nb * tt
    depth = ada_w.shape[0]
    xa = jnp.concatenate([x, ctx], axis=1).reshape(m, d)
    cond8 = jnp.concatenate([c, c_ctx[None], jnp.zeros((8 - nb - 1, d), F32)], axis=0)
    cos_t, sin_t = _rope_lane_tables(seq, nctx, nb)
    c_t, s_t = _dft_mats(seq, 1.0 / math.sqrt(seq))
    c_c, s_c = _dft_mats(FNET_GROUP_DIM, 1.0 / math.sqrt(FNET_GROUP_DIM))
    c_x, s_x = _dft_mats(nctx, 1.0 / math.sqrt(nctx))
    dft = (c_t, -s_t, c_c, s_c, c_x, s_x)
    li = jnp.arange(LANE)
    ones_bd = (li[:, None] // RWKV_HEAD_DIM == li[None, :] // RWKV_HEAD_DIM).astype(BF16)
    v_first = None
    gate_cols = N_BRANCHES * d
    for i in range(depth):
        last = i == depth - 1
        mod = _adaln(cond8, ada_w[i], ada_b[i])[:nb + 1].reshape(nb + 1, 6, 1, d)
        sh1, sc1, gm, sh2, sc2, gf = (mod[:, t] for t in range(6))
        wi = w_in[i]
        w_gate = wi[:, :gate_cols].astype(BF16)
        w_mla = _pad_cols(wi[:, gate_cols:gate_cols + MLA_IN], MLA_IN_PAD).astype(BF16)
        f0 = gate_cols + MLA_IN
        w_f = wi[:, f0:f0 + FNET_WIDTH].astype(BF16)
        zsrc = wi[:, f0 + FNET_WIDTH:]
        if i == 0:
            w_z = _z_layout(zsrc, None).astype(BF16)
            mu_z = _z_layout(rwkv_mu[i], None)
        else:
            w_z = _z_layout(zsrc, w_vres_down[i - 1]).astype(BF16)
            mu_z = _z_layout(rwkv_mu[i], vres_mu[i - 1])

        u = _norm_mod(xa, norm1[i], sh1, sc1, nb, tt, seq)
        tm = _pick(m, (1088, 512, 256))
        gates = _flat_mm(u, w_gate, BF16, tm, 512, d)[0]
        mla_in = _flat_mm(u, w_mla, F32, _pick(m, (544, 512, 256)), MLA_IN_PAD, 2048)[0]
        f = _flat_mm(u, w_f, BF16, tm, 512, d)[0]
        z = _flat_mm(u, w_z, F32, _pick(m, (544, 512, 256)), Z_PAD // 3, d)[0]

        w_uq_p = jnp.pad(w_uq[i].reshape(Q_LORA, MLA_HEADS, QK_HEAD),
                         ((0, 0), (0, 0), (0, HEAD_PAD - QK_HEAD))).reshape(Q_LORA, MLA_HEADS * HEAD_PAD)
        q, k, v = _mla_up(mla_in, q_norm[i], w_uq_p.astype(BF16), _pad_head(q_gain[i]), kv_norm[i],
                          w_ukv[i].astype(BF16), _pad_head(k_gain[i]), cos_t, sin_t)
        o_a = _attention(q, k, v, nb, tt, seq)

        o_b = _fourier(f, nb, tt, seq, not last, dft)

        vres = None
        if i > 0:
            vres = (v_first, v0[i - 1], _pad_rows(v2[i - 1], VRES_PAD).astype(BF16))
        feat = _rwkv_features(z, nb, tt, seq, mu_z, w0[i], w2[i].astype(BF16), a0[i], a2[i].astype(BF16),
                              k_k[i], k_a[i], ones_bd, vres)
        r_, v_, kk_, kf_, kb_, lwf_, lwb_, af_, ab_, sg_ = feat[:10]
        if i == 0:
            v_first = feat[10]
        y_f = _rwkv_scan(r_, kf_, v_, lwf_, kk_, af_, False, seq)
        y_b = _rwkv_scan(r_, kb_, v_, lwb_, kk_, ab_, True, seq)
        flat = lambda t: t.reshape(m, t.shape[-1])
        o_c = _rwkv_readout(flat(y_f), flat(y_b), flat(r_), flat(kf_), flat(kb_), flat(v_), flat(sg_),
                            _pad_rows(g2[i], GATE_PAD).astype(BF16), r_k[i], lnx_w[i], lnx_b[i], ones_bd)

        mixed = _merge(gates, o_a, o_b, o_c, w_br_a[i].astype(BF16), w_br_b[i].astype(BF16),
                       w_br_c[i].astype(BF16))
        xa = _mm_resid(mixed, w_out[i].astype(BF16), xa, gm, nb, tt, seq, 1024)

        u2 = _norm_mod(xa, norm2[i], sh2, sc2, nb, tt, seq)
        jj = i // 2
        if i % 2 == 0:
            h = _swiglu(u2, _pad_cols(ffn_w1[jj], D_FF_PAD).astype(BF16),
                        _pad_cols(ffn_w3[jj], D_FF_PAD).astype(BF16), 1024)
            xa = _mm_resid(h, _pad_rows(ffn_w2[jj], D_FF_PAD).astype(BF16), xa, gf, nb, tt, seq, 1024)
        else:
            comb = _router(u2, _pad_cols(router[jj], LANE))
            for e in range(N_EXPERTS):
                h = _swiglu(u2, moe_w1[jj, e].astype(BF16), moe_w3[jj, e].astype(BF16), 1024)
                xa = _moe_down(h, moe_w2[jj, e].astype(BF16), xa, gf, comb, e, nb, tt, seq)
    return xa.reshape(nb, tt, d)[:, :seq]
```

```python
import functools
import math

import jax
import jax.numpy as jnp
import numpy as np
from jax import lax
from jax.experimental import pallas as pl
from jax.experimental.pallas import tpu as pltpu

F32 = jnp.float32
BF16 = jnp.bfloat16

D_MODEL = 4096
GRID_W = 64
NORM_EPS = 1e-6
MLA_HEADS = 16
Q_LORA = 1024
KV_LORA = 512
QK_NOPE = 128
QK_ROPE = 64
QK_HEAD = QK_NOPE + QK_ROPE
V_HEAD = 128
HEAD_PAD = 256
ROPE_AXIS_DIM = QK_ROPE // 2
ROPE_PAIRS = ROPE_AXIS_DIM // 2
ROPE_BASE = 10000.0
FNET_GROUPS = 4
FNET_GROUP_DIM = 256
FNET_WIDTH = FNET_GROUPS * FNET_GROUP_DIM
RWKV_HEADS = 16
RWKV_HEAD_DIM = 64
RWKV_WIDTH = RWKV_HEADS * RWKV_HEAD_DIM
DECAY_LORA = 128
ICLR_LORA = 128
VRES_LORA = 96
GATE_LORA = 480
GATE_PAD = 512
VRES_PAD = 128
LNX_EPS = 64e-5
N_BRANCHES = 3
MLA_IN = Q_LORA + KV_LORA + QK_ROPE
MLA_IN_PAD = Q_LORA + KV_LORA + 128
Z_MAIN = 3 * RWKV_WIDTH + 2 * DECAY_LORA + 2 * ICLR_LORA
Z_PAD = Z_MAIN + GATE_PAD + VRES_PAD
D_FF = 11008
D_FF_PAD = 11264
N_EXPERTS = 8
TOP_K = 2
D_FF_EXPERT = 3072

LANE = 128
CHUNK = 64
SCAN_BLOCK = 256
SCAN_GROUP = 2
SCAN_PAIRS = 2
INV_BASE = 16
VMEM_LIMIT = 52 * 1024 * 1024


def _cparams(sem):
    return pltpu.CompilerParams(dimension_semantics=sem, vmem_limit_bytes=VMEM_LIMIT)


def _pick(n, cands):
    for c in cands:
        if n % c == 0:
            return c
    raise ValueError(f"no tile for {n} in {cands}")


def _silu(x):
    return x * (1.0 / (1.0 + jnp.exp(-x)))


def _sigmoid(x):
    return 1.0 / (1.0 + jnp.exp(-x))


def _bdot(a, b):
    return jnp.dot(a.astype(BF16), b.astype(BF16), preferred_element_type=F32)


def _split2(x):
    hi = x.astype(BF16)
    lo = (x - hi.astype(F32)).astype(BF16)
    return hi, lo


def _split3(x):
    hi = x.astype(BF16)
    r1 = x - hi.astype(F32)
    mid = r1.astype(BF16)
    lo = (r1 - mid.astype(F32)).astype(BF16)
    return hi, mid, lo


def _dot_x3(a, b):
    ah, al = _split2(a)
    bh, bl = _split2(b)
    d = functools.partial(jnp.dot, preferred_element_type=F32)
    return d(ah, bh) + (d(ah, bl) + d(al, bh))


def _dot_exact_rhs(a, b_bf16):
    h, m, l = _split3(a)
    d = functools.partial(jnp.dot, preferred_element_type=F32)
    return d(h, b_bf16) + (d(m, b_bf16) + d(l, b_bf16))


def _fmm_kernel(*refs, n_prod, n_extra, n_out, nk, kaxis, epilogue):
    a = refs[:n_prod]
    b = refs[n_prod:2 * n_prod]
    ex = refs[2 * n_prod:2 * n_prod + n_extra]
    outs = refs[2 * n_prod + n_extra:2 * n_prod + n_extra + n_out]
    accs = refs[2 * n_prod + n_extra + n_out:]
    pids = [pl.program_id(ax) for ax in range(kaxis + 1)]
    prods = [jnp.dot(a[i][...], b[i][...].astype(BF16), preferred_element_type=F32) for i in range(n_prod)]
    if nk == 1:
        epilogue(prods, ex, outs, pids)
        return
    k = pids[kaxis]

    @pl.when(k == 0)
    def _():
        for i in range(n_prod):
            accs[i][...] = prods[i]

    @pl.when(k > 0)
    def _():
        for i in range(n_prod):
            accs[i][...] += prods[i]

    @pl.when(k == nk - 1)
    def _():
        epilogue([acc[...] for acc in accs], ex, outs, pids)


def _fmm(grid, a_list, a_specs, b_list, b_specs, extras, extra_specs, out_shapes, out_specs,
         acc_shape, nk, epilogue, aliases=None):
    n_prod = len(a_list)
    kern = functools.partial(_fmm_kernel, n_prod=n_prod, n_extra=len(extras), n_out=len(out_shapes),
                             nk=nk, kaxis=len(grid) - 1, epilogue=epilogue)
    scratch = [pltpu.VMEM(acc_shape, F32) for _ in range(n_prod)] if nk > 1 else []
    sem = ("parallel",) * (len(grid) - 1) + ("arbitrary",)
    return pl.pallas_call(
        kern,
        out_shape=out_shapes,
        grid=grid,
        in_specs=list(a_specs) + list(b_specs) + list(extra_specs),
        out_specs=out_specs,
        scratch_shapes=scratch,
        input_output_aliases=aliases or {},
        compiler_params=_cparams(sem),
    )(*a_list, *b_list, *extras)


def _wspec(w, tk, tn):
    if isinstance(w, tuple):
        arr, lead = w
        return arr, pl.BlockSpec((None,) * len(lead) + (tk, tn), lambda i, j, k: tuple(lead) + (k, j))
    return w, pl.BlockSpec((tk, tn), lambda i, j, k: (k, j))


def _flat_mm(a, b, out_dtype, tm, tn, tk, epilogue=None, extras=(), extra_specs=(), bs=None,
             out_shapes=None, out_specs=None, aliases=None, n=None):
    m, kdim = a.shape
    bs = bs if bs is not None else [b]
    bs, b_specs = zip(*[_wspec(w, tk, tn) for w in bs])
    n = n if n is not None else bs[0].shape[-1]
    nk = kdim // tk
    grid = (m // tm, n // tn, nk)
    a_specs = [pl.BlockSpec((tm, tk), lambda i, j, k: (i, k))] * len(bs)
    if epilogue is None:
        def epilogue(p, ex, outs, pids):
            outs[0][...] = p[0].astype(outs[0].dtype)
    if out_shapes is None:
        out_shapes = [jax.ShapeDtypeStruct((m, n), out_dtype)]
        out_specs = [pl.BlockSpec((tm, tn), lambda i, j, k: (i, j))]
    return _fmm(grid, [a] * len(bs), a_specs, bs, b_specs, list(extras), list(extra_specs), out_shapes,
                out_specs, (tm, tn), nk, epilogue, aliases)


def _adaln_kernel(c_ref, w_ref, b_ref, o_ref):
    c = _silu(c_ref[...]).astype(BF16)
    o_ref[...] = jnp.dot(c, w_ref[...].astype(BF16), preferred_element_type=F32) + b_ref[...]


def _adaln(cond8, w_all, layer, bias):
    _, d, n = w_all.shape
    tn = 512
    return pl.pallas_call(
        _adaln_kernel,
        out_shape=jax.ShapeDtypeStruct((8, n), F32),
        grid=(n // tn,),
        in_specs=[pl.BlockSpec((8, d), lambda j: (0, 0)),
                  pl.BlockSpec((None, d, tn), lambda j: (layer, 0, j)),
                  pl.BlockSpec((1, tn), lambda j: (0, j))],
        out_specs=pl.BlockSpec((8, tn), lambda j: (0, j)),
        compiler_params=_cparams(("parallel",)),
    )(cond8, w_all, bias.reshape(1, n))


def _ctx_rows(tile_idx, tm, tiles_per_batch, seq):
    rows = (tile_idx % tiles_per_batch) * tm + lax.broadcasted_iota(jnp.int32, (tm, 1), 0)
    return rows >= seq


def _norm_mod_kernel(x_ref, g_ref, shl_ref, scl_ref, shc_ref, scc_ref, o_ref, *, tm, tpb, seq):
    x = x_ref[...]
    y = x * lax.rsqrt(jnp.mean(x * x, axis=-1, keepdims=True) + NORM_EPS) * g_ref[...]
    ctx = _ctx_rows(pl.program_id(0), tm, tpb, seq)
    sh = jnp.where(ctx, shc_ref[...], shl_ref[...])
    sc = jnp.where(ctx, scc_ref[...], scl_ref[...])
    o_ref[...] = (y * (1.0 + sc) + sh).astype(o_ref.dtype)


def _norm_mod(x, g, sh, sc, nb, tt, seq):
    m, d = x.shape
    tm = _pick(tt, (272, 256, 128))
    tpb = tt // tm
    lat = pl.BlockSpec((None, 1, d), lambda i: (i // tpb, 0, 0))
    ctx = pl.BlockSpec((None, 1, d), lambda i: (nb, 0, 0))
    return pl.pallas_call(
        functools.partial(_norm_mod_kernel, tm=tm, tpb=tpb, seq=seq),
        out_shape=jax.ShapeDtypeStruct((m, d), BF16),
        grid=(m // tm,),
        in_specs=[pl.BlockSpec((tm, d), lambda i: (i, 0)), pl.BlockSpec((1, d), lambda i: (0, 0)),
                  lat, lat, ctx, ctx],
        out_specs=pl.BlockSpec((tm, d), lambda i: (i, 0)),
        compiler_params=_cparams(("parallel",)),
    )(x, g.reshape(1, d), sh, sc, sh, sc)


def _rope128(x, cos, sin):
    lane = lax.broadcasted_iota(jnp.int32, x.shape, 1)
    first = (lane % (2 * ROPE_PAIRS)) < ROPE_PAIRS
    swapped = jnp.where(first, pltpu.roll(x, LANE - ROPE_PAIRS, 1), pltpu.roll(x, ROPE_PAIRS, 1))
    return x * cos + swapped * sin


def _head_norm_rope(x_lo, x_hi, gain_lo, gain_hi, cos, sin, scale):
    ss = jnp.sum(x_lo * x_lo, axis=-1, keepdims=True) + jnp.sum(x_hi * x_hi, axis=-1, keepdims=True)
    inv = lax.rsqrt(ss * (1.0 / QK_HEAD) + NORM_EPS)
    lo = x_lo * inv * gain_lo
    hi = _rope128(x_hi * inv * gain_hi, cos, sin)
    return lo * scale, hi * scale


def _mla_q_kernel(a_ref, g_ref, w_ref, gain_ref, cos_ref, sin_ref, q_ref, *, heads):
    a = a_ref[...]
    an = a * lax.rsqrt(jnp.mean(a * a, axis=-1, keepdims=True) + NORM_EPS) * g_ref[...]
    acc = jnp.dot(an.astype(BF16), w_ref[...], preferred_element_type=F32)
    cos, sin = cos_ref[...], sin_ref[...]
    gain = gain_ref[...]
    for h in range(heads):
        c0 = h * HEAD_PAD
        lo, hi = _head_norm_rope(acc[:, c0:c0 + LANE], acc[:, c0 + LANE:c0 + HEAD_PAD], gain[:, :LANE],
                                 gain[:, LANE:], cos, sin, QK_HEAD ** -0.5)
        q_ref[:, c0:c0 + LANE] = lo.astype(q_ref.dtype)
        q_ref[:, c0 + LANE:c0 + HEAD_PAD] = hi.astype(q_ref.dtype)


def _mla_kv_kernel(a_ref, g_ref, w_ref, kr_ref, gain_ref, cos_ref, sin_ref, k_ref, v_ref, *, heads):
    a = a_ref[...]
    an = a * lax.rsqrt(jnp.mean(a * a, axis=-1, keepdims=True) + NORM_EPS) * g_ref[...]
    acc = jnp.dot(an.astype(BF16), w_ref[...], preferred_element_type=F32)
    cos, sin = cos_ref[...], sin_ref[...]
    gain = gain_ref[...]
    kr = kr_ref[...]
    for h in range(heads):
        c0 = h * HEAD_PAD
        lo, hi = _head_norm_rope(acc[:, c0:c0 + LANE], kr, gain[:, :LANE], gain[:, LANE:], cos, sin, 1.0)
        k_ref[:, c0:c0 + LANE] = lo.astype(k_ref.dtype)
        k_ref[:, c0 + LANE:c0 + HEAD_PAD] = hi.astype(k_ref.dtype)
        v_ref[:, h * V_HEAD:(h + 1) * V_HEAD] = acc[:, c0 + LANE:c0 + HEAD_PAD].astype(v_ref.dtype)


def _mla_up(mla_in, q_norm, w_uq, q_gain, kv_norm, w_ukv, k_gain, cos, sin):
    m = mla_in.shape[0]
    tm = _pick(m, (544, 512, 256))
    hpt = 4
    tn = hpt * HEAD_PAD
    grid = (m // tm, MLA_HEADS // hpt)
    row = lambda w: pl.BlockSpec((tm, w), lambda i, j: (i, 0))
    q = pl.pallas_call(
        functools.partial(_mla_q_kernel, heads=hpt),
        out_shape=jax.ShapeDtypeStruct((m, MLA_HEADS * HEAD_PAD), BF16),
        grid=grid,
        in_specs=[pl.BlockSpec((tm, Q_LORA), lambda i, j: (i, 0)),
                  pl.BlockSpec((1, Q_LORA), lambda i, j: (0, 0)),
                  pl.BlockSpec((Q_LORA, tn), lambda i, j: (0, j)),
                  pl.BlockSpec((1, HEAD_PAD), lambda i, j: (0, 0)),
                  row(LANE), row(LANE)],
        out_specs=pl.BlockSpec((tm, tn), lambda i, j: (i, j)),
        compiler_params=_cparams(("parallel", "parallel")),
    )(mla_in, q_norm.reshape(1, -1), w_uq, q_gain, cos, sin)
    k, v = pl.pallas_call(
        functools.partial(_mla_kv_kernel, heads=hpt),
        out_shape=[jax.ShapeDtypeStruct((m, MLA_HEADS * HEAD_PAD), BF16),
                   jax.ShapeDtypeStruct((m, MLA_HEADS * V_HEAD), BF16)],
        grid=grid,
        in_specs=[pl.BlockSpec((tm, KV_LORA), lambda i, j: (i, Q_LORA // KV_LORA)),
                  pl.BlockSpec((1, KV_LORA), lambda i, j: (0, 0)),
                  pl.BlockSpec((KV_LORA, tn), lambda i, j: (0, j)),
                  pl.BlockSpec((tm, LANE), lambda i, j: (i, (Q_LORA + KV_LORA) // LANE)),
                  pl.BlockSpec((1, HEAD_PAD), lambda i, j: (0, 0)),
                  row(LANE), row(LANE)],
        out_specs=[pl.BlockSpec((tm, tn), lambda i, j: (i, j)),
                   pl.BlockSpec((tm, hpt * V_HEAD), lambda i, j: (i, j))],
        compiler_params=_cparams(("parallel", "parallel")),
    )(mla_in, kv_norm.reshape(1, -1), w_ukv, mla_in, k_gain, cos, sin)
    return q, k, v


def _attn_kernel(q_ref, k_ref, v_ref, o_ref, *, n_lat_blocks, seq):
    s = lax.dot_general(q_ref[...], k_ref[...], (((1,), (1,)), ((), ())), preferred_element_type=F32)
    is_ctx = pl.program_id(2) >= n_lat_blocks
    col = lax.broadcasted_iota(jnp.int32, s.shape, 1)
    s = jnp.where(jnp.logical_and(is_ctx, col < seq), -1e30, s)
    m = jnp.max(s, axis=-1, keepdims=True)
    p = jnp.exp(s - m)
    l = jnp.sum(p, axis=-1, keepdims=True)
    o = jnp.dot(p.astype(BF16), v_ref[...], preferred_element_type=F32)
    o_ref[...] = (o * (1.0 / l)).astype(o_ref.dtype)


def _attention(q, k, v, nb, tt, seq):
    tq = 256
    q3 = q.reshape(nb, tt, MLA_HEADS * HEAD_PAD)
    k3 = k.reshape(nb, tt, MLA_HEADS * HEAD_PAD)
    v3 = v.reshape(nb, tt, MLA_HEADS * V_HEAD)
    out = pl.pallas_call(
        functools.partial(_attn_kernel, n_lat_blocks=seq // tq, seq=seq),
        out_shape=jax.ShapeDtypeStruct((nb, tt, MLA_HEADS * V_HEAD), BF16),
        grid=(nb, MLA_HEADS, tt // tq),
        in_specs=[pl.BlockSpec((None, tq, HEAD_PAD), lambda b, h, i: (b, i, h)),
                  pl.BlockSpec((None, tt, HEAD_PAD), lambda b, h, i: (b, 0, h)),
                  pl.BlockSpec((None, tt, V_HEAD), lambda b, h, i: (b, 0, h))],
        out_specs=pl.BlockSpec((None, tq, V_HEAD), lambda b, h, i: (b, i, h)),
        compiler_params=_cparams(("parallel", "parallel", "arbitrary")),
    )(q3, k3, v3)
    return out.reshape(nb * tt, MLA_HEADS * V_HEAD)


def _dft_angles(n, cols):
    rows = jnp.arange(n, dtype=jnp.int32)
    ang = ((rows[:, None] * cols[None, :]) % n).astype(F32) * (2.0 * math.pi / n)
    return jnp.cos(ang), jnp.sin(ang)


def _dft_mats(n, scale):
    if n % GRID_W or n <= GRID_W:
        c, s = _dft_angles(n, jnp.arange(n, dtype=jnp.int32))
    else:
        ca, sa = _dft_angles(n, jnp.arange(n // GRID_W, dtype=jnp.int32) * GRID_W)
        cb, sb = _dft_angles(n, jnp.arange(GRID_W, dtype=jnp.int32))
        c = (ca[:, :, None] * cb[:, None, :] - sa[:, :, None] * sb[:, None, :]).reshape(n, n)
        s = (sa[:, :, None] * cb[:, None, :] + ca[:, :, None] * sb[:, None, :]).reshape(n, n)
    return (c * scale).astype(BF16), (s * scale).astype(BF16)


def _fourier(f, nb, tt, seq, with_ctx, dft):
    c_t, sneg_t, c_c, s_c, c_x, s_x = dft
    m = f.shape[0]
    fz = f.reshape(m * FNET_GROUPS, FNET_GROUP_DIM)
    rows = fz.shape[0]
    tmz = _pick(rows, (4352, 2048, 1024))

    def ep2(p, ex, outs, pids):
        outs[0][...] = p[0].astype(BF16)
        outs[1][...] = p[1].astype(BF16)

    shp = jax.ShapeDtypeStruct((rows, FNET_GROUP_DIM), BF16)
    ospec = pl.BlockSpec((tmz, FNET_GROUP_DIM), lambda i, j, k: (i, j))
    zc, zs = _flat_mm(fz, None, BF16, tmz, FNET_GROUP_DIM, FNET_GROUP_DIM, epilogue=ep2, bs=[c_c, s_c],
                      out_shapes=[shp, shp], out_specs=[ospec, ospec])
    zc = zc.reshape(nb, tt, FNET_WIDTH)
    zs = zs.reshape(nb, tt, FNET_WIDTH)

    def ep_sum(p, ex, outs, pids):
        outs[0][...] = (p[0] + p[1]).astype(BF16)

    tm = _pick(seq, (1024, 512, 256))
    tk = tm
    tn = FNET_WIDTH
    nk = seq // tk
    aspec = pl.BlockSpec((tm, tk), lambda b, i, j, k: (i, k))
    bspec = pl.BlockSpec((None, tk, tn), lambda b, i, j, k: (b, k, j))
    y_lat = _fmm((nb, seq // tm, FNET_WIDTH // tn, nk), [c_t, sneg_t], [aspec, aspec], [zc, zs], [bspec, bspec],
                 [], [], [jax.ShapeDtypeStruct((nb, seq, FNET_WIDTH), BF16)],
                 [pl.BlockSpec((None, tm, tn), lambda b, i, j, k: (b, i, j))], (tm, tn), nk, ep_sum)[0]
    nctx = tt - seq
    if with_ctx:
        def ep_diff(p, ex, outs, pids):
            outs[0][...] = (p[0] - p[1]).astype(BF16)

        cb = seq // nctx
        aspec = pl.BlockSpec((nctx, nctx), lambda b, k: (0, 0))
        bspec = pl.BlockSpec((None, nctx, FNET_WIDTH), lambda b, k: (b, cb, 0))
        y_ctx = _fmm((nb, 1), [c_x, s_x], [aspec, aspec], [zc, zs], [bspec, bspec], [], [],
                     [jax.ShapeDtypeStruct((nb, nctx, FNET_WIDTH), BF16)],
                     [pl.BlockSpec((None, nctx, FNET_WIDTH), lambda b, k: (b, 0, 0))],
                     (nctx, FNET_WIDTH), 1, ep_diff)[0]
    else:
        y_ctx = jnp.zeros((nb, nctx, FNET_WIDTH), BF16)
    return jnp.concatenate([y_lat, y_ctx], axis=1).reshape(nb * tt, FNET_WIDTH)


def _head_sum(x, ones_bd):
    parts = []
    for s in range(x.shape[1] // LANE):
        hi, lo = _split2(x[:, s * LANE:(s + 1) * LANE])
        parts.append(jnp.dot(hi, ones_bd, preferred_element_type=F32)
                     + jnp.dot(lo, ones_bd, preferred_element_type=F32))
    return jnp.concatenate(parts, axis=-1)


def _feat_kernel(*refs, tf, tpb, lat_tiles, has_vres):
    (z_ref, zp_ref, zn_ref, mu_ref, w0_ref, w2_ref, a0_ref, a2_ref, kk_ref_p, ka_ref, bd_ref) = refs[:11]
    pos = 11
    if has_vres:
        vf_ref, v0_ref, v2_ref = refs[pos:pos + 3]
        pos += 3
    outs = refs[pos:]
    r_o, v_o, kk_o, kf_o, kb_o, lwf_o, lwb_o, af_o, ab_o, sg_o = outs[:10]
    j = pl.program_id(1)
    has_prev = jnp.logical_and(j != 0, j != lat_tiles)
    has_next = jnp.logical_and(j != lat_tiles - 1, j != tpb - 1)
    row = lax.broadcasted_iota(jnp.int32, (tf, 1), 0)

    def zf(lo, hi):
        z = z_ref[:, lo:hi]
        prev_row = jnp.where(has_prev, zp_ref[7:8, lo:hi], 0.0)
        next_row = jnp.where(has_next, zn_ref[0:1, lo:hi], 0.0)
        zp = jnp.where(row == 0, prev_row, pltpu.roll(z, 1, 0))
        zn = jnp.where(row == tf - 1, next_row, pltpu.roll(z, tf - 1, 0))
        return z + mu_ref[:, lo:hi] * (0.5 * (zp + zn) - z)

    w = RWKV_WIDTH
    r = zf(0, w)
    k = zf(w, 2 * w)
    v = zf(2 * w, 3 * w)
    r_o[...] = r
    if has_vres:
        vl = zf(Z_MAIN + GATE_PAD, Z_PAD)
        gate = _sigmoid(v0_ref[...] + _bdot(vl, v2_ref[...]))
        v_o[...] = v + (vf_ref[...] - v) * gate
    else:
        v_o[...] = v
        outs[10][...] = v
    kk = k * kk_ref_p[...]
    ss = _head_sum(kk * kk, bd_ref[...])
    kk_o[...] = kk * lax.rsqrt(ss + 1e-12)
    base = 3 * w
    for d, (lw_o, a_o, k_o) in enumerate(((lwf_o, af_o, kf_o), (lwb_o, ab_o, kb_o))):
        wd = zf(base + d * DECAY_LORA, base + (d + 1) * DECAY_LORA)
        ad = zf(base + 2 * DECAY_LORA + d * ICLR_LORA, base + 2 * DECAY_LORA + (d + 1) * ICLR_LORA)
        xw = w0_ref[d:d + 1, :] + _bdot(jnp.tanh(wd), w2_ref[d])
        lw_o[...] = -_sigmoid(xw) * math.exp(-0.5)
        a = _sigmoid(a0_ref[d:d + 1, :] + _bdot(ad, a2_ref[d]))
        a_o[...] = a
        k_o[...] = k * (1.0 + (a - 1.0) * ka_ref[...])
    sg_o[...] = _sigmoid(zf(Z_MAIN, Z_MAIN + GATE_PAD)).astype(sg_o.dtype)


def _rwkv_features(z, nb, tt, seq, mu, w0, w2, a0, a2, k_k, k_a, ones_bd, vres):
    tf = 128
    tpb = tt // tf
    lat_tiles = seq // tf
    z3 = z.reshape(nb, tt, Z_PAD)
    hb = tf // 8
    last8 = tt // 8 - 1
    w = RWKV_WIDTH
    full = lambda shape: pl.BlockSpec(shape, lambda b, j: (0,) * len(shape))
    tile = lambda width: pl.BlockSpec((None, tf, width), lambda b, j: (b, j, 0))
    in_specs = [tile(Z_PAD),
                pl.BlockSpec((None, 8, Z_PAD), lambda b, j: (b, jnp.maximum(j * hb - 1, 0), 0)),
                pl.BlockSpec((None, 8, Z_PAD), lambda b, j: (b, jnp.minimum((j + 1) * hb, last8), 0)),
                full((1, Z_PAD)), full((2, w)), full((2, DECAY_LORA, w)), full((2, w)), full((2, ICLR_LORA, w)),
                full((1, w)), full((1, w)), full((LANE, LANE))]
    args = [z3, z3, z3, mu.reshape(1, Z_PAD), w0, w2, a0, a2, k_k.reshape(1, w), k_a.reshape(1, w), ones_bd]
    has_vres = vres is not None
    if has_vres:
        v_first, v0, v2 = vres
        in_specs += [tile(w), full((1, w)), full((VRES_PAD, w))]
        args += [v_first.reshape(nb, tt, w), v0.reshape(1, w), v2]
    f3 = jax.ShapeDtypeStruct((nb, tt, w), F32)
    out_shape = [f3] * 9 + [jax.ShapeDtypeStruct((nb, tt, GATE_PAD), BF16)]
    out_specs = [tile(w)] * 9 + [tile(GATE_PAD)]
    if not has_vres:
        out_shape.append(f3)
        out_specs.append(tile(w))
    return pl.pallas_call(
        functools.partial(_feat_kernel, tf=tf, tpb=tpb, lat_tiles=lat_tiles, has_vres=has_vres),
        out_shape=out_shape,
        grid=(nb, tpb),
        in_specs=in_specs,
        out_specs=out_specs,
        compiler_params=_cparams(("parallel", "arbitrary")),
    )(*args)


def _stack2(x, lo_mask):
    return jnp.concatenate([jnp.where(lo_mask, x, 0.0), jnp.where(lo_mask, 0.0, x)], axis=0)


def _dot16(a, b):
    return jnp.dot(a.astype(BF16), b.astype(BF16), preferred_element_type=F32)


def _dg16(a, b, dims):
    return lax.dot_general(a.astype(BF16), b.astype(BF16), dims, preferred_element_type=F32)


def _scan_group(r, k, v, lw, kk, a, h, rev, consts, g):
    tri_cum, strict, incl, lo_mask, eye_big, eye, base_mask, level_masks = consts
    c = CHUNK
    rows = 2 * c
    s2 = lambda x: _stack2(x, lo_mask)
    cols = {name: [] for name in ("kkt", "rt", "bh", "kh", "kb", "bb", "v")}
    tots = []
    for ch in range(g):
        sl = slice(ch * c, (ch + 1) * c)
        lw_c = lw[sl]
        cum = _dot_exact_rhs_left(tri_cum, lw_c)
        tot = cum[0:1, :] if rev else cum[c - 1:c, :]
        g_inv = jnp.exp(-cum)
        g_tail = jnp.exp(tot - cum)
        b_c = kk[sl] * a[sl]
        cols["kkt"].append(s2(kk[sl] * jnp.exp(cum - lw_c)))
        cols["rt"].append(s2(r[sl] * jnp.exp(cum)))
        cols["bh"].append(s2(b_c * g_inv))
        cols["kh"].append(s2(k[sl] * g_inv))
        cols["kb"].append(s2(k[sl] * g_tail))
        cols["bb"].append(s2(b_c * g_tail))
        cols["v"].append(s2(v[sl]))
        tots.append(tot)
    cat = lambda name: jnp.concatenate(cols[name], axis=0)
    rt = cat("rt")
    kkt16, rt16, bh16, kh16, kb16, bb16, v16 = (cat(nm).astype(BF16) for nm in
                                                ("kkt", "rt", "bh", "kh", "kb", "bb", "v"))
    nt = (((1,), (1,)), ((), ()))
    tn = (((0,), (0,)), ((), ()))
    nr = g * rows
    lhs = jnp.concatenate([kkt16, rt16], axis=0)
    big_b = _dg16(lhs, bh16, nt)
    big_k = _dg16(lhs, kh16, nt)
    m_k = jnp.where(strict, big_k[:nr], 0.0)
    a_qb = jnp.where(incl, big_b[nr:], 0.0)
    a_qk = jnp.where(incl, big_k[nr:], 0.0)
    m_b = jnp.where(strict, big_b[:nr], 0.0)
    n = jnp.where(base_mask, -m_b, 0.0)
    tinv = eye_big + n
    for _ in range(int(math.log2(INV_BASE)) - 1):
        n = _dot16(n, n)
        tinv = tinv + _dot16(tinv, n)
    for off_mask in level_masks:
        tinv = tinv - _dot16(_dot16(tinv, jnp.where(off_mask, m_b, 0.0)), tinv)
    mkv = _dot16(m_k, v16)
    wu16 = _dot16(tinv, jnp.concatenate([kkt16, mkv.astype(BF16)], axis=1)).astype(BF16)
    qy = _dot16(a_qb, wu16)
    q16 = (rt - qy[:, :LANE]).astype(BF16)
    y0 = _dot16(a_qk, v16) - qy[:, LANE:]
    ys = [None] * g
    for ch in (range(g - 1, -1, -1) if rev else range(g)):
        rs = slice(ch * rows, (ch + 1) * rows)
        pg = _dg16(bb16[rs], wu16[rs], tn)
        p_mat = eye * jnp.exp(tots[ch]) - pg[:, :LANE]
        g_mat = _dg16(kb16[rs], v16[rs], tn) - pg[:, LANE:]
        h16 = h.astype(BF16)
        y_st = _dot16(q16[rs], h16) + y0[rs]
        h = _dot16(p_mat, h16) + g_mat
        ys[ch] = y_st[:c] + y_st[c:]
    return jnp.concatenate(ys, axis=0), h


def _dot_exact_rhs_left(tri_bf16, x):
    h, m, l = _split3(x)
    d = functools.partial(jnp.dot, preferred_element_type=F32)
    return d(tri_bf16, h) + (d(tri_bf16, m) + d(tri_bf16, l))


def _scan_kernel(r_ref, k_ref, v_ref, lw_ref, kk_ref, a_ref, y_ref, h_ref, *, rev, n_groups, g):
    @pl.when(pl.program_id(2) == 0)
    def _():
        h_ref[...] = jnp.zeros_like(h_ref)

    c = CHUNK
    ri = lax.broadcasted_iota(jnp.int32, (c, c), 0)
    ci = lax.broadcasted_iota(jnp.int32, (c, c), 1)
    tri_cum = (ci >= ri if rev else ci <= ri).astype(BF16)
    nr = g * 2 * c
    r2 = lax.broadcasted_iota(jnp.int32, (nr, nr), 0)
    c2 = lax.broadcasted_iota(jnp.int32, (nr, nr), 1)
    same = (r2 // (2 * c)) == (c2 // (2 * c))
    strict = jnp.logical_and(same, c2 > r2 if rev else c2 < r2)
    incl = jnp.logical_and(same, c2 >= r2 if rev else c2 <= r2)
    eye_big = (r2 == c2).astype(F32)
    eye = eye_big[:LANE, :LANE]
    lo_mask = lax.broadcasted_iota(jnp.int32, (c, LANE), 1) < RWKV_HEAD_DIM
    base_mask = (r2 // INV_BASE) == (c2 // INV_BASE)
    level_masks = []
    s = 2 * INV_BASE
    while s <= c:
        level_masks.append(jnp.logical_and((r2 // s) == (c2 // s), (r2 // (s // 2)) != (c2 // (s // 2))))
        s *= 2
    consts = (tri_cum, strict, incl, lo_mask, eye_big, eye, base_mask, level_masks)
    gl = g * c
    for pi in range(SCAN_PAIRS):
        ls = slice(pi * LANE, (pi + 1) * LANE)
        h = h_ref[pi]
        for gi in (range(n_groups - 1, -1, -1) if rev else range(n_groups)):
            sl = slice(gi * gl, (gi + 1) * gl)
            y, h = _scan_group(r_ref[sl, ls], k_ref[sl, ls], v_ref[sl, ls], lw_ref[sl, ls], kk_ref[sl, ls],
                               a_ref[sl, ls], h, rev, consts, g)
            y_ref[sl, ls] = y
        h_ref[pi] = h


def _rwkv_scan(r, k, v, lw, kk, a, rev, seq):
    nb, tt, w = r.shape
    nblk = tt // SCAN_BLOCK
    lat_blk = seq // SCAN_BLOCK
    if rev:
        def blk(j):
            return nblk - 1 - j
    else:
        def blk(j):
            return jnp.where(j < nblk - lat_blk, lat_blk + j, j - (nblk - lat_blk))
    spec = pl.BlockSpec((None, SCAN_BLOCK, SCAN_PAIRS * LANE), lambda b, p, j: (b, blk(j), p))
    return pl.pallas_call(
        functools.partial(_scan_kernel, rev=rev, n_groups=SCAN_BLOCK // (CHUNK * SCAN_GROUP), g=SCAN_GROUP),
        out_shape=jax.ShapeDtypeStruct((nb, tt, w), F32),
        grid=(nb, w // (SCAN_PAIRS * LANE), nblk),
        in_specs=[spec] * 6,
        out_specs=spec,
        scratch_shapes=[pltpu.VMEM((SCAN_PAIRS, LANE, LANE), F32)],
        compiler_params=_cparams(("parallel", "parallel", "arbitrary")),
    )(r, k, v, lw, kk, a)


def _readout_kernel(yf_ref, yb_ref, r_ref, kf_ref, kb_ref, v_ref, sg_ref, g2_ref, rk_ref, lw_ref, lb_ref, bd_ref,
                    o_ref):
    bd = bd_ref[...]
    inv = 1.0 / RWKV_HEAD_DIM
    y = yf_ref[...] + yb_ref[...]
    mean = _head_sum(y, bd) * inv
    dlt = y - mean
    var = _head_sum(dlt * dlt, bd) * inv
    yn = dlt * lax.rsqrt(var + LNX_EPS) * lw_ref[...] + lb_ref[...]
    bonus = _head_sum(r_ref[...] * (kf_ref[...] + kb_ref[...]) * rk_ref[...], bd) * v_ref[...]
    g = jnp.dot(sg_ref[...], g2_ref[...], preferred_element_type=F32)
    o_ref[...] = ((yn + bonus) * g).astype(o_ref.dtype)


def _rwkv_readout(yf, yb, r, kf, kb, v, sg, g2, r_k, lnx_w, lnx_b, ones_bd):
    m, w = yf.shape
    tm = _pick(m, (272, 256, 128))
    row = lambda width: pl.BlockSpec((tm, width), lambda i: (i, 0))
    full = lambda shape: pl.BlockSpec(shape, lambda i: (0,) * len(shape))
    return pl.pallas_call(
        _readout_kernel,
        out_shape=jax.ShapeDtypeStruct((m, w), BF16),
        grid=(m // tm,),
        in_specs=[row(w)] * 6 + [row(GATE_PAD), full((GATE_PAD, w)), full((1, w)), full((1, w)), full((1, w)),
                                 full((LANE, LANE))],
        out_specs=row(w),
        compiler_params=_cparams(("parallel",)),
    )(yf, yb, r, kf, kb, v, sg, g2, r_k.reshape(1, w), lnx_w.reshape(1, w), lnx_b.reshape(1, w), ones_bd)


def _merge(gates, o_a, o_b, o_c, w_a, w_b, w_c):
    m = o_a.shape[0]
    d = w_a[0].shape[-1]
    tm = _pick(m, (1088, 512, 256))
    tn = 512

    def ep(p, ex, outs, pids):
        y = (_sigmoid(ex[0][...].astype(F32)) * p[0] + _sigmoid(ex[1][...].astype(F32)) * p[1]
             + _sigmoid(ex[2][...].astype(F32)) * p[2])
        outs[0][...] = y.astype(BF16)

    nbn = d // tn
    a_specs = [pl.BlockSpec((tm, o.shape[1]), lambda i, j, k: (i, 0)) for o in (o_a, o_b, o_c)]
    ws, b_specs = zip(*[_wspec(wt, wt[0].shape[-2], tn) for wt in (w_a, w_b, w_c)])
    g_specs = [pl.BlockSpec((tm, tn), functools.partial(lambda i, j, k, off: (i, off + j), off=br * nbn))
               for br in range(N_BRANCHES)]
    return _fmm((m // tm, nbn, 1), [o_a, o_b, o_c], a_specs, ws, b_specs, [gates] * 3, g_specs,
                [jax.ShapeDtypeStruct((m, d), BF16)], [pl.BlockSpec((tm, tn), lambda i, j, k: (i, j))],
                (tm, tn), 1, ep)[0]


def _mm_resid(a, w, x, gate, nb, tt, seq, tk):
    m, kdim = a.shape
    tm = _pick(tt, (1088, 512, 256))
    tpb = tt // tm
    tn = 1024
    nk = kdim // tk
    w, wspec = _wspec(w, tk, tn)
    d = w.shape[-1]

    def ep(p, ex, outs, pids):
        ctx = _ctx_rows(pids[0], tm, tpb, seq)
        g = jnp.where(ctx, ex[2][...], ex[1][...])
        outs[0][...] = ex[0][...] + g * p[0]

    xspec = pl.BlockSpec((tm, tn), lambda i, j, k: (i, j))
    extra_specs = [xspec,
                   pl.BlockSpec((None, 1, tn), lambda i, j, k: (i // tpb, 0, j)),
                   pl.BlockSpec((None, 1, tn), lambda i, j, k: (nb, 0, j))]
    return _fmm((m // tm, d // tn, nk), [a], [pl.BlockSpec((tm, tk), lambda i, j, k: (i, k))],
                [w], [wspec], [x, gate, gate], extra_specs,
                [jax.ShapeDtypeStruct((m, d), F32)], [xspec], (tm, tn), nk, ep, aliases={2: 0})[0]


def _swiglu(u, w1, w3, tn):
    m = u.shape[0]
    tm = _pick(m, (1088, 512, 256))

    def ep(p, ex, outs, pids):
        outs[0][...] = (_silu(p[0]) * p[1]).astype(BF16)

    return _flat_mm(u, None, BF16, tm, tn, 1024, epilogue=ep, bs=[w1, w3])[0]


def _router_kernel(u_ref, w_ref, o_ref):
    logits = _dot_x3(u_ref[...].astype(F32), w_ref[...])
    lane = lax.broadcasted_iota(jnp.int32, logits.shape, 1).astype(F32)
    valid = lane < N_EXPERTS
    neg = -1e30
    lg = jnp.where(valid, logits, neg)
    m1 = jnp.max(lg, axis=-1, keepdims=True)
    i1 = jnp.min(jnp.where(lg == m1, lane, float(LANE)), axis=-1, keepdims=True)
    lg2 = jnp.where(lane == i1, neg, lg)
    m2 = jnp.max(lg2, axis=-1, keepdims=True)
    i2 = jnp.min(jnp.where(lg2 == m2, lane, float(LANE)), axis=-1, keepdims=True)
    e2 = jnp.exp(m2 - m1)
    w1 = 1.0 / (1.0 + e2)
    w2 = e2 / (1.0 + e2)
    o_ref[...] = jnp.where(lane == i1, w1, 0.0) + jnp.where(lane == i2, w2, 0.0)


def _router(u, router_pad):
    m, d = u.shape
    tm = _pick(m, (544, 512, 256))
    return pl.pallas_call(
        _router_kernel,
        out_shape=jax.ShapeDtypeStruct((m, LANE), F32),
        grid=(m // tm,),
        in_specs=[pl.BlockSpec((tm, d), lambda i: (i, 0)), pl.BlockSpec((d, LANE), lambda i: (0, 0))],
        out_specs=pl.BlockSpec((tm, LANE), lambda i: (i, 0)),
        compiler_params=_cparams(("parallel",)),
    )(u, router_pad)


def _moe_down(h, w2, x, gate, comb, e, nb, tt, seq):
    m, kdim = h.shape
    tm = _pick(tt, (1088, 512, 256))
    tpb = tt // tm
    tn = 1024
    tk = 1024
    nk = kdim // tk
    w2, wspec = _wspec(w2, tk, tn)
    d = w2.shape[-1]

    def ep(p, ex, outs, pids):
        ctx = _ctx_rows(pids[0], tm, tpb, seq)
        g = jnp.where(ctx, ex[2][...], ex[1][...])
        cw = ex[3][...]
        lane = lax.broadcasted_iota(jnp.int32, cw.shape, 1)
        ce = jnp.sum(jnp.where(lane == e, cw, 0.0), axis=-1, keepdims=True)
        outs[0][...] = ex[0][...] + (g * ce) * p[0]

    xspec = pl.BlockSpec((tm, tn), lambda i, j, k: (i, j))
    extra_specs = [xspec,
                   pl.BlockSpec((None, 1, tn), lambda i, j, k: (i // tpb, 0, j)),
                   pl.BlockSpec((None, 1, tn), lambda i, j, k: (nb, 0, j)),
                   pl.BlockSpec((tm, LANE), lambda i, j, k: (i, 0))]
    return _fmm((m // tm, d // tn, nk), [h], [pl.BlockSpec((tm, tk), lambda i, j, k: (i, k))],
                [w2], [wspec], [x, gate, gate, comb], extra_specs,
                [jax.ShapeDtypeStruct((m, d), F32)], [xspec], (tm, tn), nk, ep, aliases={2: 0})[0]


def _rope_lane_tables(seq, nctx, nb):
    rows = seq // GRID_W
    row = jnp.repeat(jnp.arange(rows, dtype=F32), GRID_W)
    col = jnp.tile(jnp.arange(GRID_W, dtype=F32), rows)
    inv_freq = ROPE_BASE ** (-2.0 * jnp.arange(ROPE_PAIRS, dtype=F32) / ROPE_AXIS_DIM)
    ar = row[:, None] * inv_freq
    ac = col[:, None] * inv_freq
    ones = jnp.ones((seq, LANE - QK_ROPE), F32)
    zeros = jnp.zeros((seq, LANE - QK_ROPE), F32)
    cos = jnp.concatenate([jnp.cos(ar), jnp.cos(ar), jnp.cos(ac), jnp.cos(ac), ones], axis=1)
    sin = jnp.concatenate([-jnp.sin(ar), jnp.sin(ar), -jnp.sin(ac), jnp.sin(ac), zeros], axis=1)
    cos = jnp.concatenate([cos, jnp.ones((nctx, LANE), F32)], axis=0)
    sin = jnp.concatenate([sin, jnp.zeros((nctx, LANE), F32)], axis=0)
    return jnp.tile(cos, (nb, 1)), jnp.tile(sin, (nb, 1))


def _pad_cols(w, n):
    return jnp.pad(w, ((0, 0), (0, n - w.shape[1])))


def _pad_rows(w, n):
    return jnp.pad(w, ((0, n - w.shape[0]), (0, 0)))


def _z_layout(zsrc, vres_src):
    lead = zsrc.shape[:-1]
    parts = [zsrc[..., :Z_MAIN], zsrc[..., Z_MAIN:], jnp.zeros(lead + (GATE_PAD - GATE_LORA,), zsrc.dtype)]
    if vres_src is None:
        parts.append(jnp.zeros(lead + (VRES_PAD,), zsrc.dtype))
    else:
        parts += [vres_src, jnp.zeros(lead + (VRES_PAD - VRES_LORA,), zsrc.dtype)]
    return jnp.concatenate(parts, axis=-1)


def _pad_head(g):
    return jnp.pad(g, (0, HEAD_PAD - QK_HEAD)).reshape(1, HEAD_PAD)


def kernel(x, c, ctx, c_ctx, ada_w, ada_b, norm1, w_in, w_vres_down, q_norm, w_uq, kv_norm, w_ukv, q_gain, k_gain,
           rwkv_mu, vres_mu, w0, w2, a0, a2, k_k, k_a, v0, v2, r_k, lnx_w, lnx_b, g2, w_br_a, w_br_b, w_br_c,
           w_out, norm2, ffn_w1, ffn_w3, ffn_w2, router, moe_w1, moe_w3, moe_w2):
    nb, seq, d = x.shape
    nctx = ctx.shape[1]
    tt = seq + nctx
    m = nb * tt
    depth = ada_w.shape[0]
    xa = jnp.concatenate([x, ctx], axis=1).reshape(m, d)
    cond8 = jnp.concatenate([c, c_ctx[None], jnp.zeros((8 - nb - 1, d), F32)], axis=0)
    cos_t, sin_t = _rope_lane_tables(seq, nctx, nb)
    c_t, s_t = _dft_mats(seq, 1.0 / math.sqrt(seq))
    c_c, s_c = _dft_mats(FNET_GROUP_DIM, 1.0 / math.sqrt(FNET_GROUP_DIM))
    c_x, s_x = _dft_mats(nctx, 1.0 / math.sqrt(nctx))
    dft = (c_t, -s_t, c_c, s_c, c_x, s_x)
    li = jnp.arange(LANE)
    ones_bd = (li[:, None] // RWKV_HEAD_DIM == li[None, :] // RWKV_HEAD_DIM).astype(BF16)
    v_first = None
    gate_cols = N_BRANCHES * d
    for i in range(depth):
        last = i == depth - 1
        mod = _adaln(cond8, ada_w, i, ada_b[i])[:nb + 1].reshape(nb + 1, 6, 1, d)
        sh1, sc1, gm, sh2, sc2, gf = (mod[:, t] for t in range(6))
        wi = w_in[i]
        w_mla = _pad_cols(wi[:, gate_cols:gate_cols + MLA_IN], MLA_IN_PAD).astype(BF16)
        f0 = gate_cols + MLA_IN
        w_f = wi[:, f0:f0 + FNET_WIDTH].astype(BF16)
        zsrc = wi[:, f0 + FNET_WIDTH:]
        if i == 0:
            w_z = _z_layout(zsrc, None).astype(BF16)
            mu_z = _z_layout(rwkv_mu[i], None)
        else:
            w_z = _z_layout(zsrc, w_vres_down[i - 1]).astype(BF16)
            mu_z = _z_layout(rwkv_mu[i], vres_mu[i - 1])

        u = _norm_mod(xa, norm1[i], sh1, sc1, nb, tt, seq)
        tm = _pick(m, (1088, 512, 256))
        gates = _flat_mm(u, (w_in, (i,)), BF16, tm, 512, d, n=gate_cols)[0]
        mla_in = _flat_mm(u, w_mla, F32, _pick(m, (544, 512, 256)), MLA_IN_PAD, 2048)[0]
        f = _flat_mm(u, w_f, BF16, tm, 512, d)[0]
        z = _flat_mm(u, w_z, F32, _pick(m, (544, 512, 256)), Z_PAD // 3, d)[0]

        w_uq_p = jnp.pad(w_uq[i].reshape(Q_LORA, MLA_HEADS, QK_HEAD),
                         ((0, 0), (0, 0), (0, HEAD_PAD - QK_HEAD))).reshape(Q_LORA, MLA_HEADS * HEAD_PAD)
        q, k, v = _mla_up(mla_in, q_norm[i], w_uq_p.astype(BF16), _pad_head(q_gain[i]), kv_norm[i],
                          w_ukv[i].astype(BF16), _pad_head(k_gain[i]), cos_t, sin_t)
        o_a = _attention(q, k, v, nb, tt, seq)

        o_b = _fourier(f, nb, tt, seq, not last, dft)

        vres = None
        if i > 0:
            vres = (v_first, v0[i - 1], _pad_rows(v2[i - 1], VRES_PAD).astype(BF16))
        feat = _rwkv_features(z, nb, tt, seq, mu_z, w0[i], w2[i].astype(BF16), a0[i], a2[i].astype(BF16),
                              k_k[i], k_a[i], ones_bd, vres)
        r_, v_, kk_, kf_, kb_, lwf_, lwb_, af_, ab_, sg_ = feat[:10]
        if i == 0:
            v_first = feat[10]
        y_f = _rwkv_scan(r_, kf_, v_, lwf_, kk_, af_, False, seq)
        y_b = _rwkv_scan(r_, kb_, v_, lwb_, kk_, ab_, True, seq)
        flat = lambda t: t.reshape(m, t.shape[-1])
        o_c = _rwkv_readout(flat(y_f), flat(y_b), flat(r_), flat(kf_), flat(kb_), flat(v_), flat(sg_),
                            _pad_rows(g2[i], GATE_PAD).astype(BF16), r_k[i], lnx_w[i], lnx_b[i], ones_bd)

        mixed = _merge(gates, o_a, o_b, o_c, (w_br_a, (i,)), (w_br_b, (i,)), (w_br_c, (i,)))
        xa = _mm_resid(mixed, (w_out, (i,)), xa, gm, nb, tt, seq, 1024)

        u2 = _norm_mod(xa, norm2[i], sh2, sc2, nb, tt, seq)
        jj = i // 2
        if i % 2 == 0:
            h = _swiglu(u2, _pad_cols(ffn_w1[jj], D_FF_PAD).astype(BF16),
                        _pad_cols(ffn_w3[jj], D_FF_PAD).astype(BF16), 1024)
            xa = _mm_resid(h, _pad_rows(ffn_w2[jj], D_FF_PAD).astype(BF16), xa, gf, nb, tt, seq, 1024)
        else:
            comb = _router(u2, _pad_cols(router[jj], LANE))
            for e in range(N_EXPERTS):
                h = _swiglu(u2, (moe_w1, (jj, e)), (moe_w3, (jj, e)), 1024)
                xa = _moe_down(h, (moe_w2, (jj, e)), xa, gf, comb, e, nb, tt, seq)
    return xa.reshape(nb, tt, d)[:, :seq]
```

```python
import functools
import math

import jax
import jax.numpy as jnp
import numpy as np
from jax import lax
from jax.experimental import pallas as pl
from jax.experimental.pallas import tpu as pltpu

F32 = jnp.float32
BF16 = jnp.bfloat16

D_MODEL = 4096
GRID_W = 64
NORM_EPS = 1e-6
MLA_HEADS = 16
Q_LORA = 1024
KV_LORA = 512
QK_NOPE = 128
QK_ROPE = 64
QK_HEAD = QK_NOPE + QK_ROPE
V_HEAD = 128
HEAD_PAD = 256
Q_SCALE = QK_HEAD ** -0.5 * math.log2(math.e)
ROPE_AXIS_DIM = QK_ROPE // 2
ROPE_PAIRS = ROPE_AXIS_DIM // 2
ROPE_BASE = 10000.0
FNET_GROUPS = 4
FNET_GROUP_DIM = 256
FNET_WIDTH = FNET_GROUPS * FNET_GROUP_DIM
RWKV_HEADS = 16
RWKV_HEAD_DIM = 64
RWKV_WIDTH = RWKV_HEADS * RWKV_HEAD_DIM
DECAY_LORA = 128
ICLR_LORA = 128
VRES_LORA = 96
GATE_LORA = 480
GATE_PAD = 512
VRES_PAD = 128
LNX_EPS = 64e-5
N_BRANCHES = 3
MLA_IN = Q_LORA + KV_LORA + QK_ROPE
MLA_IN_PAD = Q_LORA + KV_LORA + 128
Z_MAIN = 3 * RWKV_WIDTH + 2 * DECAY_LORA + 2 * ICLR_LORA
Z_PAD = Z_MAIN + GATE_PAD + VRES_PAD
D_FF = 11008
D_FF_PAD = 11264
N_EXPERTS = 8
TOP_K = 2
D_FF_EXPERT = 3072

LANE = 128
CHUNK = 64
SCAN_BLOCK = 256
SCAN_GROUP = 2
SCAN_PAIRS = 2
INV_BASE = 16
VMEM_LIMIT = 52 * 1024 * 1024


def _cparams(sem):
    return pltpu.CompilerParams(dimension_semantics=sem, vmem_limit_bytes=VMEM_LIMIT)


def _pick(n, cands):
    for c in cands:
        if n % c == 0:
            return c
    raise ValueError(f"no tile for {n} in {cands}")


def _silu(x):
    return x * (1.0 / (1.0 + jnp.exp(-x)))


def _sigmoid(x):
    return 1.0 / (1.0 + jnp.exp(-x))


def _bdot(a, b):
    return jnp.dot(a.astype(BF16), b.astype(BF16), preferred_element_type=F32)


def _split2(x):
    hi = x.astype(BF16)
    lo = (x - hi.astype(F32)).astype(BF16)
    return hi, lo


def _split3(x):
    hi = x.astype(BF16)
    r1 = x - hi.astype(F32)
    mid = r1.astype(BF16)
    lo = (r1 - mid.astype(F32)).astype(BF16)
    return hi, mid, lo


def _dot_x3(a, b):
    ah, al = _split2(a)
    bh, bl = _split2(b)
    d = functools.partial(jnp.dot, preferred_element_type=F32)
    return d(ah, bh) + (d(ah, bl) + d(al, bh))


def _dot_exact_rhs(a, b_bf16):
    h, m, l = _split3(a)
    d = functools.partial(jnp.dot, preferred_element_type=F32)
    return d(h, b_bf16) + (d(m, b_bf16) + d(l, b_bf16))


def _fmm_kernel(*refs, n_a, n_prod, n_extra, n_out, nk, kaxis, epilogue):
    a = refs[:n_a]
    b = refs[n_a:n_a + n_prod]
    ex = refs[n_a + n_prod:n_a + n_prod + n_extra]
    outs = refs[n_a + n_prod + n_extra:n_a + n_prod + n_extra + n_out]
    accs = refs[n_a + n_prod + n_extra + n_out:]
    pids = [pl.program_id(ax) for ax in range(kaxis + 1)]
    prods = [jnp.dot(a[i % n_a][...], b[i][...].astype(BF16), preferred_element_type=F32) for i in range(n_prod)]
    if nk == 1:
        epilogue(prods, ex, outs, pids)
        return
    k = pids[kaxis]

    @pl.when(k == 0)
    def _():
        for i in range(n_prod):
            accs[i][...] = prods[i]

    @pl.when(k > 0)
    def _():
        for i in range(n_prod):
            accs[i][...] += prods[i]

    @pl.when(k == nk - 1)
    def _():
        epilogue([acc[...] for acc in accs], ex, outs, pids)


def _fmm(grid, a_list, a_specs, b_list, b_specs, extras, extra_specs, out_shapes, out_specs,
         acc_shape, nk, epilogue, aliases=None):
    n_prod = len(b_list)
    kern = functools.partial(_fmm_kernel, n_a=len(a_list), n_prod=n_prod, n_extra=len(extras), n_out=len(out_shapes),
                             nk=nk, kaxis=len(grid) - 1, epilogue=epilogue)
    scratch = [pltpu.VMEM(acc_shape, F32) for _ in range(n_prod)] if nk > 1 else []
    sem = ("parallel",) * (len(grid) - 1) + ("arbitrary",)
    return pl.pallas_call(
        kern,
        out_shape=out_shapes,
        grid=grid,
        in_specs=list(a_specs) + list(b_specs) + list(extra_specs),
        out_specs=out_specs,
        scratch_shapes=scratch,
        input_output_aliases=aliases or {},
        compiler_params=_cparams(sem),
    )(*a_list, *b_list, *extras)


def _wspec(w, tk, tn):
    if isinstance(w, tuple):
        arr, lead = w
        return arr, pl.BlockSpec((None,) * len(lead) + (tk, tn), lambda i, j, k: tuple(lead) + (k, j))
    return w, pl.BlockSpec((tk, tn), lambda i, j, k: (k, j))


def _flat_mm(a, b, out_dtype, tm, tn, tk, epilogue=None, extras=(), extra_specs=(), bs=None,
             out_shapes=None, out_specs=None, aliases=None, n=None):
    m, kdim = a.shape
    bs = bs if bs is not None else [b]
    bs, b_specs = zip(*[_wspec(w, tk, tn) for w in bs])
    n = n if n is not None else bs[0].shape[-1]
    nk = kdim // tk
    grid = (m // tm, n // tn, nk)
    a_specs = [pl.BlockSpec((tm, tk), lambda i, j, k: (i, k))]
    if epilogue is None:
        def epilogue(p, ex, outs, pids):
            outs[0][...] = p[0].astype(outs[0].dtype)
    if out_shapes is None:
        out_shapes = [jax.ShapeDtypeStruct((m, n), out_dtype)]
        out_specs = [pl.BlockSpec((tm, tn), lambda i, j, k: (i, j))]
    return _fmm(grid, [a], a_specs, bs, b_specs, list(extras), list(extra_specs), out_shapes,
                out_specs, (tm, tn), nk, epilogue, aliases)


def _adaln_kernel(c_ref, w_ref, b_ref, o_ref):
    c = _silu(c_ref[...]).astype(BF16)
    o_ref[...] = jnp.dot(c, w_ref[...].astype(BF16), preferred_element_type=F32) + b_ref[...]


def _adaln(cond8, w_all, layer, bias):
    _, d, n = w_all.shape
    tn = 512
    return pl.pallas_call(
        _adaln_kernel,
        out_shape=jax.ShapeDtypeStruct((8, n), F32),
        grid=(n // tn,),
        in_specs=[pl.BlockSpec((8, d), lambda j: (0, 0)),
                  pl.BlockSpec((None, d, tn), lambda j: (layer, 0, j)),
                  pl.BlockSpec((1, tn), lambda j: (0, j))],
        out_specs=pl.BlockSpec((8, tn), lambda j: (0, j)),
        compiler_params=_cparams(("parallel",)),
    )(cond8, w_all, bias.reshape(1, n))


def _ctx_rows(tile_idx, tm, tiles_per_batch, seq):
    rows = (tile_idx % tiles_per_batch) * tm + lax.broadcasted_iota(jnp.int32, (tm, 1), 0)
    return rows >= seq


def _norm_mod_kernel(x_ref, g_ref, shl_ref, scl_ref, shc_ref, scc_ref, o_ref, *, tm, tpb, seq):
    x = x_ref[...]
    y = x * lax.rsqrt(jnp.mean(x * x, axis=-1, keepdims=True) + NORM_EPS) * g_ref[...]
    ctx = _ctx_rows(pl.program_id(0), tm, tpb, seq)
    sh = jnp.where(ctx, shc_ref[...], shl_ref[...])
    sc = jnp.where(ctx, scc_ref[...], scl_ref[...])
    o_ref[...] = (y * (1.0 + sc) + sh).astype(o_ref.dtype)


def _norm_mod(x, g, sh, sc, nb, tt, seq, out_dtype=BF16):
    m, d = x.shape
    tm = _pick(tt, (272, 256, 128))
    tpb = tt // tm
    lat = pl.BlockSpec((None, 1, d), lambda i: (i // tpb, 0, 0))
    ctx = pl.BlockSpec((None, 1, d), lambda i: (nb, 0, 0))
    return pl.pallas_call(
        functools.partial(_norm_mod_kernel, tm=tm, tpb=tpb, seq=seq),
        out_shape=jax.ShapeDtypeStruct((m, d), out_dtype),
        grid=(m // tm,),
        in_specs=[pl.BlockSpec((tm, d), lambda i: (i, 0)), pl.BlockSpec((1, d), lambda i: (0, 0)),
                  lat, lat, ctx, ctx],
        out_specs=pl.BlockSpec((tm, d), lambda i: (i, 0)),
        compiler_params=_cparams(("parallel",)),
    )(x, g.reshape(1, d), sh, sc, sh, sc)


def _rope128(x, cos, sin):
    lane = lax.broadcasted_iota(jnp.int32, x.shape, 1)
    first = (lane % (2 * ROPE_PAIRS)) < ROPE_PAIRS
    swapped = jnp.where(first, pltpu.roll(x, LANE - ROPE_PAIRS, 1), pltpu.roll(x, ROPE_PAIRS, 1))
    return x * cos + swapped * sin


def _head_norm_rope(x_lo, x_hi, gain_lo, gain_hi, cos, sin, scale):
    ss = jnp.sum(x_lo * x_lo, axis=-1, keepdims=True) + jnp.sum(x_hi * x_hi, axis=-1, keepdims=True)
    inv = lax.rsqrt(ss * (1.0 / QK_HEAD) + NORM_EPS)
    lo = x_lo * inv * gain_lo
    hi = _rope128(x_hi * inv * gain_hi, cos, sin)
    return lo * scale, hi * scale


def _mla_q_kernel(a_ref, g_ref, w_ref, gain_ref, cos_ref, sin_ref, q_ref, *, heads):
    a = a_ref[...]
    an = a * lax.rsqrt(jnp.mean(a * a, axis=-1, keepdims=True) + NORM_EPS) * g_ref[...]
    acc = jnp.dot(an.astype(BF16), w_ref[...], preferred_element_type=F32)
    cos, sin = cos_ref[...], sin_ref[...]
    gain = gain_ref[...]
    for h in range(heads):
        c0 = h * HEAD_PAD
        lo, hi = _head_norm_rope(acc[:, c0:c0 + LANE], acc[:, c0 + LANE:c0 + HEAD_PAD], gain[:, :LANE],
                                 gain[:, LANE:], cos, sin, Q_SCALE)
        q_ref[:, c0:c0 + LANE] = lo.astype(q_ref.dtype)
        q_ref[:, c0 + LANE:c0 + HEAD_PAD] = hi.astype(q_ref.dtype)


def _mla_kv_kernel(a_ref, g_ref, w_ref, kr_ref, gain_ref, cos_ref, sin_ref, k_ref, v_ref, *, heads):
    a = a_ref[...]
    an = a * lax.rsqrt(jnp.mean(a * a, axis=-1, keepdims=True) + NORM_EPS) * g_ref[...]
    acc = jnp.dot(an.astype(BF16), w_ref[...], preferred_element_type=F32)
    cos, sin = cos_ref[...], sin_ref[...]
    gain = gain_ref[...]
    kr = kr_ref[...]
    for h in range(heads):
        c0 = h * HEAD_PAD
        lo, hi = _head_norm_rope(acc[:, c0:c0 + LANE], kr, gain[:, :LANE], gain[:, LANE:], cos, sin, 1.0)
        k_ref[:, c0:c0 + LANE] = lo.astype(k_ref.dtype)
        k_ref[:, c0 + LANE:c0 + HEAD_PAD] = hi.astype(k_ref.dtype)
        v_ref[:, h * V_HEAD:(h + 1) * V_HEAD] = acc[:, c0 + LANE:c0 + HEAD_PAD].astype(v_ref.dtype)


def _mla_up(mla_in, q_norm, w_uq, q_gain, kv_norm, w_ukv, k_gain, cos, sin):
    m = mla_in.shape[0]
    tm = _pick(m, (544, 512, 256))
    hpt = 4
    tn = hpt * HEAD_PAD
    grid = (m // tm, MLA_HEADS // hpt)
    row = lambda w: pl.BlockSpec((tm, w), lambda i, j: (i, 0))
    q = pl.pallas_call(
        functools.partial(_mla_q_kernel, heads=hpt),
        out_shape=jax.ShapeDtypeStruct((m, MLA_HEADS * HEAD_PAD), BF16),
        grid=grid,
        in_specs=[pl.BlockSpec((tm, Q_LORA), lambda i, j: (i, 0)),
                  pl.BlockSpec((1, Q_LORA), lambda i, j: (0, 0)),
                  pl.BlockSpec((Q_LORA, tn), lambda i, j: (0, j)),
                  pl.BlockSpec((1, HEAD_PAD), lambda i, j: (0, 0)),
                  row(LANE), row(LANE)],
        out_specs=pl.BlockSpec((tm, tn), lambda i, j: (i, j)),
        compiler_params=_cparams(("parallel", "parallel")),
    )(mla_in, q_norm.reshape(1, -1), w_uq, q_gain, cos, sin)
    k, v = pl.pallas_call(
        functools.partial(_mla_kv_kernel, heads=hpt),
        out_shape=[jax.ShapeDtypeStruct((m, MLA_HEADS * HEAD_PAD), BF16),
                   jax.ShapeDtypeStruct((m, MLA_HEADS * V_HEAD), BF16)],
        grid=grid,
        in_specs=[pl.BlockSpec((tm, KV_LORA), lambda i, j: (i, Q_LORA // KV_LORA)),
                  pl.BlockSpec((1, KV_LORA), lambda i, j: (0, 0)),
                  pl.BlockSpec((KV_LORA, tn), lambda i, j: (0, j)),
                  pl.BlockSpec((tm, LANE), lambda i, j: (i, (Q_LORA + KV_LORA) // LANE)),
                  pl.BlockSpec((1, HEAD_PAD), lambda i, j: (0, 0)),
                  row(LANE), row(LANE)],
        out_specs=[pl.BlockSpec((tm, tn), lambda i, j: (i, j)),
                   pl.BlockSpec((tm, hpt * V_HEAD), lambda i, j: (i, j))],
        compiler_params=_cparams(("parallel", "parallel")),
    )(mla_in, kv_norm.reshape(1, -1), w_ukv, mla_in, k_gain, cos, sin)
    return q, k, v


def _softmax_pv(q, k, v, o_ref):
    s = lax.dot_general(q, k, (((1,), (1,)), ((), ())), preferred_element_type=F32)
    p = jnp.exp2(s - jnp.max(s, axis=-1, keepdims=True))
    l = jnp.sum(p, axis=-1, keepdims=True)
    o = jnp.dot(p.astype(BF16), v, preferred_element_type=F32)
    o_ref[...] = (o * (1.0 / l)).astype(o_ref.dtype)


def _attn_kernel(q_ref, k_ref, v_ref, o_ref, *, n_lat_blocks, seq):
    is_ctx = pl.program_id(2) >= n_lat_blocks

    @pl.when(jnp.logical_not(is_ctx))
    def _():
        _softmax_pv(q_ref[...], k_ref[...], v_ref[...], o_ref)

    @pl.when(is_ctx)
    def _():
        _softmax_pv(q_ref[...], k_ref[seq:, :], v_ref[seq:, :], o_ref)


def _attention(q, k, v, nb, tt, seq):
    tq = 256
    q3 = q.reshape(nb, tt, MLA_HEADS * HEAD_PAD)
    k3 = k.reshape(nb, tt, MLA_HEADS * HEAD_PAD)
    v3 = v.reshape(nb, tt, MLA_HEADS * V_HEAD)
    out = pl.pallas_call(
        functools.partial(_attn_kernel, n_lat_blocks=seq // tq, seq=seq),
        out_shape=jax.ShapeDtypeStruct((nb, tt, MLA_HEADS * V_HEAD), BF16),
        grid=(nb, MLA_HEADS, tt // tq),
        in_specs=[pl.BlockSpec((None, tq, HEAD_PAD), lambda b, h, i: (b, i, h)),
                  pl.BlockSpec((None, tt, HEAD_PAD), lambda b, h, i: (b, 0, h)),
                  pl.BlockSpec((None, tt, V_HEAD), lambda b, h, i: (b, 0, h))],
        out_specs=pl.BlockSpec((None, tq, V_HEAD), lambda b, h, i: (b, i, h)),
        compiler_params=_cparams(("parallel", "parallel", "arbitrary")),
    )(q3, k3, v3)
    return out.reshape(nb * tt, MLA_HEADS * V_HEAD)


def _dft_angles(n, cols):
    rows = jnp.arange(n, dtype=jnp.int32)
    ang = ((rows[:, None] * cols[None, :]) % n).astype(F32) * (2.0 * math.pi / n)
    return jnp.cos(ang), jnp.sin(ang)


def _dft_mats(n, scale):
    if n % GRID_W or n <= GRID_W:
        c, s = _dft_angles(n, jnp.arange(n, dtype=jnp.int32))
    else:
        ca, sa = _dft_angles(n, jnp.arange(n // GRID_W, dtype=jnp.int32) * GRID_W)
        cb, sb = _dft_angles(n, jnp.arange(GRID_W, dtype=jnp.int32))
        c = (ca[:, :, None] * cb[:, None, :] - sa[:, :, None] * sb[:, None, :]).reshape(n, n)
        s = (sa[:, :, None] * cb[:, None, :] + ca[:, :, None] * sb[:, None, :]).reshape(n, n)
    return (c * scale).astype(BF16), (s * scale).astype(BF16)


def _fourier(f, nb, tt, seq, with_ctx, dft):
    c_t, sneg_t, c_c, s_c, c_x, s_x = dft
    m = f.shape[0]
    fz = f.reshape(m * FNET_GROUPS, FNET_GROUP_DIM)
    rows = fz.shape[0]
    tmz = _pick(rows, (4352, 2048, 1024))

    def ep2(p, ex, outs, pids):
        outs[0][...] = p[0].astype(BF16)
        outs[1][...] = p[1].astype(BF16)

    shp = jax.ShapeDtypeStruct((rows, FNET_GROUP_DIM), BF16)
    ospec = pl.BlockSpec((tmz, FNET_GROUP_DIM), lambda i, j, k: (i, j))
    zc, zs = _flat_mm(fz, None, BF16, tmz, FNET_GROUP_DIM, FNET_GROUP_DIM, epilogue=ep2, bs=[c_c, s_c],
                      out_shapes=[shp, shp], out_specs=[ospec, ospec])
    zc = zc.reshape(nb, tt, FNET_WIDTH)
    zs = zs.reshape(nb, tt, FNET_WIDTH)

    def ep_sum(p, ex, outs, pids):
        outs[0][...] = (p[0] + p[1]).astype(BF16)

    tm = _pick(seq, (1024, 512, 256))
    tk = tm
    tn = FNET_WIDTH
    nk = seq // tk
    aspec = pl.BlockSpec((tm, tk), lambda b, i, j, k: (i, k))
    bspec = pl.BlockSpec((None, tk, tn), lambda b, i, j, k: (b, k, j))
    y_lat = _fmm((nb, seq // tm, FNET_WIDTH // tn, nk), [c_t, sneg_t], [aspec, aspec], [zc, zs], [bspec, bspec],
                 [], [], [jax.ShapeDtypeStruct((nb, seq, FNET_WIDTH), BF16)],
                 [pl.BlockSpec((None, tm, tn), lambda b, i, j, k: (b, i, j))], (tm, tn), nk, ep_sum)[0]
    nctx = tt - seq
    if with_ctx:
        def ep_diff(p, ex, outs, pids):
            outs[0][...] = (p[0] - p[1]).astype(BF16)

        cb = seq // nctx
        aspec = pl.BlockSpec((nctx, nctx), lambda b, k: (0, 0))
        bspec = pl.BlockSpec((None, nctx, FNET_WIDTH), lambda b, k: (b, cb, 0))
        y_ctx = _fmm((nb, 1), [c_x, s_x], [aspec, aspec], [zc, zs], [bspec, bspec], [], [],
                     [jax.ShapeDtypeStruct((nb, nctx, FNET_WIDTH), BF16)],
                     [pl.BlockSpec((None, nctx, FNET_WIDTH), lambda b, k: (b, 0, 0))],
                     (nctx, FNET_WIDTH), 1, ep_diff)[0]
    else:
        y_ctx = jnp.zeros((nb, nctx, FNET_WIDTH), BF16)
    return jnp.concatenate([y_lat, y_ctx], axis=1).reshape(nb * tt, FNET_WIDTH)


def _head_sum(x, ones_bd):
    parts = []
    for s in range(x.shape[1] // LANE):
        hi, lo = _split2(x[:, s * LANE:(s + 1) * LANE])
        parts.append(jnp.dot(hi, ones_bd, preferred_element_type=F32)
                     + jnp.dot(lo, ones_bd, preferred_element_type=F32))
    return jnp.concatenate(parts, axis=-1)


def _feat_kernel(*refs, tf, tpb, lat_tiles, has_vres):
    (z_ref, zp_ref, zn_ref, mu_ref, w0_ref, w2_ref, a0_ref, a2_ref, kk_ref_p, ka_ref, bd_ref) = refs[:11]
    pos = 11
    if has_vres:
        vf_ref, v0_ref, v2_ref = refs[pos:pos + 3]
        pos += 3
    outs = refs[pos:]
    r_o, v_o, kk_o, kf_o, kb_o, lwf_o, lwb_o, af_o, ab_o, sg_o = outs[:10]
    j = pl.program_id(1)
    has_prev = jnp.logical_and(j != 0, j != lat_tiles)
    has_next = jnp.logical_and(j != lat_tiles - 1, j != tpb - 1)
    row = lax.broadcasted_iota(jnp.int32, (tf, 1), 0)

    def zf(lo, hi):
        z = z_ref[:, lo:hi]
        prev_row = jnp.where(has_prev, zp_ref[7:8, lo:hi], 0.0)
        next_row = jnp.where(has_next, zn_ref[0:1, lo:hi], 0.0)
        zp = jnp.where(row == 0, prev_row, pltpu.roll(z, 1, 0))
        zn = jnp.where(row == tf - 1, next_row, pltpu.roll(z, tf - 1, 0))
        return z + mu_ref[:, lo:hi] * (0.5 * (zp + zn) - z)

    w = RWKV_WIDTH
    r = zf(0, w)
    k = zf(w, 2 * w)
    v = zf(2 * w, 3 * w)
    r_o[...] = r
    if has_vres:
        vl = zf(Z_MAIN + GATE_PAD, Z_PAD)
        gate = _sigmoid(v0_ref[...] + _bdot(vl, v2_ref[...]))
        v_o[...] = v + (vf_ref[...] - v) * gate
    else:
        v_o[...] = v
        outs[10][...] = v
    kk = k * kk_ref_p[...]
    ss = _head_sum(kk * kk, bd_ref[...])
    kk_o[...] = kk * lax.rsqrt(ss + 1e-12)
    base = 3 * w
    for d, (lw_o, a_o, k_o) in enumerate(((lwf_o, af_o, kf_o), (lwb_o, ab_o, kb_o))):
        wd = zf(base + d * DECAY_LORA, base + (d + 1) * DECAY_LORA)
        ad = zf(base + 2 * DECAY_LORA + d * ICLR_LORA, base + 2 * DECAY_LORA + (d + 1) * ICLR_LORA)
        xw = w0_ref[d:d + 1, :] + _bdot(jnp.tanh(wd), w2_ref[d])
        lw_o[...] = -_sigmoid(xw) * math.exp(-0.5)
        a = _sigmoid(a0_ref[d:d + 1, :] + _bdot(ad, a2_ref[d]))
        a_o[...] = a
        k_o[...] = k * (1.0 + (a - 1.0) * ka_ref[...])
    sg_o[...] = _sigmoid(zf(Z_MAIN, Z_MAIN + GATE_PAD)).astype(sg_o.dtype)


def _rwkv_features(z, nb, tt, seq, mu, w0, w2, a0, a2, k_k, k_a, ones_bd, vres):
    tf = 128
    tpb = tt // tf
    lat_tiles = seq // tf
    z3 = z.reshape(nb, tt, Z_PAD)
    hb = tf // 8
    last8 = tt // 8 - 1
    w = RWKV_WIDTH
    full = lambda shape: pl.BlockSpec(shape, lambda b, j: (0,) * len(shape))
    tile = lambda width: pl.BlockSpec((None, tf, width), lambda b, j: (b, j, 0))
    in_specs = [tile(Z_PAD),
                pl.BlockSpec((None, 8, Z_PAD), lambda b, j: (b, jnp.maximum(j * hb - 1, 0), 0)),
                pl.BlockSpec((None, 8, Z_PAD), lambda b, j: (b, jnp.minimum((j + 1) * hb, last8), 0)),
                full((1, Z_PAD)), full((2, w)), full((2, DECAY_LORA, w)), full((2, w)), full((2, ICLR_LORA, w)),
                full((1, w)), full((1, w)), full((LANE, LANE))]
    args = [z3, z3, z3, mu.reshape(1, Z_PAD), w0, w2, a0, a2, k_k.reshape(1, w), k_a.reshape(1, w), ones_bd]
    has_vres = vres is not None
    if has_vres:
        v_first, v0, v2 = vres
        in_specs += [tile(w), full((1, w)), full((VRES_PAD, w))]
        args += [v_first.reshape(nb, tt, w), v0.reshape(1, w), v2]
    f3 = jax.ShapeDtypeStruct((nb, tt, w), F32)
    out_shape = [f3] * 9 + [jax.ShapeDtypeStruct((nb, tt, GATE_PAD), BF16)]
    out_specs = [tile(w)] * 9 + [tile(GATE_PAD)]
    if not has_vres:
        out_shape.append(f3)
        out_specs.append(tile(w))
    return pl.pallas_call(
        functools.partial(_feat_kernel, tf=tf, tpb=tpb, lat_tiles=lat_tiles, has_vres=has_vres),
        out_shape=out_shape,
        grid=(nb, tpb),
        in_specs=in_specs,
        out_specs=out_specs,
        compiler_params=_cparams(("parallel", "arbitrary")),
    )(*args)


def _stack2(x, lo_mask):
    return jnp.concatenate([jnp.where(lo_mask, x, 0.0), jnp.where(lo_mask, 0.0, x)], axis=0)


def _dot16(a, b):
    return jnp.dot(a.astype(BF16), b.astype(BF16), preferred_element_type=F32)


def _dg16(a, b, dims):
    return lax.dot_general(a.astype(BF16), b.astype(BF16), dims, preferred_element_type=F32)


def _scan_group(r, k, v, lw, kk, a, h, rev, consts, g):
    tri_cum, strict, incl, lo_mask, eye_big, eye, base_mask, level_masks = consts
    c = CHUNK
    rows = 2 * c
    s2 = lambda x: _stack2(x, lo_mask)
    cols = {name: [] for name in ("kkt", "rt", "bh", "kh", "kb", "bb", "v")}
    tots = []
    for ch in range(g):
        sl = slice(ch * c, (ch + 1) * c)
        lw_c = lw[sl]
        cum = _dot_exact_rhs_left(tri_cum, lw_c)
        tot = cum[0:1, :] if rev else cum[c - 1:c, :]
        g_inv = jnp.exp(-cum)
        g_tail = jnp.exp(tot - cum)
        b_c = kk[sl] * a[sl]
        cols["kkt"].append(s2(kk[sl] * jnp.exp(cum - lw_c)))
        cols["rt"].append(s2(r[sl] * jnp.exp(cum)))
        cols["bh"].append(s2(b_c * g_inv))
        cols["kh"].append(s2(k[sl] * g_inv))
        cols["kb"].append(s2(k[sl] * g_tail))
        cols["bb"].append(s2(b_c * g_tail))
        cols["v"].append(s2(v[sl]))
        tots.append(tot)
    cat = lambda name: jnp.concatenate(cols[name], axis=0)
    rt = cat("rt")
    kkt16, rt16, bh16, kh16, kb16, bb16, v16 = (cat(nm).astype(BF16) for nm in
                                                ("kkt", "rt", "bh", "kh", "kb", "bb", "v"))
    nt = (((1,), (1,)), ((), ()))
    tn = (((0,), (0,)), ((), ()))
    nr = g * rows
    lhs = jnp.concatenate([kkt16, rt16], axis=0)
    big_b = _dg16(lhs, bh16, nt)
    big_k = _dg16(lhs, kh16, nt)
    m_k = jnp.where(strict, big_k[:nr], 0.0)
    a_qb = jnp.where(incl, big_b[nr:], 0.0)
    a_qk = jnp.where(incl, big_k[nr:], 0.0)
    m_b = jnp.where(strict, big_b[:nr], 0.0)
    n = jnp.where(base_mask, -m_b, 0.0)
    tinv = eye_big + n
    for _ in range(int(math.log2(INV_BASE)) - 1):
        n = _dot16(n, n)
        tinv = tinv + _dot16(tinv, n)
    for off_mask in level_masks:
        tinv = tinv - _dot16(_dot16(tinv, jnp.where(off_mask, m_b, 0.0)), tinv)
    mkv = _dot16(m_k, v16)
    wu16 = _dot16(tinv, jnp.concatenate([kkt16, mkv.astype(BF16)], axis=1)).astype(BF16)
    qy = _dot16(a_qb, wu16)
    q16 = (rt - qy[:, :LANE]).astype(BF16)
    y0 = _dot16(a_qk, v16) - qy[:, LANE:]
    ys = [None] * g
    for ch in (range(g - 1, -1, -1) if rev else range(g)):
        rs = slice(ch * rows, (ch + 1) * rows)
        pg = _dg16(bb16[rs], wu16[rs], tn)
        p_mat = eye * jnp.exp(tots[ch]) - pg[:, :LANE]
        g_mat = _dg16(kb16[rs], v16[rs], tn) - pg[:, LANE:]
        h16 = h.astype(BF16)
        y_st = _dot16(q16[rs], h16) + y0[rs]
        h = _dot16(p_mat, h16) + g_mat
        ys[ch] = y_st[:c] + y_st[c:]
    return jnp.concatenate(ys, axis=0), h


def _dot_exact_rhs_left(tri_bf16, x):
    h, m, l = _split3(x)
    d = functools.partial(jnp.dot, preferred_element_type=F32)
    return d(tri_bf16, h) + (d(tri_bf16, m) + d(tri_bf16, l))


def _scan_kernel(r_ref, k_ref, v_ref, lw_ref, kk_ref, a_ref, y_ref, h_ref, *, rev, n_groups, g):
    @pl.when(pl.program_id(2) == 0)
    def _():
        h_ref[...] = jnp.zeros_like(h_ref)

    c = CHUNK
    ri = lax.broadcasted_iota(jnp.int32, (c, c), 0)
    ci = lax.broadcasted_iota(jnp.int32, (c, c), 1)
    tri_cum = (ci >= ri if rev else ci <= ri).astype(BF16)
    nr = g * 2 * c
    r2 = lax.broadcasted_iota(jnp.int32, (nr, nr), 0)
    c2 = lax.broadcasted_iota(jnp.int32, (nr, nr), 1)
    same = (r2 // (2 * c)) == (c2 // (2 * c))
    strict = jnp.logical_and(same, c2 > r2 if rev else c2 < r2)
    incl = jnp.logical_and(same, c2 >= r2 if rev else c2 <= r2)
    eye_big = (r2 == c2).astype(F32)
    eye = eye_big[:LANE, :LANE]
    lo_mask = lax.broadcasted_iota(jnp.int32, (c, LANE), 1) < RWKV_HEAD_DIM
    base_mask = (r2 // INV_BASE) == (c2 // INV_BASE)
    level_masks = []
    s = 2 * INV_BASE
    while s <= c:
        level_masks.append(jnp.logical_and((r2 // s) == (c2 // s), (r2 // (s // 2)) != (c2 // (s // 2))))
        s *= 2
    consts = (tri_cum, strict, incl, lo_mask, eye_big, eye, base_mask, level_masks)
    gl = g * c
    for pi in range(SCAN_PAIRS):
        ls = slice(pi * LANE, (pi + 1) * LANE)
        h = h_ref[pi]
        for gi in (range(n_groups - 1, -1, -1) if rev else range(n_groups)):
            sl = slice(gi * gl, (gi + 1) * gl)
            y, h = _scan_group(r_ref[sl, ls], k_ref[sl, ls], v_ref[sl, ls], lw_ref[sl, ls], kk_ref[sl, ls],
                               a_ref[sl, ls], h, rev, consts, g)
            y_ref[sl, ls] = y
        h_ref[pi] = h


def _rwkv_scan(r, k, v, lw, kk, a, rev, seq):
    nb, tt, w = r.shape
    nblk = tt // SCAN_BLOCK
    lat_blk = seq // SCAN_BLOCK
    if rev:
        def blk(j):
            return nblk - 1 - j
    else:
        def blk(j):
            return jnp.where(j < nblk - lat_blk, lat_blk + j, j - (nblk - lat_blk))
    spec = pl.BlockSpec((None, SCAN_BLOCK, SCAN_PAIRS * LANE), lambda b, p, j: (b, blk(j), p))
    return pl.pallas_call(
        functools.partial(_scan_kernel, rev=rev, n_groups=SCAN_BLOCK // (CHUNK * SCAN_GROUP), g=SCAN_GROUP),
        out_shape=jax.ShapeDtypeStruct((nb, tt, w), F32),
        grid=(nb, w // (SCAN_PAIRS * LANE), nblk),
        in_specs=[spec] * 6,
        out_specs=spec,
        scratch_shapes=[pltpu.VMEM((SCAN_PAIRS, LANE, LANE), F32)],
        compiler_params=_cparams(("parallel", "parallel", "arbitrary")),
    )(r, k, v, lw, kk, a)


def _readout_kernel(yf_ref, yb_ref, r_ref, kf_ref, kb_ref, v_ref, sg_ref, g2_ref, rk_ref, lw_ref, lb_ref, bd_ref,
                    o_ref):
    bd = bd_ref[...]
    inv = 1.0 / RWKV_HEAD_DIM
    y = yf_ref[...] + yb_ref[...]
    mean = _head_sum(y, bd) * inv
    dlt = y - mean
    var = _head_sum(dlt * dlt, bd) * inv
    yn = dlt * lax.rsqrt(var + LNX_EPS) * lw_ref[...] + lb_ref[...]
    bonus = _head_sum(r_ref[...] * (kf_ref[...] + kb_ref[...]) * rk_ref[...], bd) * v_ref[...]
    g = jnp.dot(sg_ref[...], g2_ref[...], preferred_element_type=F32)
    o_ref[...] = ((yn + bonus) * g).astype(o_ref.dtype)


def _rwkv_readout(yf, yb, r, kf, kb, v, sg, g2, r_k, lnx_w, lnx_b, ones_bd):
    m, w = yf.shape
    tm = _pick(m, (272, 256, 128))
    row = lambda width: pl.BlockSpec((tm, width), lambda i: (i, 0))
    full = lambda shape: pl.BlockSpec(shape, lambda i: (0,) * len(shape))
    return pl.pallas_call(
        _readout_kernel,
        out_shape=jax.ShapeDtypeStruct((m, w), BF16),
        grid=(m // tm,),
        in_specs=[row(w)] * 6 + [row(GATE_PAD), full((GATE_PAD, w)), full((1, w)), full((1, w)), full((1, w)),
                                 full((LANE, LANE))],
        out_specs=row(w),
        compiler_params=_cparams(("parallel",)),
    )(yf, yb, r, kf, kb, v, sg, g2, r_k.reshape(1, w), lnx_w.reshape(1, w), lnx_b.reshape(1, w), ones_bd)


def _merge(gates, o_a, o_b, o_c, w_a, w_b, w_c):
    m = o_a.shape[0]
    d = w_a[0].shape[-1]
    tm = _pick(m, (1088, 512, 256))
    tn = 512

    def ep(p, ex, outs, pids):
        y = (_sigmoid(ex[0][...].astype(F32)) * p[0] + _sigmoid(ex[1][...].astype(F32)) * p[1]
             + _sigmoid(ex[2][...].astype(F32)) * p[2])
        outs[0][...] = y.astype(BF16)

    nbn = d // tn
    a_specs = [pl.BlockSpec((tm, o.shape[1]), lambda i, j, k: (i, 0)) for o in (o_a, o_b, o_c)]
    ws, b_specs = zip(*[_wspec(wt, wt[0].shape[-2], tn) for wt in (w_a, w_b, w_c)])
    g_specs = [pl.BlockSpec((tm, tn), functools.partial(lambda i, j, k, off: (i, off + j), off=br * nbn))
               for br in range(N_BRANCHES)]
    return _fmm((m // tm, nbn, 1), [o_a, o_b, o_c], a_specs, ws, b_specs, [gates] * 3, g_specs,
                [jax.ShapeDtypeStruct((m, d), BF16)], [pl.BlockSpec((tm, tn), lambda i, j, k: (i, j))],
                (tm, tn), 1, ep)[0]


def _mm_resid(a, w, x, gate, nb, tt, seq, tk, tn):
    m, kdim = a.shape
    tm = _pick(tt, (1088, 512, 256))
    tpb = tt // tm
    nk = kdim // tk
    w, wspec = _wspec(w, tk, tn)
    d = w.shape[-1]

    def ep(p, ex, outs, pids):
        ctx = _ctx_rows(pids[0], tm, tpb, seq)
        g = jnp.where(ctx, ex[2][...], ex[1][...])
        outs[0][...] = ex[0][...] + g * p[0]

    xspec = pl.BlockSpec((tm, tn), lambda i, j, k: (i, j))
    extra_specs = [xspec,
                   pl.BlockSpec((None, 1, tn), lambda i, j, k: (i // tpb, 0, j)),
                   pl.BlockSpec((None, 1, tn), lambda i, j, k: (nb, 0, j))]
    return _fmm((m // tm, d // tn, nk), [a], [pl.BlockSpec((tm, tk), lambda i, j, k: (i, k))],
                [w], [wspec], [x, gate, gate], extra_specs,
                [jax.ShapeDtypeStruct((m, d), F32)], [xspec], (tm, tn), nk, ep, aliases={2: 0})[0]


def _swiglu(u, w1, w3, tn, tk):
    m = u.shape[0]
    tm = _pick(m, (1088, 512, 256))

    def ep(p, ex, outs, pids):
        outs[0][...] = (_silu(p[0]) * p[1]).astype(BF16)

    return _flat_mm(u, None, BF16, tm, tn, tk, epilogue=ep, bs=[w1, w3])[0]


ROUTE_LANE = 8


def _router_kernel(u_ref, w_ref, o_ref):
    logits = _dot_x3(u_ref[...].astype(F32), w_ref[...])
    lane = lax.broadcasted_iota(jnp.int32, logits.shape, 1).astype(F32)
    valid = lane < N_EXPERTS
    neg = -1e30
    lg = jnp.where(valid, logits, neg)
    m1 = jnp.max(lg, axis=-1, keepdims=True)
    i1 = jnp.min(jnp.where(lg == m1, lane, float(LANE)), axis=-1, keepdims=True)
    lg2 = jnp.where(lane == i1, neg, lg)
    m2 = jnp.max(lg2, axis=-1, keepdims=True)
    i2 = jnp.min(jnp.where(lg2 == m2, lane, float(LANE)), axis=-1, keepdims=True)
    e2 = jnp.exp(m2 - m1)
    w1 = 1.0 / (1.0 + e2)
    w2 = e2 / (1.0 + e2)
    comb = jnp.where(lane == i1, w1, 0.0) + jnp.where(lane == i2, w2, 0.0)
    meta = (jnp.where(lane == ROUTE_LANE, i1, 0.0) + jnp.where(lane == ROUTE_LANE + 1, i2, 0.0)
            + jnp.where(lane == ROUTE_LANE + 2, w1, 0.0) + jnp.where(lane == ROUTE_LANE + 3, w2, 0.0))
    o_ref[...] = comb + meta


def _router(u, router_pad):
    m, d = u.shape
    tm = _pick(m, (544, 512, 256))
    return pl.pallas_call(
        _router_kernel,
        out_shape=jax.ShapeDtypeStruct((m, LANE), F32),
        grid=(m // tm,),
        in_specs=[pl.BlockSpec((tm, d), lambda i: (i, 0)), pl.BlockSpec((d, LANE), lambda i: (0, 0))],
        out_specs=pl.BlockSpec((tm, LANE), lambda i: (i, 0)),
        compiler_params=_cparams(("parallel",)),
    )(u, router_pad)


MOE_TILE = 512
GATHER_TILE = 256


def _route_plan(e1, e2, tm):
    n_tok = e1.shape[0]
    e_flat = jnp.stack([e1, e2], axis=1).reshape(-1)
    n_asg = e_flat.shape[0]
    onehot = (e_flat[:, None] == jnp.arange(N_EXPERTS, dtype=jnp.int32)[None, :]).astype(jnp.int32)
    rank = jnp.sum((jnp.cumsum(onehot, axis=0) - onehot) * onehot, axis=1)
    counts = jnp.sum(onehot, axis=0)
    padded = ((counts + tm - 1) // tm) * tm
    ends = jnp.cumsum(padded)
    slot = (ends - padded)[e_flat] + rank
    n_slots = n_asg + N_EXPERTS * tm
    src_tok = jnp.zeros((n_slots,), jnp.int32).at[slot].set(jnp.arange(n_asg, dtype=jnp.int32) // TOP_K)
    tile_start = jnp.arange(n_slots // tm, dtype=jnp.int32) * tm
    tile_e = jnp.sum((tile_start[:, None] >= ends[None, :]).astype(jnp.int32), axis=1)
    valid = (tile_start < ends[-1]).astype(jnp.int32)
    tile_e = jnp.where(valid == 1, tile_e, tile_e[jnp.maximum(ends[-1] // tm - 1, 0)])
    return src_tok, slot.reshape(n_tok, TOP_K), tile_e, valid


def _row_copy(src_hbm, row, buf, r, sem):
    return pltpu.make_async_copy(src_hbm.at[pl.ds(row, 1)], buf.at[pl.ds(r, 1)], sem)


def _gather_kernel(idx_ref, src_hbm, o_ref, buf, sem, *, tg):
    base = pl.program_id(0) * tg

    def start(r, carry):
        _row_copy(src_hbm, idx_ref[base + r], buf, r, sem).start()
        return carry

    def wait(r, carry):
        _row_copy(src_hbm, 0, buf, r, sem).wait()
        return carry

    lax.fori_loop(0, tg, start, 0, unroll=8)
    lax.fori_loop(0, tg, wait, 0, unroll=8)
    o_ref[...] = buf[...].astype(o_ref.dtype)


def _gather_rows(src, idx, out_dtype):
    n_slots = idx.shape[0]
    d = src.shape[1]
    tg = GATHER_TILE
    return pl.pallas_call(
        functools.partial(_gather_kernel, tg=tg),
        out_shape=jax.ShapeDtypeStruct((n_slots, d), out_dtype),
        grid_spec=pltpu.PrefetchScalarGridSpec(
            num_scalar_prefetch=1,
            grid=(n_slots // tg,),
            in_specs=[pl.BlockSpec(memory_space=pl.ANY)],
            out_specs=pl.BlockSpec((tg, d), lambda i, idx_ref: (i, 0)),
            scratch_shapes=[pltpu.VMEM((tg, d), src.dtype), pltpu.SemaphoreType.DMA(())]),
        compiler_params=_cparams(("arbitrary",)),
    )(idx, src)


def _grouped_up_kernel(te_ref, tv_ref, a_ref, w1_ref, w3_ref, o_ref):
    valid = tv_ref[pl.program_id(1)] == 1

    @pl.when(valid)
    def _():
        a = a_ref[...]
        p1 = jnp.dot(a, w1_ref[...].astype(BF16), preferred_element_type=F32)
        p3 = jnp.dot(a, w3_ref[...].astype(BF16), preferred_element_type=F32)
        o_ref[...] = (_silu(p1) * p3).astype(o_ref.dtype)

    @pl.when(jnp.logical_not(valid))
    def _():
        o_ref[...] = jnp.zeros_like(o_ref)


def _grouped_down_kernel(te_ref, tv_ref, a_ref, w_ref, o_ref):
    valid = tv_ref[pl.program_id(1)] == 1

    @pl.when(valid)
    def _():
        o_ref[...] = jnp.dot(a_ref[...], w_ref[...].astype(BF16), preferred_element_type=F32)

    @pl.when(jnp.logical_not(valid))
    def _():
        o_ref[...] = jnp.zeros_like(o_ref)


def _grouped_mm(kern, a, ws, layer, tile_e, tile_valid, tn, out_dtype):
    n_slots, kdim = a.shape
    n = ws[0].shape[-1]
    tm = MOE_TILE
    wspec = pl.BlockSpec((None, None, kdim, tn), lambda j, i, te, tv: (layer, te[i], 0, j))
    return pl.pallas_call(
        kern,
        out_shape=jax.ShapeDtypeStruct((n_slots, n), out_dtype),
        grid_spec=pltpu.PrefetchScalarGridSpec(
            num_scalar_prefetch=2,
            grid=(n // tn, n_slots // tm),
            in_specs=[pl.BlockSpec((tm, kdim), lambda j, i, te, tv: (i, 0))] + [wspec] * len(ws),
            out_specs=pl.BlockSpec((tm, tn), lambda j, i, te, tv: (i, j))),
        compiler_params=_cparams(("parallel", "arbitrary")),
    )(tile_e, tile_valid, a, *ws)


def _combine_kernel(s1_ref, s2_ref, ys_hbm, x_ref, rw_ref, gate_ref, o_ref, buf1, buf2, sem, *, tc, spb):
    base = (pl.program_id(0) * spb + pl.program_id(1)) * tc

    def start(r, carry):
        _row_copy(ys_hbm, s1_ref[base + r], buf1, r, sem).start()
        _row_copy(ys_hbm, s2_ref[base + r], buf2, r, sem).start()
        return carry

    def wait(r, carry):
        _row_copy(ys_hbm, 0, buf1, r, sem).wait()
        _row_copy(ys_hbm, 0, buf2, r, sem).wait()
        return carry

    lax.fori_loop(0, tc, start, 0, unroll=8)
    lax.fori_loop(0, tc, wait, 0, unroll=8)
    rw = rw_ref[...]
    lane = lax.broadcasted_iota(jnp.int32, rw.shape, 1)
    w1 = jnp.sum(jnp.where(lane == ROUTE_LANE + 2, rw, 0.0), axis=-1, keepdims=True)
    w2 = jnp.sum(jnp.where(lane == ROUTE_LANE + 3, rw, 0.0), axis=-1, keepdims=True)
    o_ref[...] = x_ref[...] + gate_ref[...] * (w1 * buf1[...] + w2 * buf2[...])


def _moe_combine(x, ys, slots, route, gate, nb, tt, seq):
    m, d = x.shape
    tc = GATHER_TILE
    spb = seq // tc
    bpb = tt // tc
    row = lambda b, i, s1, s2: (b * bpb + i, 0)
    return pl.pallas_call(
        functools.partial(_combine_kernel, tc=tc, spb=spb),
        out_shape=jax.ShapeDtypeStruct((m, d), F32),
        grid_spec=pltpu.PrefetchScalarGridSpec(
            num_scalar_prefetch=2,
            grid=(nb, spb),
            in_specs=[pl.BlockSpec(memory_space=pl.ANY),
                      pl.BlockSpec((tc, d), row),
                      pl.BlockSpec((tc, LANE), row),
                      pl.BlockSpec((None, 1, d), lambda b, i, s1, s2: (b, 0, 0))],
            out_specs=pl.BlockSpec((tc, d), row),
            scratch_shapes=[pltpu.VMEM((tc, d), F32), pltpu.VMEM((tc, d), F32), pltpu.SemaphoreType.DMA(())]),
        input_output_aliases={3: 0},
        compiler_params=_cparams(("arbitrary", "arbitrary")),
    )(slots[:, 0], slots[:, 1], ys, x, route, gate)


def _moe_routed(x, u, route, moe_w1, moe_w3, moe_w2, layer, gate, nb, tt, seq):
    m, d = x.shape
    meta = route.reshape(nb, tt, LANE)[:, :seq].reshape(nb * seq, LANE)
    e1 = meta[:, ROUTE_LANE].astype(jnp.int32)
    e2 = meta[:, ROUTE_LANE + 1].astype(jnp.int32)
    src_tok, slots, tile_e, tile_valid = _route_plan(e1, e2, MOE_TILE)
    src_row = (src_tok // seq) * tt + src_tok % seq
    xs = _gather_rows(u, src_row, BF16)
    h = _grouped_mm(_grouped_up_kernel, xs, [moe_w1, moe_w3], layer, tile_e, tile_valid, 512, BF16)
    ys = _grouped_mm(_grouped_down_kernel, h, [moe_w2], layer, tile_e, tile_valid, 1024, F32)
    return _moe_combine(x, ys, slots, route, gate, nb, tt, seq)


def _rope_lane_tables(seq, nctx, nb):
    rows = seq // GRID_W
    row = jnp.repeat(jnp.arange(rows, dtype=F32), GRID_W)
    col = jnp.tile(jnp.arange(GRID_W, dtype=F32), rows)
    inv_freq = ROPE_BASE ** (-2.0 * jnp.arange(ROPE_PAIRS, dtype=F32) / ROPE_AXIS_DIM)
    ar = row[:, None] * inv_freq
    ac = col[:, None] * inv_freq
    ones = jnp.ones((seq, LANE - QK_ROPE), F32)
    zeros = jnp.zeros((seq, LANE - QK_ROPE), F32)
    cos = jnp.concatenate([jnp.cos(ar), jnp.cos(ar), jnp.cos(ac), jnp.cos(ac), ones], axis=1)
    sin = jnp.concatenate([-jnp.sin(ar), jnp.sin(ar), -jnp.sin(ac), jnp.sin(ac), zeros], axis=1)
    cos = jnp.concatenate([cos, jnp.ones((nctx, LANE), F32)], axis=0)
    sin = jnp.concatenate([sin, jnp.zeros((nctx, LANE), F32)], axis=0)
    return jnp.tile(cos, (nb, 1)), jnp.tile(sin, (nb, 1))


def _pad_cols(w, n):
    return jnp.pad(w, ((0, 0), (0, n - w.shape[1])))


def _pad_rows(w, n):
    return jnp.pad(w, ((0, n - w.shape[0]), (0, 0)))


def _z_layout(zsrc, vres_src):
    lead = zsrc.shape[:-1]
    parts = [zsrc[..., :Z_MAIN], zsrc[..., Z_MAIN:], jnp.zeros(lead + (GATE_PAD - GATE_LORA,), zsrc.dtype)]
    if vres_src is None:
        parts.append(jnp.zeros(lead + (VRES_PAD,), zsrc.dtype))
    else:
        parts += [vres_src, jnp.zeros(lead + (VRES_PAD - VRES_LORA,), zsrc.dtype)]
    return jnp.concatenate(parts, axis=-1)


def _pad_head(g):
    return jnp.pad(g, (0, HEAD_PAD - QK_HEAD)).reshape(1, HEAD_PAD)


def kernel(x, c, ctx, c_ctx, ada_w, ada_b, norm1, w_in, w_vres_down, q_norm, w_uq, kv_norm, w_ukv, q_gain, k_gain,
           rwkv_mu, vres_mu, w0, w2, a0, a2, k_k, k_a, v0, v2, r_k, lnx_w, lnx_b, g2, w_br_a, w_br_b, w_br_c,
           w_out, norm2, ffn_w1, ffn_w3, ffn_w2, router, moe_w1, moe_w3, moe_w2):
    nb, seq, d = x.shape
    nctx = ctx.shape[1]
    tt = seq + nctx
    m = nb * tt
    depth = ada_w.shape[0]
    xa = jnp.concatenate([x, ctx], axis=1).reshape(m, d)
    cond8 = jnp.concatenate([c, c_ctx[None], jnp.zeros((8 - nb - 1, d), F32)], axis=0)
    cos_t, sin_t = _rope_lane_tables(seq, nctx, nb)
    c_t, s_t = _dft_mats(seq, 1.0 / math.sqrt(seq))
    c_c, s_c = _dft_mats(FNET_GROUP_DIM, 1.0 / math.sqrt(FNET_GROUP_DIM))
    c_x, s_x = _dft_mats(nctx, 1.0 / math.sqrt(nctx))
    dft = (c_t, -s_t, c_c, s_c, c_x, s_x)
    li = jnp.arange(LANE)
    ones_bd = (li[:, None] // RWKV_HEAD_DIM == li[None, :] // RWKV_HEAD_DIM).astype(BF16)
    v_first = None
    gate_cols = N_BRANCHES * d
    w_in_bf = w_in.astype(BF16)
    for i in range(depth):
        last = i == depth - 1
        mod = _adaln(cond8, ada_w, i, ada_b[i])[:nb + 1].reshape(nb + 1, 6, 1, d)
        sh1, sc1, gm, sh2, sc2, gf = (mod[:, t] for t in range(6))
        wi = w_in_bf[i]
        w_mla = _pad_cols(wi[:, gate_cols:gate_cols + MLA_IN], MLA_IN_PAD).astype(BF16)
        f0 = gate_cols + MLA_IN
        w_f = wi[:, f0:f0 + FNET_WIDTH].astype(BF16)
        zsrc = wi[:, f0 + FNET_WIDTH:]
        if i == 0:
            w_z = _z_layout(zsrc, None).astype(BF16)
            mu_z = _z_layout(rwkv_mu[i], None)
        else:
            w_z = _z_layout(zsrc, w_vres_down[i - 1]).astype(BF16)
            mu_z = _z_layout(rwkv_mu[i], vres_mu[i - 1])

        u = _norm_mod(xa, norm1[i], sh1, sc1, nb, tt, seq)
        tm = _pick(m, (1088, 512, 256))
        gates = _flat_mm(u, (w_in_bf, (i,)), BF16, tm, 512, d, n=gate_cols)[0]
        mla_in = _flat_mm(u, w_mla, F32, _pick(m, (544, 512, 256)), MLA_IN_PAD, 2048)[0]
        f = _flat_mm(u, w_f, BF16, tm, 512, d)[0]
        z = _flat_mm(u, w_z, F32, _pick(m, (544, 512, 256)), Z_PAD // 3, d)[0]

        w_uq_p = jnp.pad(w_uq[i].reshape(Q_LORA, MLA_HEADS, QK_HEAD),
                         ((0, 0), (0, 0), (0, HEAD_PAD - QK_HEAD))).reshape(Q_LORA, MLA_HEADS * HEAD_PAD)
        q, k, v = _mla_up(mla_in, q_norm[i], w_uq_p.astype(BF16), _pad_head(q_gain[i]), kv_norm[i],
                          w_ukv[i].astype(BF16), _pad_head(k_gain[i]), cos_t, sin_t)
        o_a = _attention(q, k, v, nb, tt, seq)

        o_b = _fourier(f, nb, tt, seq, not last, dft)

        vres = None
        if i > 0:
            vres = (v_first, v0[i - 1], _pad_rows(v2[i - 1], VRES_PAD).astype(BF16))
        feat = _rwkv_features(z, nb, tt, seq, mu_z, w0[i], w2[i].astype(BF16), a0[i], a2[i].astype(BF16),
                              k_k[i], k_a[i], ones_bd, vres)
        r_, v_, kk_, kf_, kb_, lwf_, lwb_, af_, ab_, sg_ = feat[:10]
        if i == 0:
            v_first = feat[10]
        y_f = _rwkv_scan(r_, kf_, v_, lwf_, kk_, af_, False, seq)
        y_b = _rwkv_scan(r_, kb_, v_, lwb_, kk_, ab_, True, seq)
        flat = lambda t: t.reshape(m, t.shape[-1])
        o_c = _rwkv_readout(flat(y_f), flat(y_b), flat(r_), flat(kf_), flat(kb_), flat(v_), flat(sg_),
                            _pad_rows(g2[i], GATE_PAD).astype(BF16), r_k[i], lnx_w[i], lnx_b[i], ones_bd)

        mixed = _merge(gates, o_a, o_b, o_c, (w_br_a, (i,)), (w_br_b, (i,)), (w_br_c, (i,)))
        xa = _mm_resid(mixed, (w_out, (i,)), xa, gm, nb, tt, seq, d, 512)

        jj = i // 2
        u2 = _norm_mod(xa, norm2[i], sh2, sc2, nb, tt, seq, BF16 if i % 2 == 0 else F32)
        if i % 2 == 0:
            h = _swiglu(u2, _pad_cols(ffn_w1[jj], D_FF_PAD).astype(BF16),
                        _pad_cols(ffn_w3[jj], D_FF_PAD).astype(BF16), 512, d)
            xa = _mm_resid(h, _pad_rows(ffn_w2[jj], D_FF_PAD).astype(BF16), xa, gf, nb, tt, seq, D_FF_PAD // 4, 1024)
        else:
            route = _router(u2, _pad_cols(router[jj], LANE))
            xa = _moe_routed(xa, u2, route, moe_w1, moe_w3, moe_w2, jj, gf, nb, tt, seq)
    return xa.reshape(nb, tt, d)[:, :seq]
```

```python
import functools
import math

import jax
import jax.numpy as jnp
import numpy as np
from jax import lax
from jax.experimental import pallas as pl
from jax.experimental.pallas import tpu as pltpu

F32 = jnp.float32
BF16 = jnp.bfloat16

D_MODEL = 4096
GRID_W = 64
NORM_EPS = 1e-6
MLA_HEADS = 16
Q_LORA = 1024
KV_LORA = 512
QK_NOPE = 128
QK_ROPE = 64
QK_HEAD = QK_NOPE + QK_ROPE
V_HEAD = 128
HEAD_PAD = 256
Q_SCALE = QK_HEAD ** -0.5 * math.log2(math.e)
ROPE_AXIS_DIM = QK_ROPE // 2
ROPE_PAIRS = ROPE_AXIS_DIM // 2
ROPE_BASE = 10000.0
FNET_GROUPS = 4
FNET_GROUP_DIM = 256
FNET_WIDTH = FNET_GROUPS * FNET_GROUP_DIM
RWKV_HEADS = 16
RWKV_HEAD_DIM = 64
RWKV_WIDTH = RWKV_HEADS * RWKV_HEAD_DIM
DECAY_LORA = 128
ICLR_LORA = 128
VRES_LORA = 96
GATE_LORA = 480
GATE_PAD = 512
VRES_PAD = 128
LNX_EPS = 64e-5
N_BRANCHES = 3
MLA_IN = Q_LORA + KV_LORA + QK_ROPE
MLA_IN_PAD = Q_LORA + KV_LORA + 128
Z_MAIN = 3 * RWKV_WIDTH + 2 * DECAY_LORA + 2 * ICLR_LORA
Z_PAD = Z_MAIN + GATE_PAD + VRES_PAD
D_FF = 11008
D_FF_PAD = 11264
N_EXPERTS = 8
TOP_K = 2
D_FF_EXPERT = 3072

LANE = 128
CHUNK = 64
SCAN_BLOCK = 256
SCAN_GROUP = 1
SCAN_PAIRS = 4
INV_BASE = 16
VMEM_LIMIT = 52 * 1024 * 1024


def _cparams(sem):
    return pltpu.CompilerParams(dimension_semantics=sem, vmem_limit_bytes=VMEM_LIMIT)


def _pick(n, cands):
    for c in cands:
        if n % c == 0:
            return c
    raise ValueError(f"no tile for {n} in {cands}")


def _silu(x):
    return x * (1.0 / (1.0 + jnp.exp(-x)))


def _sigmoid(x):
    return 1.0 / (1.0 + jnp.exp(-x))


def _bdot(a, b):
    return jnp.dot(a.astype(BF16), b.astype(BF16), preferred_element_type=F32)


def _split2(x):
    hi = x.astype(BF16)
    lo = (x - hi.astype(F32)).astype(BF16)
    return hi, lo


def _split3(x):
    hi = x.astype(BF16)
    r1 = x - hi.astype(F32)
    mid = r1.astype(BF16)
    lo = (r1 - mid.astype(F32)).astype(BF16)
    return hi, mid, lo


def _dot_x3(a, b):
    ah, al = _split2(a)
    bh, bl = _split2(b)
    d = functools.partial(jnp.dot, preferred_element_type=F32)
    return d(ah, bh) + (d(ah, bl) + d(al, bh))


def _dot_exact_rhs(a, b_bf16):
    h, m, l = _split3(a)
    d = functools.partial(jnp.dot, preferred_element_type=F32)
    return d(h, b_bf16) + (d(m, b_bf16) + d(l, b_bf16))


def _fmm_kernel(*refs, n_a, n_prod, n_extra, n_out, nk, kaxis, epilogue):
    a = refs[:n_a]
    b = refs[n_a:n_a + n_prod]
    ex = refs[n_a + n_prod:n_a + n_prod + n_extra]
    outs = refs[n_a + n_prod + n_extra:n_a + n_prod + n_extra + n_out]
    accs = refs[n_a + n_prod + n_extra + n_out:]
    pids = [pl.program_id(ax) for ax in range(kaxis + 1)]
    prods = [jnp.dot(a[i % n_a][...], b[i][...].astype(BF16), preferred_element_type=F32) for i in range(n_prod)]
    if nk == 1:
        epilogue(prods, ex, outs, pids)
        return
    k = pids[kaxis]

    @pl.when(k == 0)
    def _():
        for i in range(n_prod):
            accs[i][...] = prods[i]

    @pl.when(k > 0)
    def _():
        for i in range(n_prod):
            accs[i][...] += prods[i]

    @pl.when(k == nk - 1)
    def _():
        epilogue([acc[...] for acc in accs], ex, outs, pids)


def _fmm(grid, a_list, a_specs, b_list, b_specs, extras, extra_specs, out_shapes, out_specs,
         acc_shape, nk, epilogue, aliases=None):
    n_prod = len(b_list)
    kern = functools.partial(_fmm_kernel, n_a=len(a_list), n_prod=n_prod, n_extra=len(extras), n_out=len(out_shapes),
                             nk=nk, kaxis=len(grid) - 1, epilogue=epilogue)
    scratch = [pltpu.VMEM(acc_shape, F32) for _ in range(n_prod)] if nk > 1 else []
    sem = ("parallel",) * (len(grid) - 1) + ("arbitrary",)
    return pl.pallas_call(
        kern,
        out_shape=out_shapes,
        grid=grid,
        in_specs=list(a_specs) + list(b_specs) + list(extra_specs),
        out_specs=out_specs,
        scratch_shapes=scratch,
        input_output_aliases=aliases or {},
        compiler_params=_cparams(sem),
    )(*a_list, *b_list, *extras)


def _wspec(w, tk, tn):
    if isinstance(w, tuple):
        arr, lead = w
        return arr, pl.BlockSpec((None,) * len(lead) + (tk, tn), lambda i, j, k: tuple(lead) + (k, j))
    return w, pl.BlockSpec((tk, tn), lambda i, j, k: (k, j))


def _flat_mm(a, b, out_dtype, tm, tn, tk, epilogue=None, extras=(), extra_specs=(), bs=None,
             out_shapes=None, out_specs=None, aliases=None, n=None):
    m, kdim = a.shape
    bs = bs if bs is not None else [b]
    bs, b_specs = zip(*[_wspec(w, tk, tn) for w in bs])
    n = n if n is not None else bs[0].shape[-1]
    nk = kdim // tk
    grid = (m // tm, n // tn, nk)
    a_specs = [pl.BlockSpec((tm, tk), lambda i, j, k: (i, k))]
    if epilogue is None:
        def epilogue(p, ex, outs, pids):
            outs[0][...] = p[0].astype(outs[0].dtype)
    if out_shapes is None:
        out_shapes = [jax.ShapeDtypeStruct((m, n), out_dtype)]
        out_specs = [pl.BlockSpec((tm, tn), lambda i, j, k: (i, j))]
    return _fmm(grid, [a], a_specs, bs, b_specs, list(extras), list(extra_specs), out_shapes,
                out_specs, (tm, tn), nk, epilogue, aliases)


def _adaln_kernel(c_ref, w_ref, b_ref, o_ref):
    c = _silu(c_ref[...]).astype(BF16)
    o_ref[...] = jnp.dot(c, w_ref[...].astype(BF16), preferred_element_type=F32) + b_ref[...]


def _adaln(cond8, w_all, layer, bias):
    _, d, n = w_all.shape
    tn = 512
    return pl.pallas_call(
        _adaln_kernel,
        out_shape=jax.ShapeDtypeStruct((8, n), F32),
        grid=(n // tn,),
        in_specs=[pl.BlockSpec((8, d), lambda j: (0, 0)),
                  pl.BlockSpec((None, d, tn), lambda j: (layer, 0, j)),
                  pl.BlockSpec((1, tn), lambda j: (0, j))],
        out_specs=pl.BlockSpec((8, tn), lambda j: (0, j)),
        compiler_params=_cparams(("parallel",)),
    )(cond8, w_all, bias.reshape(1, n))


def _ctx_rows(tile_idx, tm, tiles_per_batch, seq):
    rows = (tile_idx % tiles_per_batch) * tm + lax.broadcasted_iota(jnp.int32, (tm, 1), 0)
    return rows >= seq


def _norm_mod_kernel(x_ref, g_ref, shl_ref, scl_ref, shc_ref, scc_ref, o_ref, *, tm, tpb, seq):
    x = x_ref[...]
    y = x * lax.rsqrt(jnp.mean(x * x, axis=-1, keepdims=True) + NORM_EPS) * g_ref[...]
    ctx = _ctx_rows(pl.program_id(0), tm, tpb, seq)
    sh = jnp.where(ctx, shc_ref[...], shl_ref[...])
    sc = jnp.where(ctx, scc_ref[...], scl_ref[...])
    o_ref[...] = (y * (1.0 + sc) + sh).astype(o_ref.dtype)


def _norm_mod(x, g, sh, sc, nb, tt, seq, out_dtype=BF16):
    m, d = x.shape
    tm = _pick(tt, (272, 256, 128))
    tpb = tt // tm
    lat = pl.BlockSpec((None, 1, d), lambda i: (i // tpb, 0, 0))
    ctx = pl.BlockSpec((None, 1, d), lambda i: (nb, 0, 0))
    return pl.pallas_call(
        functools.partial(_norm_mod_kernel, tm=tm, tpb=tpb, seq=seq),
        out_shape=jax.ShapeDtypeStruct((m, d), out_dtype),
        grid=(m // tm,),
        in_specs=[pl.BlockSpec((tm, d), lambda i: (i, 0)), pl.BlockSpec((1, d), lambda i: (0, 0)),
                  lat, lat, ctx, ctx],
        out_specs=pl.BlockSpec((tm, d), lambda i: (i, 0)),
        compiler_params=_cparams(("parallel",)),
    )(x, g.reshape(1, d), sh, sc, sh, sc)


def _rope128(x, cos, sin):
    lane = lax.broadcasted_iota(jnp.int32, x.shape, 1)
    first = (lane % (2 * ROPE_PAIRS)) < ROPE_PAIRS
    swapped = jnp.where(first, pltpu.roll(x, LANE - ROPE_PAIRS, 1), pltpu.roll(x, ROPE_PAIRS, 1))
    return x * cos + swapped * sin


def _head_norm_rope(x_lo, x_hi, gain_lo, gain_hi, cos, sin, scale):
    ss = jnp.sum(x_lo * x_lo, axis=-1, keepdims=True) + jnp.sum(x_hi * x_hi, axis=-1, keepdims=True)
    inv = lax.rsqrt(ss * (1.0 / QK_HEAD) + NORM_EPS)
    lo = x_lo * inv * gain_lo
    hi = _rope128(x_hi * inv * gain_hi, cos, sin)
    return lo * scale, hi * scale


def _mla_q_kernel(a_ref, g_ref, w_ref, gain_ref, cos_ref, sin_ref, q_ref, *, heads):
    a = a_ref[...]
    an = a * lax.rsqrt(jnp.mean(a * a, axis=-1, keepdims=True) + NORM_EPS) * g_ref[...]
    acc = jnp.dot(an.astype(BF16), w_ref[...], preferred_element_type=F32)
    cos, sin = cos_ref[...], sin_ref[...]
    gain = gain_ref[...]
    for h in range(heads):
        c0 = h * HEAD_PAD
        lo, hi = _head_norm_rope(acc[:, c0:c0 + LANE], acc[:, c0 + LANE:c0 + HEAD_PAD], gain[:, :LANE],
                                 gain[:, LANE:], cos, sin, Q_SCALE)
        q_ref[:, c0:c0 + LANE] = lo.astype(q_ref.dtype)
        q_ref[:, c0 + LANE:c0 + HEAD_PAD] = hi.astype(q_ref.dtype)


def _mla_kv_kernel(a_ref, g_ref, w_ref, kr_ref, gain_ref, cos_ref, sin_ref, k_ref, v_ref, *, heads):
    a = a_ref[...]
    an = a * lax.rsqrt(jnp.mean(a * a, axis=-1, keepdims=True) + NORM_EPS) * g_ref[...]
    acc = jnp.dot(an.astype(BF16), w_ref[...], preferred_element_type=F32)
    cos, sin = cos_ref[...], sin_ref[...]
    gain = gain_ref[...]
    kr = kr_ref[...]
    for h in range(heads):
        c0 = h * HEAD_PAD
        lo, hi = _head_norm_rope(acc[:, c0:c0 + LANE], kr, gain[:, :LANE], gain[:, LANE:], cos, sin, 1.0)
        k_ref[:, c0:c0 + LANE] = lo.astype(k_ref.dtype)
        k_ref[:, c0 + LANE:c0 + HEAD_PAD] = hi.astype(k_ref.dtype)
        v_ref[:, h * V_HEAD:(h + 1) * V_HEAD] = acc[:, c0 + LANE:c0 + HEAD_PAD].astype(v_ref.dtype)


def _mla_up(mla_in, q_norm, w_uq, q_gain, kv_norm, w_ukv, k_gain, cos, sin):
    m = mla_in.shape[0]
    tm = _pick(m, (544, 512, 256))
    hpt = 4
    tn = hpt * HEAD_PAD
    grid = (m // tm, MLA_HEADS // hpt)
    row = lambda w: pl.BlockSpec((tm, w), lambda i, j: (i, 0))
    q = pl.pallas_call(
        functools.partial(_mla_q_kernel, heads=hpt),
        out_shape=jax.ShapeDtypeStruct((m, MLA_HEADS * HEAD_PAD), BF16),
        grid=grid,
        in_specs=[pl.BlockSpec((tm, Q_LORA), lambda i, j: (i, 0)),
                  pl.BlockSpec((1, Q_LORA), lambda i, j: (0, 0)),
                  pl.BlockSpec((Q_LORA, tn), lambda i, j: (0, j)),
                  pl.BlockSpec((1, HEAD_PAD), lambda i, j: (0, 0)),
                  row(LANE), row(LANE)],
        out_specs=pl.BlockSpec((tm, tn), lambda i, j: (i, j)),
        compiler_params=_cparams(("parallel", "parallel")),
    )(mla_in, q_norm.reshape(1, -1), w_uq, q_gain, cos, sin)
    k, v = pl.pallas_call(
        functools.partial(_mla_kv_kernel, heads=hpt),
        out_shape=[jax.ShapeDtypeStruct((m, MLA_HEADS * HEAD_PAD), BF16),
                   jax.ShapeDtypeStruct((m, MLA_HEADS * V_HEAD), BF16)],
        grid=grid,
        in_specs=[pl.BlockSpec((tm, KV_LORA), lambda i, j: (i, Q_LORA // KV_LORA)),
                  pl.BlockSpec((1, KV_LORA), lambda i, j: (0, 0)),
                  pl.BlockSpec((KV_LORA, tn), lambda i, j: (0, j)),
                  pl.BlockSpec((tm, LANE), lambda i, j: (i, (Q_LORA + KV_LORA) // LANE)),
                  pl.BlockSpec((1, HEAD_PAD), lambda i, j: (0, 0)),
                  row(LANE), row(LANE)],
        out_specs=[pl.BlockSpec((tm, tn), lambda i, j: (i, j)),
                   pl.BlockSpec((tm, hpt * V_HEAD), lambda i, j: (i, j))],
        compiler_params=_cparams(("parallel", "parallel")),
    )(mla_in, kv_norm.reshape(1, -1), w_ukv, mla_in, k_gain, cos, sin)
    return q, k, v


def _softmax_pv(q, k, v, o_ref):
    s = lax.dot_general(q, k, (((1,), (1,)), ((), ())), preferred_element_type=F32)
    p = jnp.exp2(s - jnp.max(s, axis=-1, keepdims=True))
    l = jnp.sum(p, axis=-1, keepdims=True)
    o = jnp.dot(p.astype(BF16), v, preferred_element_type=F32)
    o_ref[...] = (o * (1.0 / l)).astype(o_ref.dtype)


def _attn_kernel(q_ref, k_ref, v_ref, o_ref):
    _softmax_pv(q_ref[...], k_ref[...], v_ref[...], o_ref)


def _attention(q, k, v, nb, tt, seq, with_ctx):
    q3 = q.reshape(nb, tt, MLA_HEADS * HEAD_PAD)
    k3 = k.reshape(nb, tt, MLA_HEADS * HEAD_PAD)
    v3 = v.reshape(nb, tt, MLA_HEADS * V_HEAD)
    tq = 256
    o_lat = pl.pallas_call(
        _attn_kernel,
        out_shape=jax.ShapeDtypeStruct((nb, seq, MLA_HEADS * V_HEAD), BF16),
        grid=(nb, MLA_HEADS, seq // tq),
        in_specs=[pl.BlockSpec((None, tq, HEAD_PAD), lambda b, h, i: (b, i, h)),
                  pl.BlockSpec((None, tt, HEAD_PAD), lambda b, h, i: (b, 0, h)),
                  pl.BlockSpec((None, tt, V_HEAD), lambda b, h, i: (b, 0, h))],
        out_specs=pl.BlockSpec((None, tq, V_HEAD), lambda b, h, i: (b, i, h)),
        compiler_params=_cparams(("parallel", "parallel", "arbitrary")),
    )(q3, k3, v3)
    nctx = tt - seq
    if with_ctx:
        cb = seq // nctx
        o_ctx = pl.pallas_call(
            _attn_kernel,
            out_shape=jax.ShapeDtypeStruct((nb, nctx, MLA_HEADS * V_HEAD), BF16),
            grid=(nb, MLA_HEADS),
            in_specs=[pl.BlockSpec((None, nctx, HEAD_PAD), lambda b, h: (b, cb, h)),
                      pl.BlockSpec((None, nctx, HEAD_PAD), lambda b, h: (b, cb, h)),
                      pl.BlockSpec((None, nctx, V_HEAD), lambda b, h: (b, cb, h))],
            out_specs=pl.BlockSpec((None, nctx, V_HEAD), lambda b, h: (b, 0, h)),
            compiler_params=_cparams(("parallel", "parallel")),
        )(q3, k3, v3)
    else:
        o_ctx = jnp.zeros((nb, nctx, MLA_HEADS * V_HEAD), BF16)
    return jnp.concatenate([o_lat, o_ctx], axis=1).reshape(nb * tt, MLA_HEADS * V_HEAD)


def _dft_angles(n, cols):
    rows = jnp.arange(n, dtype=jnp.int32)
    ang = ((rows[:, None] * cols[None, :]) % n).astype(F32) * (2.0 * math.pi / n)
    return jnp.cos(ang), jnp.sin(ang)


def _dft_mats(n, scale):
    if n % GRID_W or n <= GRID_W:
        c, s = _dft_angles(n, jnp.arange(n, dtype=jnp.int32))
    else:
        ca, sa = _dft_angles(n, jnp.arange(n // GRID_W, dtype=jnp.int32) * GRID_W)
        cb, sb = _dft_angles(n, jnp.arange(GRID_W, dtype=jnp.int32))
        c = (ca[:, :, None] * cb[:, None, :] - sa[:, :, None] * sb[:, None, :]).reshape(n, n)
        s = (sa[:, :, None] * cb[:, None, :] + ca[:, :, None] * sb[:, None, :]).reshape(n, n)
    return (c * scale).astype(BF16), (s * scale).astype(BF16)


def _fourier(f, nb, tt, seq, with_ctx, dft):
    c_t, sneg_t, c_c, s_c, c_x, s_x = dft
    m = f.shape[0]
    fz = f.reshape(m * FNET_GROUPS, FNET_GROUP_DIM)
    rows = fz.shape[0]
    tmz = _pick(rows, (4352, 2048, 1024))

    def ep2(p, ex, outs, pids):
        outs[0][...] = p[0].astype(BF16)
        outs[1][...] = p[1].astype(BF16)

    shp = jax.ShapeDtypeStruct((rows, FNET_GROUP_DIM), BF16)
    ospec = pl.BlockSpec((tmz, FNET_GROUP_DIM), lambda i, j, k: (i, j))
    zc, zs = _flat_mm(fz, None, BF16, tmz, FNET_GROUP_DIM, FNET_GROUP_DIM, epilogue=ep2, bs=[c_c, s_c],
                      out_shapes=[shp, shp], out_specs=[ospec, ospec])
    zc = zc.reshape(nb, tt, FNET_WIDTH)
    zs = zs.reshape(nb, tt, FNET_WIDTH)

    def ep_sum(p, ex, outs, pids):
        outs[0][...] = (p[0] + p[1]).astype(BF16)

    tm = _pick(seq, (1024, 512, 256))
    tk = tm
    tn = FNET_WIDTH
    nk = seq // tk
    aspec = pl.BlockSpec((tm, tk), lambda b, i, j, k: (i, k))
    bspec = pl.BlockSpec((None, tk, tn), lambda b, i, j, k: (b, k, j))
    y_lat = _fmm((nb, seq // tm, FNET_WIDTH // tn, nk), [c_t, sneg_t], [aspec, aspec], [zc, zs], [bspec, bspec],
                 [], [], [jax.ShapeDtypeStruct((nb, seq, FNET_WIDTH), BF16)],
                 [pl.BlockSpec((None, tm, tn), lambda b, i, j, k: (b, i, j))], (tm, tn), nk, ep_sum)[0]
    nctx = tt - seq
    if with_ctx:
        def ep_diff(p, ex, outs, pids):
            outs[0][...] = (p[0] - p[1]).astype(BF16)

        cb = seq // nctx
        aspec = pl.BlockSpec((nctx, nctx), lambda b, k: (0, 0))
        bspec = pl.BlockSpec((None, nctx, FNET_WIDTH), lambda b, k: (b, cb, 0))
        y_ctx = _fmm((nb, 1), [c_x, s_x], [aspec, aspec], [zc, zs], [bspec, bspec], [], [],
                     [jax.ShapeDtypeStruct((nb, nctx, FNET_WIDTH), BF16)],
                     [pl.BlockSpec((None, nctx, FNET_WIDTH), lambda b, k: (b, 0, 0))],
                     (nctx, FNET_WIDTH), 1, ep_diff)[0]
    else:
        y_ctx = jnp.zeros((nb, nctx, FNET_WIDTH), BF16)
    return jnp.concatenate([y_lat, y_ctx], axis=1).reshape(nb * tt, FNET_WIDTH)


def _head_sum(x, ones_bd):
    parts = []
    for s in range(x.shape[1] // LANE):
        hi, lo = _split2(x[:, s * LANE:(s + 1) * LANE])
        parts.append(jnp.dot(hi, ones_bd, preferred_element_type=F32)
                     + jnp.dot(lo, ones_bd, preferred_element_type=F32))
    return jnp.concatenate(parts, axis=-1)


def _feat_kernel(*refs, tf, tpb, lat_tiles, has_vres):
    (z_ref, zp_ref, zn_ref, mu_ref, w0_ref, w2_ref, a0_ref, a2_ref, kk_ref_p, ka_ref, bd_ref) = refs[:11]
    pos = 11
    if has_vres:
        vf_ref, v0_ref, v2_ref = refs[pos:pos + 3]
        pos += 3
    outs = refs[pos:]
    r_o, v_o, kk_o, kf_o, kb_o, lwf_o, lwb_o, af_o, ab_o, sg_o = outs[:10]
    j = pl.program_id(1)
    has_prev = jnp.logical_and(j != 0, j != lat_tiles)
    has_next = jnp.logical_and(j != lat_tiles - 1, j != tpb - 1)
    row = lax.broadcasted_iota(jnp.int32, (tf, 1), 0)

    def zf(lo, hi):
        z = z_ref[:, lo:hi]
        prev_row = jnp.where(has_prev, zp_ref[7:8, lo:hi], 0.0)
        next_row = jnp.where(has_next, zn_ref[0:1, lo:hi], 0.0)
        zp = jnp.where(row == 0, prev_row, pltpu.roll(z, 1, 0))
        zn = jnp.where(row == tf - 1, next_row, pltpu.roll(z, tf - 1, 0))
        return z + mu_ref[:, lo:hi] * (0.5 * (zp + zn) - z)

    w = RWKV_WIDTH
    r = zf(0, w)
    k = zf(w, 2 * w)
    v = zf(2 * w, 3 * w)
    r_o[...] = r
    if has_vres:
        vl = zf(Z_MAIN + GATE_PAD, Z_PAD)
        gate = _sigmoid(v0_ref[...] + _bdot(vl, v2_ref[...]))
        v_o[...] = v + (vf_ref[...] - v) * gate
    else:
        v_o[...] = v
        outs[10][...] = v
    kk = k * kk_ref_p[...]
    ss = _head_sum(kk * kk, bd_ref[...])
    kk_o[...] = kk * lax.rsqrt(ss + 1e-12)
    base = 3 * w
    for d, (lw_o, a_o, k_o) in enumerate(((lwf_o, af_o, kf_o), (lwb_o, ab_o, kb_o))):
        wd = zf(base + d * DECAY_LORA, base + (d + 1) * DECAY_LORA)
        ad = zf(base + 2 * DECAY_LORA + d * ICLR_LORA, base + 2 * DECAY_LORA + (d + 1) * ICLR_LORA)
        xw = w0_ref[d:d + 1, :] + _bdot(jnp.tanh(wd), w2_ref[d])
        lw_o[...] = -_sigmoid(xw) * math.exp(-0.5)
        a = _sigmoid(a0_ref[d:d + 1, :] + _bdot(ad, a2_ref[d]))
        a_o[...] = a
        k_o[...] = k * (1.0 + (a - 1.0) * ka_ref[...])
    sg_o[...] = _sigmoid(zf(Z_MAIN, Z_MAIN + GATE_PAD)).astype(sg_o.dtype)


def _rwkv_features(z, nb, tt, seq, mu, w0, w2, a0, a2, k_k, k_a, ones_bd, vres):
    tf = 128
    tpb = tt // tf
    lat_tiles = seq // tf
    z3 = z.reshape(nb, tt, Z_PAD)
    hb = tf // 8
    last8 = tt // 8 - 1
    w = RWKV_WIDTH
    full = lambda shape: pl.BlockSpec(shape, lambda b, j: (0,) * len(shape))
    tile = lambda width: pl.BlockSpec((None, tf, width), lambda b, j: (b, j, 0))
    in_specs = [tile(Z_PAD),
                pl.BlockSpec((None, 8, Z_PAD), lambda b, j: (b, jnp.maximum(j * hb - 1, 0), 0)),
                pl.BlockSpec((None, 8, Z_PAD), lambda b, j: (b, jnp.minimum((j + 1) * hb, last8), 0)),
                full((1, Z_PAD)), full((2, w)), full((2, DECAY_LORA, w)), full((2, w)), full((2, ICLR_LORA, w)),
                full((1, w)), full((1, w)), full((LANE, LANE))]
    args = [z3, z3, z3, mu.reshape(1, Z_PAD), w0, w2, a0, a2, k_k.reshape(1, w), k_a.reshape(1, w), ones_bd]
    has_vres = vres is not None
    if has_vres:
        v_first, v0, v2 = vres
        in_specs += [tile(w), full((1, w)), full((VRES_PAD, w))]
        args += [v_first.reshape(nb, tt, w), v0.reshape(1, w), v2]
    f3 = jax.ShapeDtypeStruct((nb, tt, w), F32)
    out_shape = [f3] * 9 + [jax.ShapeDtypeStruct((nb, tt, GATE_PAD), BF16)]
    out_specs = [tile(w)] * 9 + [tile(GATE_PAD)]
    if not has_vres:
        out_shape.append(f3)
        out_specs.append(tile(w))
    return pl.pallas_call(
        functools.partial(_feat_kernel, tf=tf, tpb=tpb, lat_tiles=lat_tiles, has_vres=has_vres),
        out_shape=out_shape,
        grid=(nb, tpb),
        in_specs=in_specs,
        out_specs=out_specs,
        compiler_params=_cparams(("parallel", "arbitrary")),
    )(*args)


def _stack2(x, lo_mask):
    return jnp.concatenate([jnp.where(lo_mask, x, 0.0), jnp.where(lo_mask, 0.0, x)], axis=0)


def _dot16(a, b):
    return jnp.dot(a.astype(BF16), b.astype(BF16), preferred_element_type=F32)


def _dg16(a, b, dims):
    return lax.dot_general(a.astype(BF16), b.astype(BF16), dims, preferred_element_type=F32)


def _bmm(a, b):
    return lax.dot_general(a.astype(BF16), b.astype(BF16), (((2,), (1,)), ((0,), (0,))),
                           preferred_element_type=F32)


def _bmm_nt(a, b):
    return lax.dot_general(a.astype(BF16), b.astype(BF16), (((2,), (2,)), ((0,), (0,))),
                           preferred_element_type=F32)


_GROUP_OPERANDS = ("kkt", "rt", "bh", "kh", "kb", "bb", "v")


def _group_operands(r, k, v, lw, kk, a, rev, consts, g):
    tri_cum, lo_mask = consts[0], consts[3]
    c = CHUNK
    s2 = lambda x: _stack2(x, lo_mask)
    cols = {name: [] for name in _GROUP_OPERANDS}
    tots = []
    for ch in range(g):
        sl = slice(ch * c, (ch + 1) * c)
        lw_c = lw[sl]
        cum = _dot_exact_rhs_left(tri_cum, lw_c)
        tot = cum[0:1, :] if rev else cum[c - 1:c, :]
        g_inv = jnp.exp(-cum)
        g_tail = jnp.exp(tot - cum)
        b_c = kk[sl] * a[sl]
        cols["kkt"].append(s2(kk[sl] * jnp.exp(cum - lw_c)))
        cols["rt"].append(s2(r[sl] * jnp.exp(cum)))
        cols["bh"].append(s2(b_c * g_inv))
        cols["kh"].append(s2(k[sl] * g_inv))
        cols["kb"].append(s2(k[sl] * g_tail))
        cols["bb"].append(s2(b_c * g_tail))
        cols["v"].append(s2(v[sl]))
        tots.append(tot)
    return {name: jnp.concatenate(cols[name], axis=0) for name in _GROUP_OPERANDS}, tots


def _solve_groups(ops, consts):
    _, strict, incl, _, eye_big, _, base_mask, level_masks = consts
    rt = ops["rt"]
    kkt16, rt16, bh16, kh16, v16 = (ops[nm].astype(BF16) for nm in ("kkt", "rt", "bh", "kh", "v"))
    nr = rt.shape[1]
    lhs = jnp.concatenate([kkt16, rt16], axis=1)
    big_b = _bmm_nt(lhs, bh16)
    big_k = _bmm_nt(lhs, kh16)
    m_k = jnp.where(strict, big_k[:, :nr], 0.0)
    a_qb = jnp.where(incl, big_b[:, nr:], 0.0)
    a_qk = jnp.where(incl, big_k[:, nr:], 0.0)
    m_b = jnp.where(strict, big_b[:, :nr], 0.0)
    n = jnp.where(base_mask, -m_b, 0.0)
    tinv = eye_big + n
    for _ in range(int(math.log2(INV_BASE)) - 1):
        n = _bmm(n, n)
        tinv = tinv + _bmm(tinv, n)
    for off_mask in level_masks:
        tinv = tinv - _bmm(_bmm(tinv, jnp.where(off_mask, m_b, 0.0)), tinv)
    mkv = _bmm(m_k, v16)
    wu16 = _bmm(tinv, jnp.concatenate([kkt16, mkv.astype(BF16)], axis=2)).astype(BF16)
    qy = _bmm(a_qb, wu16)
    q16 = (rt - qy[:, :, :LANE]).astype(BF16)
    y0 = _bmm(a_qk, v16) - qy[:, :, LANE:]
    return wu16, q16, y0


def _dot_exact_rhs_left(tri_bf16, x):
    h, m, l = _split3(x)
    d = functools.partial(jnp.dot, preferred_element_type=F32)
    return d(tri_bf16, h) + (d(tri_bf16, m) + d(tri_bf16, l))


def _scan_kernel(r_ref, k_ref, v_ref, lw_ref, kk_ref, a_ref, y_ref, h_ref, *, rev, n_groups, g):
    @pl.when(pl.program_id(2) == 0)
    def _():
        h_ref[...] = jnp.zeros_like(h_ref)

    c = CHUNK
    ri = lax.broadcasted_iota(jnp.int32, (c, c), 0)
    ci = lax.broadcasted_iota(jnp.int32, (c, c), 1)
    tri_cum = (ci >= ri if rev else ci <= ri).astype(BF16)
    nr = g * 2 * c
    r2 = lax.broadcasted_iota(jnp.int32, (nr, nr), 0)
    c2 = lax.broadcasted_iota(jnp.int32, (nr, nr), 1)
    same = (r2 // (2 * c)) == (c2 // (2 * c))
    strict = jnp.logical_and(same, c2 > r2 if rev else c2 < r2)
    incl = jnp.logical_and(same, c2 >= r2 if rev else c2 <= r2)
    eye_big = (r2 == c2).astype(F32)
    eye = eye_big[:LANE, :LANE]
    lo_mask = lax.broadcasted_iota(jnp.int32, (c, LANE), 1) < RWKV_HEAD_DIM
    base_mask = (r2 // INV_BASE) == (c2 // INV_BASE)
    level_masks = []
    s = 2 * INV_BASE
    while s <= c:
        level_masks.append(jnp.logical_and((r2 // s) == (c2 // s), (r2 // (s // 2)) != (c2 // (s // 2))))
        s *= 2
    consts = (tri_cum, strict, incl, lo_mask, eye_big, eye, base_mask, level_masks)
    gl = g * c
    rows = 2 * c
    units = [(pi, gi) for pi in range(SCAN_PAIRS) for gi in range(n_groups)]
    ops_list, tots_list = [], []
    for pi, gi in units:
        sl = slice(gi * gl, (gi + 1) * gl)
        ls = slice(pi * LANE, (pi + 1) * LANE)
        ops, tots = _group_operands(r_ref[sl, ls], k_ref[sl, ls], v_ref[sl, ls], lw_ref[sl, ls], kk_ref[sl, ls],
                                    a_ref[sl, ls], rev, consts, g)
        ops_list.append(ops)
        tots_list.append(tots)
    stacked = {name: jnp.stack([ops[name] for ops in ops_list]) for name in _GROUP_OPERANDS}
    wu16, q16, y0 = _solve_groups(stacked, consts)
    kb16 = stacked["kb"].astype(BF16)
    bb16 = stacked["bb"].astype(BF16)
    v16 = stacked["v"].astype(BF16)
    tn = (((0,), (0,)), ((), ()))
    for pi in range(SCAN_PAIRS):
        h = h_ref[pi]
        for gi in (range(n_groups - 1, -1, -1) if rev else range(n_groups)):
            u = units.index((pi, gi))
            for ch in (range(g - 1, -1, -1) if rev else range(g)):
                rs = slice(ch * rows, (ch + 1) * rows)
                pg = _dg16(bb16[u, rs], wu16[u, rs], tn)
                p_mat = eye * jnp.exp(tots_list[u][ch]) - pg[:, :LANE]
                g_mat = _dg16(kb16[u, rs], v16[u, rs], tn) - pg[:, LANE:]
                h16 = h.astype(BF16)
                y_st = _dot16(q16[u, rs], h16) + y0[u, rs]
                h = _dot16(p_mat, h16) + g_mat
                t0 = gi * gl + ch * c
                y_ref[t0:t0 + c, pi * LANE:(pi + 1) * LANE] = y_st[:c] + y_st[c:]
        h_ref[pi] = h


def _rwkv_scan(r, k, v, lw, kk, a, rev, seq):
    nb, tt, w = r.shape
    nblk = tt // SCAN_BLOCK
    lat_blk = seq // SCAN_BLOCK
    if rev:
        def blk(j):
            return nblk - 1 - j
    else:
        def blk(j):
            return jnp.where(j < nblk - lat_blk, lat_blk + j, j - (nblk - lat_blk))
    spec = pl.BlockSpec((None, SCAN_BLOCK, SCAN_PAIRS * LANE), lambda b, p, j: (b, blk(j), p))
    return pl.pallas_call(
        functools.partial(_scan_kernel, rev=rev, n_groups=SCAN_BLOCK // (CHUNK * SCAN_GROUP), g=SCAN_GROUP),
        out_shape=jax.ShapeDtypeStruct((nb, tt, w), F32),
        grid=(nb, w // (SCAN_PAIRS * LANE), nblk),
        in_specs=[spec] * 6,
        out_specs=spec,
        scratch_shapes=[pltpu.VMEM((SCAN_PAIRS, LANE, LANE), F32)],
        compiler_params=_cparams(("parallel", "parallel", "arbitrary")),
    )(r, k, v, lw, kk, a)


def _readout_kernel(yf_ref, yb_ref, r_ref, kf_ref, kb_ref, v_ref, sg_ref, g2_ref, rk_ref, lw_ref, lb_ref, bd_ref,
                    o_ref):
    bd = bd_ref[...]
    inv = 1.0 / RWKV_HEAD_DIM
    y = yf_ref[...] + yb_ref[...]
    mean = _head_sum(y, bd) * inv
    dlt = y - mean
    var = _head_sum(dlt * dlt, bd) * inv
    yn = dlt * lax.rsqrt(var + LNX_EPS) * lw_ref[...] + lb_ref[...]
    bonus = _head_sum(r_ref[...] * (kf_ref[...] + kb_ref[...]) * rk_ref[...], bd) * v_ref[...]
    g = jnp.dot(sg_ref[...], g2_ref[...], preferred_element_type=F32)
    o_ref[...] = ((yn + bonus) * g).astype(o_ref.dtype)


def _rwkv_readout(yf, yb, r, kf, kb, v, sg, g2, r_k, lnx_w, lnx_b, ones_bd):
    m, w = yf.shape
    tm = _pick(m, (272, 256, 128))
    row = lambda width: pl.BlockSpec((tm, width), lambda i: (i, 0))
    full = lambda shape: pl.BlockSpec(shape, lambda i: (0,) * len(shape))
    return pl.pallas_call(
        _readout_kernel,
        out_shape=jax.ShapeDtypeStruct((m, w), BF16),
        grid=(m // tm,),
        in_specs=[row(w)] * 6 + [row(GATE_PAD), full((GATE_PAD, w)), full((1, w)), full((1, w)), full((1, w)),
                                 full((LANE, LANE))],
        out_specs=row(w),
        compiler_params=_cparams(("parallel",)),
    )(yf, yb, r, kf, kb, v, sg, g2, r_k.reshape(1, w), lnx_w.reshape(1, w), lnx_b.reshape(1, w), ones_bd)


def _merge(gates, o_a, o_b, o_c, w_a, w_b, w_c):
    m = o_a.shape[0]
    d = w_a[0].shape[-1]
    tm = _pick(m, (1088, 512, 256))
    tn = 512

    def ep(p, ex, outs, pids):
        y = (_sigmoid(ex[0][...].astype(F32)) * p[0] + _sigmoid(ex[1][...].astype(F32)) * p[1]
             + _sigmoid(ex[2][...].astype(F32)) * p[2])
        outs[0][...] = y.astype(BF16)

    nbn = d // tn
    a_specs = [pl.BlockSpec((tm, o.shape[1]), lambda i, j, k: (i, 0)) for o in (o_a, o_b, o_c)]
    ws, b_specs = zip(*[_wspec(wt, wt[0].shape[-2], tn) for wt in (w_a, w_b, w_c)])
    g_specs = [pl.BlockSpec((tm, tn), functools.partial(lambda i, j, k, off: (i, off + j), off=br * nbn))
               for br in range(N_BRANCHES)]
    return _fmm((m // tm, nbn, 1), [o_a, o_b, o_c], a_specs, ws, b_specs, [gates] * 3, g_specs,
                [jax.ShapeDtypeStruct((m, d), BF16)], [pl.BlockSpec((tm, tn), lambda i, j, k: (i, j))],
                (tm, tn), 1, ep)[0]


def _mm_resid(a, w, x, gate, nb, tt, seq, tk, tn):
    m, kdim = a.shape
    tm = _pick(tt, (1088, 512, 256))
    tpb = tt // tm
    nk = kdim // tk
    w, wspec = _wspec(w, tk, tn)
    d = w.shape[-1]

    def ep(p, ex, outs, pids):
        ctx = _ctx_rows(pids[0], tm, tpb, seq)
        g = jnp.where(ctx, ex[2][...], ex[1][...])
        outs[0][...] = ex[0][...] + g * p[0]

    xspec = pl.BlockSpec((tm, tn), lambda i, j, k: (i, j))
    extra_specs = [xspec,
                   pl.BlockSpec((None, 1, tn), lambda i, j, k: (i // tpb, 0, j)),
                   pl.BlockSpec((None, 1, tn), lambda i, j, k: (nb, 0, j))]
    return _fmm((m // tm, d // tn, nk), [a], [pl.BlockSpec((tm, tk), lambda i, j, k: (i, k))],
                [w], [wspec], [x, gate, gate], extra_specs,
                [jax.ShapeDtypeStruct((m, d), F32)], [xspec], (tm, tn), nk, ep, aliases={2: 0})[0]


def _swiglu(u, w1, w3, tn, tk):
    m = u.shape[0]
    tm = _pick(m, (1088, 512, 256))

    def ep(p, ex, outs, pids):
        outs[0][...] = (_silu(p[0]) * p[1]).astype(BF16)

    return _flat_mm(u, None, BF16, tm, tn, tk, epilogue=ep, bs=[w1, w3])[0]


ROUTE_LANE = 8


def _router_kernel(u_ref, w_ref, o_ref):
    logits = _dot_x3(u_ref[...].astype(F32), w_ref[...])
    lane = lax.broadcasted_iota(jnp.int32, logits.shape, 1).astype(F32)
    valid = lane < N_EXPERTS
    neg = -1e30
    lg = jnp.where(valid, logits, neg)
    m1 = jnp.max(lg, axis=-1, keepdims=True)
    i1 = jnp.min(jnp.where(lg == m1, lane, float(LANE)), axis=-1, keepdims=True)
    lg2 = jnp.where(lane == i1, neg, lg)
    m2 = jnp.max(lg2, axis=-1, keepdims=True)
    i2 = jnp.min(jnp.where(lg2 == m2, lane, float(LANE)), axis=-1, keepdims=True)
    e2 = jnp.exp(m2 - m1)
    w1 = 1.0 / (1.0 + e2)
    w2 = e2 / (1.0 + e2)
    comb = jnp.where(lane == i1, w1, 0.0) + jnp.where(lane == i2, w2, 0.0)
    meta = (jnp.where(lane == ROUTE_LANE, i1, 0.0) + jnp.where(lane == ROUTE_LANE + 1, i2, 0.0)
            + jnp.where(lane == ROUTE_LANE + 2, w1, 0.0) + jnp.where(lane == ROUTE_LANE + 3, w2, 0.0))
    o_ref[...] = comb + meta


def _router(u, router_pad):
    m, d = u.shape
    tm = _pick(m, (544, 512, 256))
    return pl.pallas_call(
        _router_kernel,
        out_shape=jax.ShapeDtypeStruct((m, LANE), F32),
        grid=(m // tm,),
        in_specs=[pl.BlockSpec((tm, d), lambda i: (i, 0)), pl.BlockSpec((d, LANE), lambda i: (0, 0))],
        out_specs=pl.BlockSpec((tm, LANE), lambda i: (i, 0)),
        compiler_params=_cparams(("parallel",)),
    )(u, router_pad)


MOE_TILE = 512
GATHER_TILE = 256


def _route_plan(e1, e2, tm):
    n_tok = e1.shape[0]
    e_flat = jnp.stack([e1, e2], axis=1).reshape(-1)
    n_asg = e_flat.shape[0]
    onehot = (e_flat[:, None] == jnp.arange(N_EXPERTS, dtype=jnp.int32)[None, :]).astype(jnp.int32)
    rank = jnp.sum((jnp.cumsum(onehot, axis=0) - onehot) * onehot, axis=1)
    counts = jnp.sum(onehot, axis=0)
    padded = ((counts + tm - 1) // tm) * tm
    ends = jnp.cumsum(padded)
    slot = (ends - padded)[e_flat] + rank
    n_slots = n_asg + N_EXPERTS * tm
    src_tok = jnp.zeros((n_slots,), jnp.int32).at[slot].set(jnp.arange(n_asg, dtype=jnp.int32) // TOP_K)
    tile_start = jnp.arange(n_slots // tm, dtype=jnp.int32) * tm
    tile_e = jnp.sum((tile_start[:, None] >= ends[None, :]).astype(jnp.int32), axis=1)
    valid = (tile_start < ends[-1]).astype(jnp.int32)
    tile_e = jnp.where(valid == 1, tile_e, tile_e[jnp.maximum(ends[-1] // tm - 1, 0)])
    return src_tok, slot.reshape(n_tok, TOP_K), tile_e, valid


def _row_copy(src_hbm, row, buf, r, sem):
    return pltpu.make_async_copy(src_hbm.at[pl.ds(row, 1)], buf.at[pl.ds(r, 1)], sem)


def _gather_kernel(idx_ref, src_hbm, o_ref, buf, sem, *, tg):
    base = pl.program_id(0) * tg

    def start(r, carry):
        _row_copy(src_hbm, idx_ref[base + r], buf, r, sem).start()
        return carry

    def wait(r, carry):
        _row_copy(src_hbm, 0, buf, r, sem).wait()
        return carry

    lax.fori_loop(0, tg, start, 0, unroll=8)
    lax.fori_loop(0, tg, wait, 0, unroll=8)
    o_ref[...] = buf[...].astype(o_ref.dtype)


def _gather_rows(src, idx, out_dtype):
    n_slots = idx.shape[0]
    d = src.shape[1]
    tg = GATHER_TILE
    return pl.pallas_call(
        functools.partial(_gather_kernel, tg=tg),
        out_shape=jax.ShapeDtypeStruct((n_slots, d), out_dtype),
        grid_spec=pltpu.PrefetchScalarGridSpec(
            num_scalar_prefetch=1,
            grid=(n_slots // tg,),
            in_specs=[pl.BlockSpec(memory_space=pl.ANY)],
            out_specs=pl.BlockSpec((tg, d), lambda i, idx_ref: (i, 0)),
            scratch_shapes=[pltpu.VMEM((tg, d), src.dtype), pltpu.SemaphoreType.DMA(())]),
        compiler_params=_cparams(("arbitrary",)),
    )(idx, src)


def _grouped_up_kernel(te_ref, tv_ref, a_ref, w1_ref, w3_ref, o_ref):
    valid = tv_ref[pl.program_id(1)] == 1

    @pl.when(valid)
    def _():
        a = a_ref[...]
        p1 = jnp.dot(a, w1_ref[...].astype(BF16), preferred_element_type=F32)
        p3 = jnp.dot(a, w3_ref[...].astype(BF16), preferred_element_type=F32)
        o_ref[...] = (_silu(p1) * p3).astype(o_ref.dtype)

    @pl.when(jnp.logical_not(valid))
    def _():
        o_ref[...] = jnp.zeros_like(o_ref)


def _grouped_down_kernel(te_ref, tv_ref, a_ref, w_ref, o_ref):
    valid = tv_ref[pl.program_id(1)] == 1

    @pl.when(valid)
    def _():
        o_ref[...] = jnp.dot(a_ref[...], w_ref[...].astype(BF16), preferred_element_type=F32)

    @pl.when(jnp.logical_not(valid))
    def _():
        o_ref[...] = jnp.zeros_like(o_ref)


def _grouped_mm(kern, a, ws, layer, tile_e, tile_valid, tn, out_dtype):
    n_slots, kdim = a.shape
    n = ws[0].shape[-1]
    tm = MOE_TILE
    wspec = pl.BlockSpec((None, None, kdim, tn), lambda j, i, te, tv: (layer, te[i], 0, j))
    return pl.pallas_call(
        kern,
        out_shape=jax.ShapeDtypeStruct((n_slots, n), out_dtype),
        grid_spec=pltpu.PrefetchScalarGridSpec(
            num_scalar_prefetch=2,
            grid=(n // tn, n_slots // tm),
            in_specs=[pl.BlockSpec((tm, kdim), lambda j, i, te, tv: (i, 0))] + [wspec] * len(ws),
            out_specs=pl.BlockSpec((tm, tn), lambda j, i, te, tv: (i, j))),
        compiler_params=_cparams(("parallel", "arbitrary")),
    )(tile_e, tile_valid, a, *ws)


def _combine_kernel(s1_ref, s2_ref, ys_hbm, x_ref, rw_ref, gate_ref, o_ref, buf1, buf2, sem, *, tc, spb):
    base = (pl.program_id(0) * spb + pl.program_id(1)) * tc

    def start(r, carry):
        _row_copy(ys_hbm, s1_ref[base + r], buf1, r, sem).start()
        _row_copy(ys_hbm, s2_ref[base + r], buf2, r, sem).start()
        return carry

    def wait(r, carry):
        _row_copy(ys_hbm, 0, buf1, r, sem).wait()
        _row_copy(ys_hbm, 0, buf2, r, sem).wait()
        return carry

    lax.fori_loop(0, tc, start, 0, unroll=8)
    lax.fori_loop(0, tc, wait, 0, unroll=8)
    rw = rw_ref[...]
    lane = lax.broadcasted_iota(jnp.int32, rw.shape, 1)
    w1 = jnp.sum(jnp.where(lane == ROUTE_LANE + 2, rw, 0.0), axis=-1, keepdims=True)
    w2 = jnp.sum(jnp.where(lane == ROUTE_LANE + 3, rw, 0.0), axis=-1, keepdims=True)
    o_ref[...] = x_ref[...] + gate_ref[...] * (w1 * buf1[...] + w2 * buf2[...])


def _moe_combine(x, ys, slots, route, gate, nb, tt, seq):
    m, d = x.shape
    tc = GATHER_TILE
    spb = seq // tc
    bpb = tt // tc
    row = lambda b, i, s1, s2: (b * bpb + i, 0)
    return pl.pallas_call(
        functools.partial(_combine_kernel, tc=tc, spb=spb),
        out_shape=jax.ShapeDtypeStruct((m, d), F32),
        grid_spec=pltpu.PrefetchScalarGridSpec(
            num_scalar_prefetch=2,
            grid=(nb, spb),
            in_specs=[pl.BlockSpec(memory_space=pl.ANY),
                      pl.BlockSpec((tc, d), row),
                      pl.BlockSpec((tc, LANE), row),
                      pl.BlockSpec((None, 1, d), lambda b, i, s1, s2: (b, 0, 0))],
            out_specs=pl.BlockSpec((tc, d), row),
            scratch_shapes=[pltpu.VMEM((tc, d), F32), pltpu.VMEM((tc, d), F32), pltpu.SemaphoreType.DMA(())]),
        input_output_aliases={3: 0},
        compiler_params=_cparams(("arbitrary", "arbitrary")),
    )(slots[:, 0], slots[:, 1], ys, x, route, gate)


def _moe_routed(x, u, route, moe_w1, moe_w3, moe_w2, layer, gate, nb, tt, seq):
    m, d = x.shape
    meta = route.reshape(nb, tt, LANE)[:, :seq].reshape(nb * seq, LANE)
    e1 = meta[:, ROUTE_LANE].astype(jnp.int32)
    e2 = meta[:, ROUTE_LANE + 1].astype(jnp.int32)
    src_tok, slots, tile_e, tile_valid = _route_plan(e1, e2, MOE_TILE)
    src_row = (src_tok // seq) * tt + src_tok % seq
    xs = _gather_rows(u, src_row, BF16)
    h = _grouped_mm(_grouped_up_kernel, xs, [moe_w1, moe_w3], layer, tile_e, tile_valid, 512, BF16)
    ys = _grouped_mm(_grouped_down_kernel, h, [moe_w2], layer, tile_e, tile_valid, 1024, F32)
    return _moe_combine(x, ys, slots, route, gate, nb, tt, seq)


def _rope_lane_tables(seq, nctx, nb):
    rows = seq // GRID_W
    row = jnp.repeat(jnp.arange(rows, dtype=F32), GRID_W)
    col = jnp.tile(jnp.arange(GRID_W, dtype=F32), rows)
    inv_freq = ROPE_BASE ** (-2.0 * jnp.arange(ROPE_PAIRS, dtype=F32) / ROPE_AXIS_DIM)
    ar = row[:, None] * inv_freq
    ac = col[:, None] * inv_freq
    ones = jnp.ones((seq, LANE - QK_ROPE), F32)
    zeros = jnp.zeros((seq, LANE - QK_ROPE), F32)
    cos = jnp.concatenate([jnp.cos(ar), jnp.cos(ar), jnp.cos(ac), jnp.cos(ac), ones], axis=1)
    sin = jnp.concatenate([-jnp.sin(ar), jnp.sin(ar), -jnp.sin(ac), jnp.sin(ac), zeros], axis=1)
    cos = jnp.concatenate([cos, jnp.ones((nctx, LANE), F32)], axis=0)
    sin = jnp.concatenate([sin, jnp.zeros((nctx, LANE), F32)], axis=0)
    return jnp.tile(cos, (nb, 1)), jnp.tile(sin, (nb, 1))


def _pad_cols(w, n):
    return jnp.pad(w, ((0, 0), (0, n - w.shape[1])))


def _pad_rows(w, n):
    return jnp.pad(w, ((0, n - w.shape[0]), (0, 0)))


def _z_layout(zsrc, vres_src):
    lead = zsrc.shape[:-1]
    parts = [zsrc[..., :Z_MAIN], zsrc[..., Z_MAIN:], jnp.zeros(lead + (GATE_PAD - GATE_LORA,), zsrc.dtype)]
    if vres_src is None:
        parts.append(jnp.zeros(lead + (VRES_PAD,), zsrc.dtype))
    else:
        parts += [vres_src, jnp.zeros(lead + (VRES_PAD - VRES_LORA,), zsrc.dtype)]
    return jnp.concatenate(parts, axis=-1)


def _pad_head(g):
    return jnp.pad(g, (0, HEAD_PAD - QK_HEAD)).reshape(1, HEAD_PAD)


def kernel(x, c, ctx, c_ctx, ada_w, ada_b, norm1, w_in, w_vres_down, q_norm, w_uq, kv_norm, w_ukv, q_gain, k_gain,
           rwkv_mu, vres_mu, w0, w2, a0, a2, k_k, k_a, v0, v2, r_k, lnx_w, lnx_b, g2, w_br_a, w_br_b, w_br_c,
           w_out, norm2, ffn_w1, ffn_w3, ffn_w2, router, moe_w1, moe_w3, moe_w2):
    nb, seq, d = x.shape
    nctx = ctx.shape[1]
    tt = seq + nctx
    m = nb * tt
    depth = ada_w.shape[0]
    xa = jnp.concatenate([x, ctx], axis=1).reshape(m, d)
    cond8 = jnp.concatenate([c, c_ctx[None], jnp.zeros((8 - nb - 1, d), F32)], axis=0)
    cos_t, sin_t = _rope_lane_tables(seq, nctx, nb)
    c_t, s_t = _dft_mats(seq, 1.0 / math.sqrt(seq))
    c_c, s_c = _dft_mats(FNET_GROUP_DIM, 1.0 / math.sqrt(FNET_GROUP_DIM))
    c_x, s_x = _dft_mats(nctx, 1.0 / math.sqrt(nctx))
    dft = (c_t, -s_t, c_c, s_c, c_x, s_x)
    li = jnp.arange(LANE)
    ones_bd = (li[:, None] // RWKV_HEAD_DIM == li[None, :] // RWKV_HEAD_DIM).astype(BF16)
    v_first = None
    gate_cols = N_BRANCHES * d
    w_in_bf = w_in.astype(BF16)
    for i in range(depth):
        last = i == depth - 1
        mod = _adaln(cond8, ada_w, i, ada_b[i])[:nb + 1].reshape(nb + 1, 6, 1, d)
        sh1, sc1, gm, sh2, sc2, gf = (mod[:, t] for t in range(6))
        wi = w_in_bf[i]
        w_mla = _pad_cols(wi[:, gate_cols:gate_cols + MLA_IN], MLA_IN_PAD).astype(BF16)
        f0 = gate_cols + MLA_IN
        w_f = wi[:, f0:f0 + FNET_WIDTH].astype(BF16)
        zsrc = wi[:, f0 + FNET_WIDTH:]
        if i == 0:
            w_z = _z_layout(zsrc, None).astype(BF16)
            mu_z = _z_layout(rwkv_mu[i], None)
        else:
            w_z = _z_layout(zsrc, w_vres_down[i - 1]).astype(BF16)
            mu_z = _z_layout(rwkv_mu[i], vres_mu[i - 1])

        u = _norm_mod(xa, norm1[i], sh1, sc1, nb, tt, seq)
        tm = _pick(m, (1088, 512, 256))
        gates = _flat_mm(u, (w_in_bf, (i,)), BF16, tm, 512, d, n=gate_cols)[0]
        mla_in = _flat_mm(u, w_mla, F32, _pick(m, (544, 512, 256)), MLA_IN_PAD, 2048)[0]
        f = _flat_mm(u, w_f, BF16, tm, 512, d)[0]
        z = _flat_mm(u, w_z, F32, _pick(m, (544, 512, 256)), Z_PAD // 3, d)[0]

        w_uq_p = jnp.pad(w_uq[i].reshape(Q_LORA, MLA_HEADS, QK_HEAD),
                         ((0, 0), (0, 0), (0, HEAD_PAD - QK_HEAD))).reshape(Q_LORA, MLA_HEADS * HEAD_PAD)
        q, k, v = _mla_up(mla_in, q_norm[i], w_uq_p.astype(BF16), _pad_head(q_gain[i]), kv_norm[i],
                          w_ukv[i].astype(BF16), _pad_head(k_gain[i]), cos_t, sin_t)
        o_a = _attention(q, k, v, nb, tt, seq, not last)

        o_b = _fourier(f, nb, tt, seq, not last, dft)

        vres = None
        if i > 0:
            vres = (v_first, v0[i - 1], _pad_rows(v2[i - 1], VRES_PAD).astype(BF16))
        feat = _rwkv_features(z, nb, tt, seq, mu_z, w0[i], w2[i].astype(BF16), a0[i], a2[i].astype(BF16),
                              k_k[i], k_a[i], ones_bd, vres)
        r_, v_, kk_, kf_, kb_, lwf_, lwb_, af_, ab_, sg_ = feat[:10]
        if i == 0:
            v_first = feat[10]
        y_f = _rwkv_scan(r_, kf_, v_, lwf_, kk_, af_, False, seq)
        y_b = _rwkv_scan(r_, kb_, v_, lwb_, kk_, ab_, True, seq)
        flat = lambda t: t.reshape(m, t.shape[-1])
        o_c = _rwkv_readout(flat(y_f), flat(y_b), flat(r_), flat(kf_), flat(kb_), flat(v_), flat(sg_),
                            _pad_rows(g2[i], GATE_PAD).astype(BF16), r_k[i], lnx_w[i], lnx_b[i], ones_bd)

        mixed = _merge(gates, o_a, o_b, o_c, (w_br_a, (i,)), (w_br_b, (i,)), (w_br_c, (i,)))
        xa = _mm_resid(mixed, (w_out, (i,)), xa, gm, nb, tt, seq, d, 512)

        jj = i // 2
        u2 = _norm_mod(xa, norm2[i], sh2, sc2, nb, tt, seq, BF16 if i % 2 == 0 else F32)
        if i % 2 == 0:
            h = _swiglu(u2, _pad_cols(ffn_w1[jj], D_FF_PAD).astype(BF16),
                        _pad_cols(ffn_w3[jj], D_FF_PAD).astype(BF16), 512, d)
            xa = _mm_resid(h, _pad_rows(ffn_w2[jj], D_FF_PAD).astype(BF16), xa, gf, nb, tt, seq, D_FF_PAD // 4, 1024)
        else:
            route = _router(u2, _pad_cols(router[jj], LANE))
            xa = _moe_routed(xa, u2, route, moe_w1, moe_w3, moe_w2, jj, gf, nb, tt, seq)
    return xa.reshape(nb, tt, d)[:, :seq]
```

```python
import functools
import math

import jax
import jax.numpy as jnp
import numpy as np
from jax import lax
from jax.experimental import pallas as pl
from jax.experimental.pallas import tpu as pltpu

F32 = jnp.float32
BF16 = jnp.bfloat16

D_MODEL = 4096
GRID_W = 64
NORM_EPS = 1e-6
MLA_HEADS = 16
Q_LORA = 1024
KV_LORA = 512
QK_NOPE = 128
QK_ROPE = 64
QK_HEAD = QK_NOPE + QK_ROPE
V_HEAD = 128
HEAD_PAD = 256
Q_SCALE = QK_HEAD ** -0.5 * math.log2(math.e)
ROPE_AXIS_DIM = QK_ROPE // 2
ROPE_PAIRS = ROPE_AXIS_DIM // 2
ROPE_BASE = 10000.0
FNET_GROUPS = 4
FNET_GROUP_DIM = 256
FNET_WIDTH = FNET_GROUPS * FNET_GROUP_DIM
RWKV_HEADS = 16
RWKV_HEAD_DIM = 64
RWKV_WIDTH = RWKV_HEADS * RWKV_HEAD_DIM
DECAY_LORA = 128
ICLR_LORA = 128
VRES_LORA = 96
GATE_LORA = 480
GATE_PAD = 512
VRES_PAD = 128
LNX_EPS = 64e-5
N_BRANCHES = 3
MLA_IN = Q_LORA + KV_LORA + QK_ROPE
MLA_IN_PAD = Q_LORA + KV_LORA + 128
Z_MAIN = 3 * RWKV_WIDTH + 2 * DECAY_LORA + 2 * ICLR_LORA
Z_PAD = Z_MAIN + GATE_PAD + VRES_PAD
D_FF = 11008
D_FF_PAD = 11264
N_EXPERTS = 8
TOP_K = 2
D_FF_EXPERT = 3072

LANE = 128
CHUNK = 64
SCAN_BLOCK = 256
SCAN_GROUP = 1
SCAN_PAIRS = 4
INV_BASE = 16
VMEM_LIMIT = 52 * 1024 * 1024


def _cparams(sem):
    return pltpu.CompilerParams(dimension_semantics=sem, vmem_limit_bytes=VMEM_LIMIT)


def _pick(n, cands):
    for c in cands:
        if n % c == 0:
            return c
    raise ValueError(f"no tile for {n} in {cands}")


def _silu(x):
    return x * (1.0 / (1.0 + jnp.exp(-x)))


def _sigmoid(x):
    return 1.0 / (1.0 + jnp.exp(-x))


def _bdot(a, b):
    return jnp.dot(a.astype(BF16), b.astype(BF16), preferred_element_type=F32)


def _split2(x):
    hi = x.astype(BF16)
    lo = (x - hi.astype(F32)).astype(BF16)
    return hi, lo


def _split3(x):
    hi = x.astype(BF16)
    r1 = x - hi.astype(F32)
    mid = r1.astype(BF16)
    lo = (r1 - mid.astype(F32)).astype(BF16)
    return hi, mid, lo


def _dot_x3(a, b):
    ah, al = _split2(a)
    bh, bl = _split2(b)
    d = functools.partial(jnp.dot, preferred_element_type=F32)
    return d(ah, bh) + (d(ah, bl) + d(al, bh))


def _dot_exact_rhs(a, b_bf16):
    h, m, l = _split3(a)
    d = functools.partial(jnp.dot, preferred_element_type=F32)
    return d(h, b_bf16) + (d(m, b_bf16) + d(l, b_bf16))


def _fmm_kernel(*refs, n_a, n_prod, n_extra, n_out, nk, kaxis, epilogue, wt):
    a = refs[:n_a]
    b = refs[n_a:n_a + n_prod]
    ex = refs[n_a + n_prod:n_a + n_prod + n_extra]
    outs = refs[n_a + n_prod + n_extra:n_a + n_prod + n_extra + n_out]
    accs = refs[n_a + n_prod + n_extra + n_out:]
    pids = [pl.program_id(ax) for ax in range(kaxis + 1)]
    dims = (((1,), (1 if wt else 0,)), ((), ()))
    prods = [lax.dot_general(a[i % n_a][...], b[i][...].astype(BF16), dims, preferred_element_type=F32)
             for i in range(n_prod)]
    if nk == 1:
        epilogue(prods, ex, outs, pids)
        return
    k = pids[kaxis]

    @pl.when(k == 0)
    def _():
        for i in range(n_prod):
            accs[i][...] = prods[i]

    @pl.when(k > 0)
    def _():
        for i in range(n_prod):
            accs[i][...] += prods[i]

    @pl.when(k == nk - 1)
    def _():
        epilogue([acc[...] for acc in accs], ex, outs, pids)


def _fmm(grid, a_list, a_specs, b_list, b_specs, extras, extra_specs, out_shapes, out_specs,
         acc_shape, nk, epilogue, aliases=None, wt=False):
    n_prod = len(b_list)
    kern = functools.partial(_fmm_kernel, n_a=len(a_list), n_prod=n_prod, n_extra=len(extras), n_out=len(out_shapes),
                             nk=nk, kaxis=len(grid) - 1, epilogue=epilogue, wt=wt)
    scratch = [pltpu.VMEM(acc_shape, F32) for _ in range(n_prod)] if nk > 1 else []
    sem = ("parallel",) * (len(grid) - 1) + ("arbitrary",)
    return pl.pallas_call(
        kern,
        out_shape=out_shapes,
        grid=grid,
        in_specs=list(a_specs) + list(b_specs) + list(extra_specs),
        out_specs=out_specs,
        scratch_shapes=scratch,
        input_output_aliases=aliases or {},
        compiler_params=_cparams(sem),
    )(*a_list, *b_list, *extras)


def _wspec(w, tk, tn, wt=False):
    arr, lead = w if isinstance(w, tuple) else (w, ())
    if wt:
        return arr, pl.BlockSpec((None,) * len(lead) + (tn, tk), lambda i, j, k: tuple(lead) + (j, k))
    return arr, pl.BlockSpec((None,) * len(lead) + (tk, tn), lambda i, j, k: tuple(lead) + (k, j))


def _flat_mm(a, b, out_dtype, tm, tn, tk, epilogue=None, extras=(), extra_specs=(), bs=None,
             out_shapes=None, out_specs=None, aliases=None, n=None, wt=False):
    m, kdim = a.shape
    bs = bs if bs is not None else [b]
    bs, b_specs = zip(*[_wspec(w, tk, tn, wt) for w in bs])
    n = n if n is not None else bs[0].shape[-2 if wt else -1]
    nk = kdim // tk
    grid = (m // tm, n // tn, nk)
    a_specs = [pl.BlockSpec((tm, tk), lambda i, j, k: (i, k))]
    if epilogue is None:
        def epilogue(p, ex, outs, pids):
            outs[0][...] = p[0].astype(outs[0].dtype)
    if out_shapes is None:
        out_shapes = [jax.ShapeDtypeStruct((m, n), out_dtype)]
        out_specs = [pl.BlockSpec((tm, tn), lambda i, j, k: (i, j))]
    return _fmm(grid, [a], a_specs, bs, b_specs, list(extras), list(extra_specs), out_shapes,
                out_specs, (tm, tn), nk, epilogue, aliases, wt)


def _adaln_kernel(c_ref, w_ref, b_ref, o_ref):
    c = _silu(c_ref[...]).astype(BF16)
    o_ref[...] = jnp.dot(c, w_ref[...].astype(BF16), preferred_element_type=F32) + b_ref[...]


def _adaln(cond8, w_all, layer, bias):
    _, d, n = w_all.shape
    tn = 512
    return pl.pallas_call(
        _adaln_kernel,
        out_shape=jax.ShapeDtypeStruct((8, n), F32),
        grid=(n // tn,),
        in_specs=[pl.BlockSpec((8, d), lambda j: (0, 0)),
                  pl.BlockSpec((None, d, tn), lambda j: (layer, 0, j)),
                  pl.BlockSpec((1, tn), lambda j: (0, j))],
        out_specs=pl.BlockSpec((8, tn), lambda j: (0, j)),
        compiler_params=_cparams(("parallel",)),
    )(cond8, w_all, bias.reshape(1, n))


def _ctx_rows(tile_idx, tm, tiles_per_batch, seq):
    rows = (tile_idx % tiles_per_batch) * tm + lax.broadcasted_iota(jnp.int32, (tm, 1), 0)
    return rows >= seq


def _norm_mod_kernel(x_ref, g_ref, shl_ref, scl_ref, shc_ref, scc_ref, o_ref, *, tm, tpb, seq):
    x = x_ref[...]
    y = x * lax.rsqrt(jnp.mean(x * x, axis=-1, keepdims=True) + NORM_EPS) * g_ref[...]
    ctx = _ctx_rows(pl.program_id(0), tm, tpb, seq)
    sh = jnp.where(ctx, shc_ref[...], shl_ref[...])
    sc = jnp.where(ctx, scc_ref[...], scl_ref[...])
    o_ref[...] = (y * (1.0 + sc) + sh).astype(o_ref.dtype)


def _norm_mod(x, g, sh, sc, nb, tt, seq, out_dtype=BF16):
    m, d = x.shape
    tm = _pick(tt, (272, 256, 128))
    tpb = tt // tm
    lat = pl.BlockSpec((None, 1, d), lambda i: (i // tpb, 0, 0))
    ctx = pl.BlockSpec((None, 1, d), lambda i: (nb, 0, 0))
    return pl.pallas_call(
        functools.partial(_norm_mod_kernel, tm=tm, tpb=tpb, seq=seq),
        out_shape=jax.ShapeDtypeStruct((m, d), out_dtype),
        grid=(m // tm,),
        in_specs=[pl.BlockSpec((tm, d), lambda i: (i, 0)), pl.BlockSpec((1, d), lambda i: (0, 0)),
                  lat, lat, ctx, ctx],
        out_specs=pl.BlockSpec((tm, d), lambda i: (i, 0)),
        compiler_params=_cparams(("parallel",)),
    )(x, g.reshape(1, d), sh, sc, sh, sc)


def _rope128(x, cos, sin):
    lane = lax.broadcasted_iota(jnp.int32, x.shape, 1)
    first = (lane % (2 * ROPE_PAIRS)) < ROPE_PAIRS
    swapped = jnp.where(first, pltpu.roll(x, LANE - ROPE_PAIRS, 1), pltpu.roll(x, ROPE_PAIRS, 1))
    return x * cos + swapped * sin


def _head_norm_rope(x_lo, x_hi, gain_lo, gain_hi, cos, sin, scale):
    ss = jnp.sum(x_lo * x_lo, axis=-1, keepdims=True) + jnp.sum(x_hi * x_hi, axis=-1, keepdims=True)
    inv = lax.rsqrt(ss * (1.0 / QK_HEAD) + NORM_EPS)
    lo = x_lo * inv * gain_lo
    hi = _rope128(x_hi * inv * gain_hi, cos, sin)
    return lo * scale, hi * scale


def _mla_q_kernel(a_ref, g_ref, w_ref, gain_ref, cos_ref, sin_ref, q_ref, *, heads):
    a = a_ref[...]
    an = a * lax.rsqrt(jnp.mean(a * a, axis=-1, keepdims=True) + NORM_EPS) * g_ref[...]
    acc = jnp.dot(an.astype(BF16), w_ref[...], preferred_element_type=F32)
    cos, sin = cos_ref[...], sin_ref[...]
    gain = gain_ref[...]
    for h in range(heads):
        c0 = h * HEAD_PAD
        lo, hi = _head_norm_rope(acc[:, c0:c0 + LANE], acc[:, c0 + LANE:c0 + HEAD_PAD], gain[:, :LANE],
                                 gain[:, LANE:], cos, sin, Q_SCALE)
        q_ref[:, c0:c0 + LANE] = lo.astype(q_ref.dtype)
        q_ref[:, c0 + LANE:c0 + HEAD_PAD] = hi.astype(q_ref.dtype)


def _mla_kv_kernel(a_ref, g_ref, w_ref, kr_ref, gain_ref, cos_ref, sin_ref, k_ref, v_ref, *, heads):
    a = a_ref[...]
    an = a * lax.rsqrt(jnp.mean(a * a, axis=-1, keepdims=True) + NORM_EPS) * g_ref[...]
    acc = jnp.dot(an.astype(BF16), w_ref[...], preferred_element_type=F32)
    cos, sin = cos_ref[...], sin_ref[...]
    gain = gain_ref[...]
    kr = kr_ref[...]
    for h in range(heads):
        c0 = h * HEAD_PAD
        lo, hi = _head_norm_rope(acc[:, c0:c0 + LANE], kr, gain[:, :LANE], gain[:, LANE:], cos, sin, 1.0)
        k_ref[:, c0:c0 + LANE] = lo.astype(k_ref.dtype)
        k_ref[:, c0 + LANE:c0 + HEAD_PAD] = hi.astype(k_ref.dtype)
        v_ref[:, h * V_HEAD:(h + 1) * V_HEAD] = acc[:, c0 + LANE:c0 + HEAD_PAD].astype(v_ref.dtype)


def _mla_up(mla_in, q_norm, w_uq, q_gain, kv_norm, w_ukv, k_gain, cos, sin):
    m = mla_in.shape[0]
    tm = _pick(m, (544, 512, 256))
    hpt = 4
    tn = hpt * HEAD_PAD
    grid = (m // tm, MLA_HEADS // hpt)
    row = lambda w: pl.BlockSpec((tm, w), lambda i, j: (i, 0))
    q = pl.pallas_call(
        functools.partial(_mla_q_kernel, heads=hpt),
        out_shape=jax.ShapeDtypeStruct((m, MLA_HEADS * HEAD_PAD), BF16),
        grid=grid,
        in_specs=[pl.BlockSpec((tm, Q_LORA), lambda i, j: (i, 0)),
                  pl.BlockSpec((1, Q_LORA), lambda i, j: (0, 0)),
                  pl.BlockSpec((Q_LORA, tn), lambda i, j: (0, j)),
                  pl.BlockSpec((1, HEAD_PAD), lambda i, j: (0, 0)),
                  row(LANE), row(LANE)],
        out_specs=pl.BlockSpec((tm, tn), lambda i, j: (i, j)),
        compiler_params=_cparams(("parallel", "parallel")),
    )(mla_in, q_norm.reshape(1, -1), w_uq, q_gain, cos, sin)
    k, v = pl.pallas_call(
        functools.partial(_mla_kv_kernel, heads=hpt),
        out_shape=[jax.ShapeDtypeStruct((m, MLA_HEADS * HEAD_PAD), BF16),
                   jax.ShapeDtypeStruct((m, MLA_HEADS * V_HEAD), BF16)],
        grid=grid,
        in_specs=[pl.BlockSpec((tm, KV_LORA), lambda i, j: (i, Q_LORA // KV_LORA)),
                  pl.BlockSpec((1, KV_LORA), lambda i, j: (0, 0)),
                  pl.BlockSpec((KV_LORA, tn), lambda i, j: (0, j)),
                  pl.BlockSpec((tm, LANE), lambda i, j: (i, (Q_LORA + KV_LORA) // LANE)),
                  pl.BlockSpec((1, HEAD_PAD), lambda i, j: (0, 0)),
                  row(LANE), row(LANE)],
        out_specs=[pl.BlockSpec((tm, tn), lambda i, j: (i, j)),
                   pl.BlockSpec((tm, hpt * V_HEAD), lambda i, j: (i, j))],
        compiler_params=_cparams(("parallel", "parallel")),
    )(mla_in, kv_norm.reshape(1, -1), w_ukv, mla_in, k_gain, cos, sin)
    return q, k, v


def _softmax_pv(q, k, v, o_ref):
    s = lax.dot_general(q, k, (((1,), (1,)), ((), ())), preferred_element_type=F32)
    p = jnp.exp2(s - jnp.max(s, axis=-1, keepdims=True))
    l = jnp.sum(p, axis=-1, keepdims=True)
    o = jnp.dot(p.astype(BF16), v, preferred_element_type=F32)
    o_ref[...] = (o * (1.0 / l)).astype(o_ref.dtype)


def _attn_kernel(q_ref, k_ref, v_ref, o_ref):
    _softmax_pv(q_ref[...], k_ref[...], v_ref[...], o_ref)


def _attention(q, k, v, nb, tt, seq, with_ctx):
    q3 = q.reshape(nb, tt, MLA_HEADS * HEAD_PAD)
    k3 = k.reshape(nb, tt, MLA_HEADS * HEAD_PAD)
    v3 = v.reshape(nb, tt, MLA_HEADS * V_HEAD)
    tq = 256
    o_lat = pl.pallas_call(
        _attn_kernel,
        out_shape=jax.ShapeDtypeStruct((nb, seq, MLA_HEADS * V_HEAD), BF16),
        grid=(nb, MLA_HEADS, seq // tq),
        in_specs=[pl.BlockSpec((None, tq, HEAD_PAD), lambda b, h, i: (b, i, h)),
                  pl.BlockSpec((None, tt, HEAD_PAD), lambda b, h, i: (b, 0, h)),
                  pl.BlockSpec((None, tt, V_HEAD), lambda b, h, i: (b, 0, h))],
        out_specs=pl.BlockSpec((None, tq, V_HEAD), lambda b, h, i: (b, i, h)),
        compiler_params=_cparams(("parallel", "parallel", "arbitrary")),
    )(q3, k3, v3)
    nctx = tt - seq
    if with_ctx:
        cb = seq // nctx
        o_ctx = pl.pallas_call(
            _attn_kernel,
            out_shape=jax.ShapeDtypeStruct((nb, nctx, MLA_HEADS * V_HEAD), BF16),
            grid=(nb, MLA_HEADS),
            in_specs=[pl.BlockSpec((None, nctx, HEAD_PAD), lambda b, h: (b, cb, h)),
                      pl.BlockSpec((None, nctx, HEAD_PAD), lambda b, h: (b, cb, h)),
                      pl.BlockSpec((None, nctx, V_HEAD), lambda b, h: (b, cb, h))],
            out_specs=pl.BlockSpec((None, nctx, V_HEAD), lambda b, h: (b, 0, h)),
            compiler_params=_cparams(("parallel", "parallel")),
        )(q3, k3, v3)
    else:
        o_ctx = jnp.zeros((nb, nctx, MLA_HEADS * V_HEAD), BF16)
    return jnp.concatenate([o_lat, o_ctx], axis=1).reshape(nb * tt, MLA_HEADS * V_HEAD)


def _dft_angles(n, cols):
    rows = jnp.arange(n, dtype=jnp.int32)
    ang = ((rows[:, None] * cols[None, :]) % n).astype(F32) * (2.0 * math.pi / n)
    return jnp.cos(ang), jnp.sin(ang)


def _dft_mats(n, scale):
    if n % GRID_W or n <= GRID_W:
        c, s = _dft_angles(n, jnp.arange(n, dtype=jnp.int32))
    else:
        ca, sa = _dft_angles(n, jnp.arange(n // GRID_W, dtype=jnp.int32) * GRID_W)
        cb, sb = _dft_angles(n, jnp.arange(GRID_W, dtype=jnp.int32))
        c = (ca[:, :, None] * cb[:, None, :] - sa[:, :, None] * sb[:, None, :]).reshape(n, n)
        s = (sa[:, :, None] * cb[:, None, :] + ca[:, :, None] * sb[:, None, :]).reshape(n, n)
    return (c * scale).astype(BF16), (s * scale).astype(BF16)


def _chan_dft_kernel(f_ref, c_ref, s_ref, zc_ref, zs_ref):
    for g in range(FNET_GROUPS):
        cols = slice(g * FNET_GROUP_DIM, (g + 1) * FNET_GROUP_DIM)
        fz = f_ref[:, cols]
        zc_ref[:, cols] = jnp.dot(fz, c_ref[...], preferred_element_type=F32).astype(zc_ref.dtype)
        zs_ref[:, cols] = jnp.dot(fz, s_ref[...], preferred_element_type=F32).astype(zs_ref.dtype)


def _fourier(f, nb, tt, seq, with_ctx, dft):
    c_t, sneg_t, c_c, s_c, c_x, s_x = dft
    m = f.shape[0]
    tmz = _pick(m, (1088, 512, 256))
    row = pl.BlockSpec((tmz, FNET_WIDTH), lambda i: (i, 0))
    mat = pl.BlockSpec((FNET_GROUP_DIM, FNET_GROUP_DIM), lambda i: (0, 0))
    shp = jax.ShapeDtypeStruct((m, FNET_WIDTH), BF16)
    zc, zs = pl.pallas_call(
        _chan_dft_kernel,
        out_shape=[shp, shp],
        grid=(m // tmz,),
        in_specs=[row, mat, mat],
        out_specs=[row, row],
        compiler_params=_cparams(("parallel",)),
    )(f, c_c, s_c)
    zc = zc.reshape(nb, tt, FNET_WIDTH)
    zs = zs.reshape(nb, tt, FNET_WIDTH)

    def ep_sum(p, ex, outs, pids):
        outs[0][...] = (p[0] + p[1]).astype(BF16)

    tm = _pick(seq, (1024, 512, 256))
    tk = tm
    tn = FNET_WIDTH
    nk = seq // tk
    aspec = pl.BlockSpec((tm, tk), lambda b, i, j, k: (i, k))
    bspec = pl.BlockSpec((None, tk, tn), lambda b, i, j, k: (b, k, j))
    y_lat = _fmm((nb, seq // tm, FNET_WIDTH // tn, nk), [c_t, sneg_t], [aspec, aspec], [zc, zs], [bspec, bspec],
                 [], [], [jax.ShapeDtypeStruct((nb, seq, FNET_WIDTH), BF16)],
                 [pl.BlockSpec((None, tm, tn), lambda b, i, j, k: (b, i, j))], (tm, tn), nk, ep_sum)[0]
    nctx = tt - seq
    if with_ctx:
        def ep_diff(p, ex, outs, pids):
            outs[0][...] = (p[0] - p[1]).astype(BF16)

        cb = seq // nctx
        aspec = pl.BlockSpec((nctx, nctx), lambda b, k: (0, 0))
        bspec = pl.BlockSpec((None, nctx, FNET_WIDTH), lambda b, k: (b, cb, 0))
        y_ctx = _fmm((nb, 1), [c_x, s_x], [aspec, aspec], [zc, zs], [bspec, bspec], [], [],
                     [jax.ShapeDtypeStruct((nb, nctx, FNET_WIDTH), BF16)],
                     [pl.BlockSpec((None, nctx, FNET_WIDTH), lambda b, k: (b, 0, 0))],
                     (nctx, FNET_WIDTH), 1, ep_diff)[0]
    else:
        y_ctx = jnp.zeros((nb, nctx, FNET_WIDTH), BF16)
    return jnp.concatenate([y_lat, y_ctx], axis=1).reshape(nb * tt, FNET_WIDTH)


def _head_sum(x, ones_bd):
    parts = []
    for s in range(x.shape[1] // LANE):
        hi, lo = _split2(x[:, s * LANE:(s + 1) * LANE])
        parts.append(jnp.dot(hi, ones_bd, preferred_element_type=F32)
                     + jnp.dot(lo, ones_bd, preferred_element_type=F32))
    return jnp.concatenate(parts, axis=-1)


def _feat_kernel(*refs, tf, tpb, lat_tiles, has_vres):
    (z_ref, zp_ref, zn_ref, mu_ref, w0_ref, w2_ref, a0_ref, a2_ref, kk_ref_p, ka_ref, bd_ref) = refs[:11]
    pos = 11
    if has_vres:
        vf_ref, v0_ref, v2_ref = refs[pos:pos + 3]
        pos += 3
    outs = refs[pos:]
    r_o, v_o, kk_o, kf_o, kb_o, lwf_o, lwb_o, af_o, ab_o, sg_o = outs[:10]
    j = pl.program_id(1)
    has_prev = jnp.logical_and(j != 0, j != lat_tiles)
    has_next = jnp.logical_and(j != lat_tiles - 1, j != tpb - 1)
    row = lax.broadcasted_iota(jnp.int32, (tf, 1), 0)

    def zf(lo, hi):
        z = z_ref[:, lo:hi]
        prev_row = jnp.where(has_prev, zp_ref[7:8, lo:hi], 0.0)
        next_row = jnp.where(has_next, zn_ref[0:1, lo:hi], 0.0)
        zp = jnp.where(row == 0, prev_row, pltpu.roll(z, 1, 0))
        zn = jnp.where(row == tf - 1, next_row, pltpu.roll(z, tf - 1, 0))
        return z + mu_ref[:, lo:hi] * (0.5 * (zp + zn) - z)

    w = RWKV_WIDTH
    r = zf(0, w)
    k = zf(w, 2 * w)
    v = zf(2 * w, 3 * w)
    r_o[...] = r
    if has_vres:
        vl = zf(Z_MAIN + GATE_PAD, Z_PAD)
        gate = _sigmoid(v0_ref[...] + _bdot(vl, v2_ref[...]))
        v_o[...] = v + (vf_ref[...] - v) * gate
    else:
        v_o[...] = v
        outs[10][...] = v
    kk = k * kk_ref_p[...]
    ss = _head_sum(kk * kk, bd_ref[...])
    kk_o[...] = kk * lax.rsqrt(ss + 1e-12)
    base = 3 * w
    for d, (lw_o, a_o, k_o) in enumerate(((lwf_o, af_o, kf_o), (lwb_o, ab_o, kb_o))):
        wd = zf(base + d * DECAY_LORA, base + (d + 1) * DECAY_LORA)
        ad = zf(base + 2 * DECAY_LORA + d * ICLR_LORA, base + 2 * DECAY_LORA + (d + 1) * ICLR_LORA)
        xw = w0_ref[d:d + 1, :] + _bdot(jnp.tanh(wd), w2_ref[d])
        lw_o[...] = -_sigmoid(xw) * math.exp(-0.5)
        a = _sigmoid(a0_ref[d:d + 1, :] + _bdot(ad, a2_ref[d]))
        a_o[...] = a
        k_o[...] = k * (1.0 + (a - 1.0) * ka_ref[...])
    sg_o[...] = _sigmoid(zf(Z_MAIN, Z_MAIN + GATE_PAD)).astype(sg_o.dtype)


def _rwkv_features(z, nb, tt, seq, mu, w0, w2, a0, a2, k_k, k_a, ones_bd, vres):
    tf = 128
    tpb = tt // tf
    lat_tiles = seq // tf
    z3 = z.reshape(nb, tt, Z_PAD)
    hb = tf // 8
    last8 = tt // 8 - 1
    w = RWKV_WIDTH
    full = lambda shape: pl.BlockSpec(shape, lambda b, j: (0,) * len(shape))
    tile = lambda width: pl.BlockSpec((None, tf, width), lambda b, j: (b, j, 0))
    in_specs = [tile(Z_PAD),
                pl.BlockSpec((None, 8, Z_PAD), lambda b, j: (b, jnp.maximum(j * hb - 1, 0), 0)),
                pl.BlockSpec((None, 8, Z_PAD), lambda b, j: (b, jnp.minimum((j + 1) * hb, last8), 0)),
                full((1, Z_PAD)), full((2, w)), full((2, DECAY_LORA, w)), full((2, w)), full((2, ICLR_LORA, w)),
                full((1, w)), full((1, w)), full((LANE, LANE))]
    args = [z3, z3, z3, mu.reshape(1, Z_PAD), w0, w2, a0, a2, k_k.reshape(1, w), k_a.reshape(1, w), ones_bd]
    has_vres = vres is not None
    if has_vres:
        v_first, v0, v2 = vres
        in_specs += [tile(w), full((1, w)), full((VRES_PAD, w))]
        args += [v_first.reshape(nb, tt, w), v0.reshape(1, w), v2]
    f3 = jax.ShapeDtypeStruct((nb, tt, w), F32)
    out_shape = [f3] * 9 + [jax.ShapeDtypeStruct((nb, tt, GATE_PAD), BF16)]
    out_specs = [tile(w)] * 9 + [tile(GATE_PAD)]
    if not has_vres:
        out_shape.append(f3)
        out_specs.append(tile(w))
    return pl.pallas_call(
        functools.partial(_feat_kernel, tf=tf, tpb=tpb, lat_tiles=lat_tiles, has_vres=has_vres),
        out_shape=out_shape,
        grid=(nb, tpb),
        in_specs=in_specs,
        out_specs=out_specs,
        compiler_params=_cparams(("parallel", "arbitrary")),
    )(*args)


def _stack2(x, lo_mask):
    return jnp.concatenate([jnp.where(lo_mask, x, 0.0), jnp.where(lo_mask, 0.0, x)], axis=0)


def _dot16(a, b):
    return jnp.dot(a.astype(BF16), b.astype(BF16), preferred_element_type=F32)


def _dg16(a, b, dims):
    return lax.dot_general(a.astype(BF16), b.astype(BF16), dims, preferred_element_type=F32)


def _bmm(a, b):
    return lax.dot_general(a.astype(BF16), b.astype(BF16), (((2,), (1,)), ((0,), (0,))),
                           preferred_element_type=F32)


def _bmm_nt(a, b):
    return lax.dot_general(a.astype(BF16), b.astype(BF16), (((2,), (2,)), ((0,), (0,))),
                           preferred_element_type=F32)


_GROUP_OPERANDS = ("kkt", "rt", "bh", "kh", "kb", "bb", "v")


def _group_operands(r, k, v, lw, kk, a, rev, consts, g):
    tri_cum, lo_mask = consts[0], consts[3]
    c = CHUNK
    s2 = lambda x: _stack2(x, lo_mask)
    cols = {name: [] for name in _GROUP_OPERANDS}
    tots = []
    for ch in range(g):
        sl = slice(ch * c, (ch + 1) * c)
        lw_c = lw[sl]
        cum = _dot_exact_rhs_left(tri_cum, lw_c)
        tot = cum[0:1, :] if rev else cum[c - 1:c, :]
        g_inv = jnp.exp(-cum)
        g_tail = jnp.exp(tot - cum)
        b_c = kk[sl] * a[sl]
        cols["kkt"].append(s2(kk[sl] * jnp.exp(cum - lw_c)))
        cols["rt"].append(s2(r[sl] * jnp.exp(cum)))
        cols["bh"].append(s2(b_c * g_inv))
        cols["kh"].append(s2(k[sl] * g_inv))
        cols["kb"].append(s2(k[sl] * g_tail))
        cols["bb"].append(s2(b_c * g_tail))
        cols["v"].append(s2(v[sl]))
        tots.append(tot)
    return {name: jnp.concatenate(cols[name], axis=0) for name in _GROUP_OPERANDS}, tots


def _solve_groups(ops, consts):
    _, strict, incl, _, eye_big, _, base_mask, level_masks = consts
    rt = ops["rt"]
    kkt16, rt16, bh16, kh16, v16 = (ops[nm].astype(BF16) for nm in ("kkt", "rt", "bh", "kh", "v"))
    nr = rt.shape[1]
    lhs = jnp.concatenate([kkt16, rt16], axis=1)
    big_b = _bmm_nt(lhs, bh16)
    big_k = _bmm_nt(lhs, kh16)
    m_k = jnp.where(strict, big_k[:, :nr], 0.0)
    a_qb = jnp.where(incl, big_b[:, nr:], 0.0)
    a_qk = jnp.where(incl, big_k[:, nr:], 0.0)
    m_b = jnp.where(strict, big_b[:, :nr], 0.0)
    n = jnp.where(base_mask, -m_b, 0.0)
    tinv = eye_big + n
    for _ in range(int(math.log2(INV_BASE)) - 1):
        n = _bmm(n, n)
        tinv = tinv + _bmm(tinv, n)
    for off_mask in level_masks:
        tinv = tinv - _bmm(_bmm(tinv, jnp.where(off_mask, m_b, 0.0)), tinv)
    mkv = _bmm(m_k, v16)
    wu16 = _bmm(tinv, jnp.concatenate([kkt16, mkv.astype(BF16)], axis=2)).astype(BF16)
    qy = _bmm(a_qb, wu16)
    q16 = (rt - qy[:, :, :LANE]).astype(BF16)
    y0 = _bmm(a_qk, v16) - qy[:, :, LANE:]
    return wu16, q16, y0


def _dot_exact_rhs_left(tri_bf16, x):
    h, m, l = _split3(x)
    d = functools.partial(jnp.dot, preferred_element_type=F32)
    return d(tri_bf16, h) + (d(tri_bf16, m) + d(tri_bf16, l))


def _scan_kernel(r_ref, k_ref, v_ref, lw_ref, kk_ref, a_ref, y_ref, h_ref, *, rev, n_groups, g):
    @pl.when(pl.program_id(2) == 0)
    def _():
        h_ref[...] = jnp.zeros_like(h_ref)

    c = CHUNK
    ri = lax.broadcasted_iota(jnp.int32, (c, c), 0)
    ci = lax.broadcasted_iota(jnp.int32, (c, c), 1)
    tri_cum = (ci >= ri if rev else ci <= ri).astype(BF16)
    nr = g * 2 * c
    r2 = lax.broadcasted_iota(jnp.int32, (nr, nr), 0)
    c2 = lax.broadcasted_iota(jnp.int32, (nr, nr), 1)
    same = (r2 // (2 * c)) == (c2 // (2 * c))
    strict = jnp.logical_and(same, c2 > r2 if rev else c2 < r2)
    incl = jnp.logical_and(same, c2 >= r2 if rev else c2 <= r2)
    eye_big = (r2 == c2).astype(F32)
    eye = eye_big[:LANE, :LANE]
    lo_mask = lax.broadcasted_iota(jnp.int32, (c, LANE), 1) < RWKV_HEAD_DIM
    base_mask = (r2 // INV_BASE) == (c2 // INV_BASE)
    level_masks = []
    s = 2 * INV_BASE
    while s <= c:
        level_masks.append(jnp.logical_and((r2 // s) == (c2 // s), (r2 // (s // 2)) != (c2 // (s // 2))))
        s *= 2
    consts = (tri_cum, strict, incl, lo_mask, eye_big, eye, base_mask, level_masks)
    gl = g * c
    rows = 2 * c
    units = [(pi, gi) for pi in range(SCAN_PAIRS) for gi in range(n_groups)]
    ops_list, tots_list = [], []
    for pi, gi in units:
        sl = slice(gi * gl, (gi + 1) * gl)
        ls = slice(pi * LANE, (pi + 1) * LANE)
        ops, tots = _group_operands(r_ref[sl, ls], k_ref[sl, ls], v_ref[sl, ls], lw_ref[sl, ls], kk_ref[sl, ls],
                                    a_ref[sl, ls], rev, consts, g)
        ops_list.append(ops)
        tots_list.append(tots)
    stacked = {name: jnp.stack([ops[name] for ops in ops_list]) for name in _GROUP_OPERANDS}
    wu16, q16, y0 = _solve_groups(stacked, consts)
    kb16 = stacked["kb"].astype(BF16)
    bb16 = stacked["bb"].astype(BF16)
    v16 = stacked["v"].astype(BF16)
    tn = (((0,), (0,)), ((), ()))
    for pi in range(SCAN_PAIRS):
        h = h_ref[pi]
        for gi in (range(n_groups - 1, -1, -1) if rev else range(n_groups)):
            u = units.index((pi, gi))
            for ch in (range(g - 1, -1, -1) if rev else range(g)):
                rs = slice(ch * rows, (ch + 1) * rows)
                pg = _dg16(bb16[u, rs], wu16[u, rs], tn)
                p_mat = eye * jnp.exp(tots_list[u][ch]) - pg[:, :LANE]
                g_mat = _dg16(kb16[u, rs], v16[u, rs], tn) - pg[:, LANE:]
                h16 = h.astype(BF16)
                y_st = _dot16(q16[u, rs], h16) + y0[u, rs]
                h = _dot16(p_mat, h16) + g_mat
                t0 = gi * gl + ch * c
                y_ref[t0:t0 + c, pi * LANE:(pi + 1) * LANE] = y_st[:c] + y_st[c:]
        h_ref[pi] = h


def _rwkv_scan(r, k, v, lw, kk, a, rev, seq):
    nb, tt, w = r.shape
    nblk = tt // SCAN_BLOCK
    lat_blk = seq // SCAN_BLOCK
    if rev:
        def blk(j):
            return nblk - 1 - j
    else:
        def blk(j):
            return jnp.where(j < nblk - lat_blk, lat_blk + j, j - (nblk - lat_blk))
    spec = pl.BlockSpec((None, SCAN_BLOCK, SCAN_PAIRS * LANE), lambda b, p, j: (b, blk(j), p))
    return pl.pallas_call(
        functools.partial(_scan_kernel, rev=rev, n_groups=SCAN_BLOCK // (CHUNK * SCAN_GROUP), g=SCAN_GROUP),
        out_shape=jax.ShapeDtypeStruct((nb, tt, w), F32),
        grid=(nb, w // (SCAN_PAIRS * LANE), nblk),
        in_specs=[spec] * 6,
        out_specs=spec,
        scratch_shapes=[pltpu.VMEM((SCAN_PAIRS, LANE, LANE), F32)],
        compiler_params=_cparams(("parallel", "parallel", "arbitrary")),
    )(r, k, v, lw, kk, a)


def _readout_kernel(yf_ref, yb_ref, r_ref, kf_ref, kb_ref, v_ref, sg_ref, g2_ref, rk_ref, lw_ref, lb_ref, bd_ref,
                    o_ref):
    bd = bd_ref[...]
    inv = 1.0 / RWKV_HEAD_DIM
    y = yf_ref[...] + yb_ref[...]
    mean = _head_sum(y, bd) * inv
    dlt = y - mean
    var = _head_sum(dlt * dlt, bd) * inv
    yn = dlt * lax.rsqrt(var + LNX_EPS) * lw_ref[...] + lb_ref[...]
    bonus = _head_sum(r_ref[...] * (kf_ref[...] + kb_ref[...]) * rk_ref[...], bd) * v_ref[...]
    g = jnp.dot(sg_ref[...], g2_ref[...], preferred_element_type=F32)
    o_ref[...] = ((yn + bonus) * g).astype(o_ref.dtype)


def _rwkv_readout(yf, yb, r, kf, kb, v, sg, g2, r_k, lnx_w, lnx_b, ones_bd):
    m, w = yf.shape
    tm = _pick(m, (272, 256, 128))
    row = lambda width: pl.BlockSpec((tm, width), lambda i: (i, 0))
    full = lambda shape: pl.BlockSpec(shape, lambda i: (0,) * len(shape))
    return pl.pallas_call(
        _readout_kernel,
        out_shape=jax.ShapeDtypeStruct((m, w), BF16),
        grid=(m // tm,),
        in_specs=[row(w)] * 6 + [row(GATE_PAD), full((GATE_PAD, w)), full((1, w)), full((1, w)), full((1, w)),
                                 full((LANE, LANE))],
        out_specs=row(w),
        compiler_params=_cparams(("parallel",)),
    )(yf, yb, r, kf, kb, v, sg, g2, r_k.reshape(1, w), lnx_w.reshape(1, w), lnx_b.reshape(1, w), ones_bd)


def _merge(gates, o_a, o_b, o_c, w_a, w_b, w_c):
    m = o_a.shape[0]
    d = w_a[0].shape[-1]
    tm = _pick(m, (1088, 512, 256))
    tn = 512

    def ep(p, ex, outs, pids):
        y = (_sigmoid(ex[0][...].astype(F32)) * p[0] + _sigmoid(ex[1][...].astype(F32)) * p[1]
             + _sigmoid(ex[2][...].astype(F32)) * p[2])
        outs[0][...] = y.astype(BF16)

    nbn = d // tn
    a_specs = [pl.BlockSpec((tm, o.shape[1]), lambda i, j, k: (i, 0)) for o in (o_a, o_b, o_c)]
    ws, b_specs = zip(*[_wspec(wt, wt[0].shape[-2], tn) for wt in (w_a, w_b, w_c)])
    g_specs = [pl.BlockSpec((tm, tn), functools.partial(lambda i, j, k, off: (i, off + j), off=br * nbn))
               for br in range(N_BRANCHES)]
    return _fmm((m // tm, nbn, 1), [o_a, o_b, o_c], a_specs, ws, b_specs, [gates] * 3, g_specs,
                [jax.ShapeDtypeStruct((m, d), BF16)], [pl.BlockSpec((tm, tn), lambda i, j, k: (i, j))],
                (tm, tn), 1, ep)[0]


def _mm_resid(a, w, x, gate, nb, tt, seq, tk, tn):
    m, kdim = a.shape
    tm = _pick(tt, (1088, 512, 256))
    tpb = tt // tm
    nk = kdim // tk
    w, wspec = _wspec(w, tk, tn)
    d = w.shape[-1]

    def ep(p, ex, outs, pids):
        ctx = _ctx_rows(pids[0], tm, tpb, seq)
        g = jnp.where(ctx, ex[2][...], ex[1][...])
        outs[0][...] = ex[0][...] + g * p[0]

    xspec = pl.BlockSpec((tm, tn), lambda i, j, k: (i, j))
    extra_specs = [xspec,
                   pl.BlockSpec((None, 1, tn), lambda i, j, k: (i // tpb, 0, j)),
                   pl.BlockSpec((None, 1, tn), lambda i, j, k: (nb, 0, j))]
    return _fmm((m // tm, d // tn, nk), [a], [pl.BlockSpec((tm, tk), lambda i, j, k: (i, k))],
                [w], [wspec], [x, gate, gate], extra_specs,
                [jax.ShapeDtypeStruct((m, d), F32)], [xspec], (tm, tn), nk, ep, aliases={2: 0})[0]


def _swiglu(u, w1, w3, tn, tk):
    m = u.shape[0]
    tm = _pick(m, (1088, 512, 256))

    def ep(p, ex, outs, pids):
        outs[0][...] = (_silu(p[0]) * p[1]).astype(BF16)

    return _flat_mm(u, None, BF16, tm, tn, tk, epilogue=ep, bs=[w1, w3])[0]


ROUTE_LANE = 8


def _router_kernel(u_ref, w_ref, o_ref):
    logits = _dot_x3(u_ref[...].astype(F32), w_ref[...])
    lane = lax.broadcasted_iota(jnp.int32, logits.shape, 1).astype(F32)
    valid = lane < N_EXPERTS
    neg = -1e30
    lg = jnp.where(valid, logits, neg)
    m1 = jnp.max(lg, axis=-1, keepdims=True)
    i1 = jnp.min(jnp.where(lg == m1, lane, float(LANE)), axis=-1, keepdims=True)
    lg2 = jnp.where(lane == i1, neg, lg)
    m2 = jnp.max(lg2, axis=-1, keepdims=True)
    i2 = jnp.min(jnp.where(lg2 == m2, lane, float(LANE)), axis=-1, keepdims=True)
    e2 = jnp.exp(m2 - m1)
    w1 = 1.0 / (1.0 + e2)
    w2 = e2 / (1.0 + e2)
    comb = jnp.where(lane == i1, w1, 0.0) + jnp.where(lane == i2, w2, 0.0)
    meta = (jnp.where(lane == ROUTE_LANE, i1, 0.0) + jnp.where(lane == ROUTE_LANE + 1, i2, 0.0)
            + jnp.where(lane == ROUTE_LANE + 2, w1, 0.0) + jnp.where(lane == ROUTE_LANE + 3, w2, 0.0))
    o_ref[...] = comb + meta


def _router(u, router_pad):
    m, d = u.shape
    tm = _pick(m, (544, 512, 256))
    return pl.pallas_call(
        _router_kernel,
        out_shape=jax.ShapeDtypeStruct((m, LANE), F32),
        grid=(m // tm,),
        in_specs=[pl.BlockSpec((tm, d), lambda i: (i, 0)), pl.BlockSpec((d, LANE), lambda i: (0, 0))],
        out_specs=pl.BlockSpec((tm, LANE), lambda i: (i, 0)),
        compiler_params=_cparams(("parallel",)),
    )(u, router_pad)


MOE_TILE = 512
GATHER_TILE = 256


def _route_plan(e1, e2, tm):
    n_tok = e1.shape[0]
    e_flat = jnp.stack([e1, e2], axis=1).reshape(-1)
    n_asg = e_flat.shape[0]
    onehot = (e_flat[:, None] == jnp.arange(N_EXPERTS, dtype=jnp.int32)[None, :]).astype(jnp.int32)
    rank = jnp.sum((jnp.cumsum(onehot, axis=0) - onehot) * onehot, axis=1)
    counts = jnp.sum(onehot, axis=0)
    padded = ((counts + tm - 1) // tm) * tm
    ends = jnp.cumsum(padded)
    slot = (ends - padded)[e_flat] + rank
    n_slots = n_asg + N_EXPERTS * tm
    src_tok = jnp.zeros((n_slots,), jnp.int32).at[slot].set(jnp.arange(n_asg, dtype=jnp.int32) // TOP_K)
    tile_start = jnp.arange(n_slots // tm, dtype=jnp.int32) * tm
    tile_e = jnp.sum((tile_start[:, None] >= ends[None, :]).astype(jnp.int32), axis=1)
    valid = (tile_start < ends[-1]).astype(jnp.int32)
    tile_e = jnp.where(valid == 1, tile_e, tile_e[jnp.maximum(ends[-1] // tm - 1, 0)])
    return src_tok, slot.reshape(n_tok, TOP_K), tile_e, valid


def _row_copy(src_hbm, row, buf, r, sem):
    return pltpu.make_async_copy(src_hbm.at[pl.ds(row, 1)], buf.at[pl.ds(r, 1)], sem)


def _gather_kernel(idx_ref, src_hbm, o_ref, buf, sem, *, tg, n_steps):
    i = pl.program_id(0)
    slot = i % 2

    def issue(step, s):
        def start(r, carry):
            _row_copy(src_hbm, idx_ref[step * tg + r], buf.at[s], r, sem.at[s]).start()
            return carry

        lax.fori_loop(0, tg, start, 0, unroll=8)

    @pl.when(i == 0)
    def _():
        issue(0, 0)

    @pl.when(i + 1 < n_steps)
    def _():
        issue(i + 1, 1 - slot)

    def wait(r, carry):
        _row_copy(src_hbm, 0, buf.at[slot], r, sem.at[slot]).wait()
        return carry

    lax.fori_loop(0, tg, wait, 0, unroll=8)
    o_ref[...] = buf[slot].astype(o_ref.dtype)


def _gather_rows(src, idx, out_dtype):
    n_slots = idx.shape[0]
    d = src.shape[1]
    tg = GATHER_TILE
    return pl.pallas_call(
        functools.partial(_gather_kernel, tg=tg, n_steps=n_slots // tg),
        out_shape=jax.ShapeDtypeStruct((n_slots, d), out_dtype),
        grid_spec=pltpu.PrefetchScalarGridSpec(
            num_scalar_prefetch=1,
            grid=(n_slots // tg,),
            in_specs=[pl.BlockSpec(memory_space=pl.ANY)],
            out_specs=pl.BlockSpec((tg, d), lambda i, idx_ref: (i, 0)),
            scratch_shapes=[pltpu.VMEM((2, tg, d), src.dtype), pltpu.SemaphoreType.DMA((2,))]),
        compiler_params=_cparams(("arbitrary",)),
    )(idx, src)


def _grouped_up_kernel(te_ref, tv_ref, a_ref, w1_ref, w3_ref, o_ref):
    valid = tv_ref[pl.program_id(1)] == 1

    @pl.when(valid)
    def _():
        a = a_ref[...]
        p1 = jnp.dot(a, w1_ref[...].astype(BF16), preferred_element_type=F32)
        p3 = jnp.dot(a, w3_ref[...].astype(BF16), preferred_element_type=F32)
        o_ref[...] = (_silu(p1) * p3).astype(o_ref.dtype)

    @pl.when(jnp.logical_not(valid))
    def _():
        o_ref[...] = jnp.zeros_like(o_ref)


def _grouped_down_kernel(te_ref, tv_ref, a_ref, w_ref, o_ref):
    valid = tv_ref[pl.program_id(1)] == 1

    @pl.when(valid)
    def _():
        o_ref[...] = jnp.dot(a_ref[...], w_ref[...].astype(BF16), preferred_element_type=F32)

    @pl.when(jnp.logical_not(valid))
    def _():
        o_ref[...] = jnp.zeros_like(o_ref)


def _grouped_mm(kern, a, ws, layer, tile_e, tile_valid, tn, out_dtype):
    n_slots, kdim = a.shape
    n = ws[0].shape[-1]
    tm = MOE_TILE
    wspec = pl.BlockSpec((None, None, kdim, tn), lambda j, i, te, tv: (layer, te[i], 0, j))
    return pl.pallas_call(
        kern,
        out_shape=jax.ShapeDtypeStruct((n_slots, n), out_dtype),
        grid_spec=pltpu.PrefetchScalarGridSpec(
            num_scalar_prefetch=2,
            grid=(n // tn, n_slots // tm),
            in_specs=[pl.BlockSpec((tm, kdim), lambda j, i, te, tv: (i, 0))] + [wspec] * len(ws),
            out_specs=pl.BlockSpec((tm, tn), lambda j, i, te, tv: (i, j))),
        compiler_params=_cparams(("parallel", "arbitrary")),
    )(tile_e, tile_valid, a, *ws)


def _combine_kernel(s1_ref, s2_ref, ys_hbm, x_ref, rw_ref, gate_ref, o_ref, buf1, buf2, sem, *, tc, spb, n_steps):
    i = pl.program_id(0) * spb + pl.program_id(1)
    slot = i % 2

    def issue(step, s):
        def start(r, carry):
            _row_copy(ys_hbm, s1_ref[step * tc + r], buf1.at[s], r, sem.at[s]).start()
            _row_copy(ys_hbm, s2_ref[step * tc + r], buf2.at[s], r, sem.at[s]).start()
            return carry

        lax.fori_loop(0, tc, start, 0, unroll=8)

    @pl.when(i == 0)
    def _():
        issue(0, 0)

    @pl.when(i + 1 < n_steps)
    def _():
        issue(i + 1, 1 - slot)

    def wait(r, carry):
        _row_copy(ys_hbm, 0, buf1.at[slot], r, sem.at[slot]).wait()
        _row_copy(ys_hbm, 0, buf2.at[slot], r, sem.at[slot]).wait()
        return carry

    lax.fori_loop(0, tc, wait, 0, unroll=8)
    rw = rw_ref[...]
    lane = lax.broadcasted_iota(jnp.int32, rw.shape, 1)
    w1 = jnp.sum(jnp.where(lane == ROUTE_LANE + 2, rw, 0.0), axis=-1, keepdims=True)
    w2 = jnp.sum(jnp.where(lane == ROUTE_LANE + 3, rw, 0.0), axis=-1, keepdims=True)
    o_ref[...] = x_ref[...] + gate_ref[...] * (w1 * buf1[slot] + w2 * buf2[slot])


def _moe_combine(x, ys, slots, route, gate, nb, tt, seq):
    m, d = x.shape
    tc = GATHER_TILE
    spb = seq // tc
    bpb = tt // tc
    row = lambda b, i, s1, s2: (b * bpb + i, 0)
    return pl.pallas_call(
        functools.partial(_combine_kernel, tc=tc, spb=spb, n_steps=nb * spb),
        out_shape=jax.ShapeDtypeStruct((m, d), F32),
        grid_spec=pltpu.PrefetchScalarGridSpec(
            num_scalar_prefetch=2,
            grid=(nb, spb),
            in_specs=[pl.BlockSpec(memory_space=pl.ANY),
                      pl.BlockSpec((tc, d), row),
                      pl.BlockSpec((tc, LANE), row),
                      pl.BlockSpec((None, 1, d), lambda b, i, s1, s2: (b, 0, 0))],
            out_specs=pl.BlockSpec((tc, d), row),
            scratch_shapes=[pltpu.VMEM((2, tc, d), F32), pltpu.VMEM((2, tc, d), F32),
                            pltpu.SemaphoreType.DMA((2,))]),
        input_output_aliases={3: 0},
        compiler_params=_cparams(("arbitrary", "arbitrary")),
    )(slots[:, 0], slots[:, 1], ys, x, route, gate)


def _moe_routed(x, u, route, moe_w1, moe_w3, moe_w2, layer, gate, nb, tt, seq):
    m, d = x.shape
    meta = route.reshape(nb, tt, LANE)[:, :seq].reshape(nb * seq, LANE)
    e1 = meta[:, ROUTE_LANE].astype(jnp.int32)
    e2 = meta[:, ROUTE_LANE + 1].astype(jnp.int32)
    src_tok, slots, tile_e, tile_valid = _route_plan(e1, e2, MOE_TILE)
    src_row = (src_tok // seq) * tt + src_tok % seq
    xs = _gather_rows(u, src_row, BF16)
    h = _grouped_mm(_grouped_up_kernel, xs, [moe_w1, moe_w3], layer, tile_e, tile_valid, 512, BF16)
    ys = _grouped_mm(_grouped_down_kernel, h, [moe_w2], layer, tile_e, tile_valid, 1024, F32)
    return _moe_combine(x, ys, slots, route, gate, nb, tt, seq)


def _rope_lane_tables(seq, nctx, nb):
    rows = seq // GRID_W
    row = jnp.repeat(jnp.arange(rows, dtype=F32), GRID_W)
    col = jnp.tile(jnp.arange(GRID_W, dtype=F32), rows)
    inv_freq = ROPE_BASE ** (-2.0 * jnp.arange(ROPE_PAIRS, dtype=F32) / ROPE_AXIS_DIM)
    ar = row[:, None] * inv_freq
    ac = col[:, None] * inv_freq
    ones = jnp.ones((seq, LANE - QK_ROPE), F32)
    zeros = jnp.zeros((seq, LANE - QK_ROPE), F32)
    cos = jnp.concatenate([jnp.cos(ar), jnp.cos(ar), jnp.cos(ac), jnp.cos(ac), ones], axis=1)
    sin = jnp.concatenate([-jnp.sin(ar), jnp.sin(ar), -jnp.sin(ac), jnp.sin(ac), zeros], axis=1)
    cos = jnp.concatenate([cos, jnp.ones((nctx, LANE), F32)], axis=0)
    sin = jnp.concatenate([sin, jnp.zeros((nctx, LANE), F32)], axis=0)
    return jnp.tile(cos, (nb, 1)), jnp.tile(sin, (nb, 1))


def _pad_cols(w, n):
    return jnp.pad(w, ((0, 0), (0, n - w.shape[1])))


def _pad_rows(w, n):
    return jnp.pad(w, ((0, n - w.shape[0]), (0, 0)))


def _z_layout(zsrc, vres_src, axis=-1):
    axis = axis % zsrc.ndim

    def zeros(n):
        shape = list(zsrc.shape)
        shape[axis] = n
        return jnp.zeros(shape, zsrc.dtype)

    parts = [lax.slice_in_dim(zsrc, 0, Z_MAIN, axis=axis), lax.slice_in_dim(zsrc, Z_MAIN, zsrc.shape[axis], axis=axis),
             zeros(GATE_PAD - GATE_LORA)]
    if vres_src is None:
        parts.append(zeros(VRES_PAD))
    else:
        parts += [vres_src, zeros(VRES_PAD - VRES_LORA)]
    return jnp.concatenate(parts, axis=axis)


def _pad_head(g):
    return jnp.pad(g, (0, HEAD_PAD - QK_HEAD)).reshape(1, HEAD_PAD)


def kernel(x, c, ctx, c_ctx, ada_w, ada_b, norm1, w_in, w_vres_down, q_norm, w_uq, kv_norm, w_ukv, q_gain, k_gain,
           rwkv_mu, vres_mu, w0, w2, a0, a2, k_k, k_a, v0, v2, r_k, lnx_w, lnx_b, g2, w_br_a, w_br_b, w_br_c,
           w_out, norm2, ffn_w1, ffn_w3, ffn_w2, router, moe_w1, moe_w3, moe_w2):
    nb, seq, d = x.shape
    nctx = ctx.shape[1]
    tt = seq + nctx
    m = nb * tt
    depth = ada_w.shape[0]
    xa = jnp.concatenate([x, ctx], axis=1).reshape(m, d)
    cond8 = jnp.concatenate([c, c_ctx[None], jnp.zeros((8 - nb - 1, d), F32)], axis=0)
    cos_t, sin_t = _rope_lane_tables(seq, nctx, nb)
    c_t, s_t = _dft_mats(seq, 1.0 / math.sqrt(seq))
    c_c, s_c = _dft_mats(FNET_GROUP_DIM, 1.0 / math.sqrt(FNET_GROUP_DIM))
    c_x, s_x = _dft_mats(nctx, 1.0 / math.sqrt(nctx))
    dft = (c_t, -s_t, c_c, s_c, c_x, s_x)
    li = jnp.arange(LANE)
    ones_bd = (li[:, None] // RWKV_HEAD_DIM == li[None, :] // RWKV_HEAD_DIM).astype(BF16)
    v_first = None
    gate_cols = N_BRANCHES * d
    w_in_t = jnp.swapaxes(w_in, 1, 2)
    for i in range(depth):
        last = i == depth - 1
        mod = _adaln(cond8, ada_w, i, ada_b[i])[:nb + 1].reshape(nb + 1, 6, 1, d)
        sh1, sc1, gm, sh2, sc2, gf = (mod[:, t] for t in range(6))
        wi = w_in_t[i]
        w_mla = _pad_rows(wi[gate_cols:gate_cols + MLA_IN], MLA_IN_PAD).astype(BF16)
        f0 = gate_cols + MLA_IN
        w_f = wi[f0:f0 + FNET_WIDTH].astype(BF16)
        zsrc = wi[f0 + FNET_WIDTH:]
        if i == 0:
            w_z = _z_layout(zsrc, None, 0).astype(BF16)
            mu_z = _z_layout(rwkv_mu[i], None)
        else:
            w_z = _z_layout(zsrc, jnp.swapaxes(w_vres_down[i - 1], 0, 1), 0).astype(BF16)
            mu_z = _z_layout(rwkv_mu[i], vres_mu[i - 1])

        u = _norm_mod(xa, norm1[i], sh1, sc1, nb, tt, seq)
        tm = _pick(m, (1088, 512, 256))
        gates = _flat_mm(u, (w_in_t, (i,)), BF16, tm, 512, d, n=gate_cols, wt=True)[0]
        mla_in = _flat_mm(u, w_mla, F32, _pick(m, (544, 512, 256)), MLA_IN_PAD, 2048, wt=True)[0]
        f = _flat_mm(u, w_f, BF16, tm, 512, d, wt=True)[0]
        z = _flat_mm(u, w_z, F32, _pick(m, (544, 512, 256)), Z_PAD // 3, d, wt=True)[0]

        w_uq_p = jnp.pad(w_uq[i].reshape(Q_LORA, MLA_HEADS, QK_HEAD),
                         ((0, 0), (0, 0), (0, HEAD_PAD - QK_HEAD))).reshape(Q_LORA, MLA_HEADS * HEAD_PAD)
        q, k, v = _mla_up(mla_in, q_norm[i], w_uq_p.astype(BF16), _pad_head(q_gain[i]), kv_norm[i],
                          w_ukv[i].astype(BF16), _pad_head(k_gain[i]), cos_t, sin_t)
        o_a = _attention(q, k, v, nb, tt, seq, not last)

        o_b = _fourier(f, nb, tt, seq, not last, dft)

        vres = None
        if i > 0:
            vres = (v_first, v0[i - 1], _pad_rows(v2[i - 1], VRES_PAD).astype(BF16))
        feat = _rwkv_features(z, nb, tt, seq, mu_z, w0[i], w2[i].astype(BF16), a0[i], a2[i].astype(BF16),
                              k_k[i], k_a[i], ones_bd, vres)
        r_, v_, kk_, kf_, kb_, lwf_, lwb_, af_, ab_, sg_ = feat[:10]
        if i == 0:
            v_first = feat[10]
        y_f = _rwkv_scan(r_, kf_, v_, lwf_, kk_, af_, False, seq)
        y_b = _rwkv_scan(r_, kb_, v_, lwb_, kk_, ab_, True, seq)
        flat = lambda t: t.reshape(m, t.shape[-1])
        o_c = _rwkv_readout(flat(y_f), flat(y_b), flat(r_), flat(kf_), flat(kb_), flat(v_), flat(sg_),
                            _pad_rows(g2[i], GATE_PAD).astype(BF16), r_k[i], lnx_w[i], lnx_b[i], ones_bd)

        mixed = _merge(gates, o_a, o_b, o_c, (w_br_a, (i,)), (w_br_b, (i,)), (w_br_c, (i,)))
        xa = _mm_resid(mixed, (w_out, (i,)), xa, gm, nb, tt, seq, d, 512)

        jj = i // 2
        u2 = _norm_mod(xa, norm2[i], sh2, sc2, nb, tt, seq, BF16 if i % 2 == 0 else F32)
        if i % 2 == 0:
            h = _swiglu(u2, _pad_cols(ffn_w1[jj], D_FF_PAD).astype(BF16),
                        _pad_cols(ffn_w3[jj], D_FF_PAD).astype(BF16), 512, d)
            xa = _mm_resid(h, _pad_rows(ffn_w2[jj], D_FF_PAD).astype(BF16), xa, gf, nb, tt, seq, D_FF_PAD // 4, 1024)
        else:
            route = _router(u2, _pad_cols(router[jj], LANE))
            xa = _moe_routed(xa, u2, route, moe_w1, moe_w3, moe_w2, jj, gf, nb, tt, seq)
    return xa.reshape(nb, tt, d)[:, :seq]
```

```python
import functools
import math

import jax
import jax.numpy as jnp
import numpy as np
from jax import lax
from jax.experimental import pallas as pl
from jax.experimental.pallas import tpu as pltpu

F32 = jnp.float32
BF16 = jnp.bfloat16

D_MODEL = 4096
GRID_W = 64
NORM_EPS = 1e-6
MLA_HEADS = 16
Q_LORA = 1024
KV_LORA = 512
QK_NOPE = 128
QK_ROPE = 64
QK_HEAD = QK_NOPE + QK_ROPE
V_HEAD = 128
HEAD_PAD = 256
Q_SCALE = QK_HEAD ** -0.5 * math.log2(math.e)
ROPE_AXIS_DIM = QK_ROPE // 2
ROPE_PAIRS = ROPE_AXIS_DIM // 2
ROPE_BASE = 10000.0
FNET_GROUPS = 4
FNET_GROUP_DIM = 256
FNET_WIDTH = FNET_GROUPS * FNET_GROUP_DIM
RWKV_HEADS = 16
RWKV_HEAD_DIM = 64
RWKV_WIDTH = RWKV_HEADS * RWKV_HEAD_DIM
DECAY_LORA = 128
ICLR_LORA = 128
VRES_LORA = 96
GATE_LORA = 480
GATE_PAD = 512
VRES_PAD = 128
LNX_EPS = 64e-5
N_BRANCHES = 3
MLA_IN = Q_LORA + KV_LORA + QK_ROPE
MLA_IN_PAD = Q_LORA + KV_LORA + 128
Z_MAIN = 3 * RWKV_WIDTH + 2 * DECAY_LORA + 2 * ICLR_LORA
Z_PAD = Z_MAIN + GATE_PAD + VRES_PAD
D_FF = 11008
D_FF_PAD = 11264
N_EXPERTS = 8
TOP_K = 2
D_FF_EXPERT = 3072

LANE = 128
CHUNK = 64
SCAN_BLOCK = 256
SCAN_GROUP = 1
SCAN_PAIRS = 4
INV_BASE = 16
VMEM_LIMIT = 52 * 1024 * 1024


def _cparams(sem):
    return pltpu.CompilerParams(dimension_semantics=sem, vmem_limit_bytes=VMEM_LIMIT)


def _pick(n, cands):
    for c in cands:
        if n % c == 0:
            return c
    raise ValueError(f"no tile for {n} in {cands}")


def _silu(x):
    return x * (1.0 / (1.0 + jnp.exp(-x)))


def _sigmoid(x):
    return 1.0 / (1.0 + jnp.exp(-x))


def _bdot(a, b):
    return jnp.dot(a.astype(BF16), b.astype(BF16), preferred_element_type=F32)


def _split2(x):
    hi = x.astype(BF16)
    lo = (x - hi.astype(F32)).astype(BF16)
    return hi, lo


def _split3(x):
    hi = x.astype(BF16)
    r1 = x - hi.astype(F32)
    mid = r1.astype(BF16)
    lo = (r1 - mid.astype(F32)).astype(BF16)
    return hi, mid, lo


def _dot_x3(a, b):
    ah, al = _split2(a)
    bh, bl = _split2(b)
    d = functools.partial(jnp.dot, preferred_element_type=F32)
    return d(ah, bh) + (d(ah, bl) + d(al, bh))


def _dot_exact_rhs(a, b_bf16):
    h, m, l = _split3(a)
    d = functools.partial(jnp.dot, preferred_element_type=F32)
    return d(h, b_bf16) + (d(m, b_bf16) + d(l, b_bf16))


def _fmm_kernel(*refs, n_a, n_prod, n_extra, n_out, nk, kaxis, epilogue, wt):
    a = refs[:n_a]
    b = refs[n_a:n_a + n_prod]
    ex = refs[n_a + n_prod:n_a + n_prod + n_extra]
    outs = refs[n_a + n_prod + n_extra:n_a + n_prod + n_extra + n_out]
    accs = refs[n_a + n_prod + n_extra + n_out:]
    pids = [pl.program_id(ax) for ax in range(kaxis + 1)]
    dims = (((1,), (1 if wt else 0,)), ((), ()))
    prods = [lax.dot_general(a[i % n_a][...], b[i][...].astype(BF16), dims, preferred_element_type=F32)
             for i in range(n_prod)]
    if nk == 1:
        epilogue(prods, ex, outs, pids)
        return
    k = pids[kaxis]

    @pl.when(k == 0)
    def _():
        for i in range(n_prod):
            accs[i][...] = prods[i]

    @pl.when(k > 0)
    def _():
        for i in range(n_prod):
            accs[i][...] += prods[i]

    @pl.when(k == nk - 1)
    def _():
        epilogue([acc[...] for acc in accs], ex, outs, pids)


def _fmm(grid, a_list, a_specs, b_list, b_specs, extras, extra_specs, out_shapes, out_specs,
         acc_shape, nk, epilogue, aliases=None, wt=False):
    n_prod = len(b_list)
    kern = functools.partial(_fmm_kernel, n_a=len(a_list), n_prod=n_prod, n_extra=len(extras), n_out=len(out_shapes),
                             nk=nk, kaxis=len(grid) - 1, epilogue=epilogue, wt=wt)
    scratch = [pltpu.VMEM(acc_shape, F32) for _ in range(n_prod)] if nk > 1 else []
    sem = ("parallel",) * (len(grid) - 1) + ("arbitrary",)
    return pl.pallas_call(
        kern,
        out_shape=out_shapes,
        grid=grid,
        in_specs=list(a_specs) + list(b_specs) + list(extra_specs),
        out_specs=out_specs,
        scratch_shapes=scratch,
        input_output_aliases=aliases or {},
        compiler_params=_cparams(sem),
    )(*a_list, *b_list, *extras)


def _wspec(w, tk, tn, wt=False):
    arr, lead = w if isinstance(w, tuple) else (w, ())
    if wt:
        return arr, pl.BlockSpec((None,) * len(lead) + (tn, tk), lambda i, j, k: tuple(lead) + (j, k))
    return arr, pl.BlockSpec((None,) * len(lead) + (tk, tn), lambda i, j, k: tuple(lead) + (k, j))


def _flat_mm(a, b, out_dtype, tm, tn, tk, epilogue=None, extras=(), extra_specs=(), bs=None,
             out_shapes=None, out_specs=None, aliases=None, n=None, wt=False):
    m, kdim = a.shape
    bs = bs if bs is not None else [b]
    bs, b_specs = zip(*[_wspec(w, tk, tn, wt) for w in bs])
    n = n if n is not None else bs[0].shape[-2 if wt else -1]
    nk = kdim // tk
    grid = (m // tm, n // tn, nk)
    a_specs = [pl.BlockSpec((tm, tk), lambda i, j, k: (i, k))]
    if epilogue is None:
        def epilogue(p, ex, outs, pids):
            outs[0][...] = p[0].astype(outs[0].dtype)
    if out_shapes is None:
        out_shapes = [jax.ShapeDtypeStruct((m, n), out_dtype)]
        out_specs = [pl.BlockSpec((tm, tn), lambda i, j, k: (i, j))]
    return _fmm(grid, [a], a_specs, bs, b_specs, list(extras), list(extra_specs), out_shapes,
                out_specs, (tm, tn), nk, epilogue, aliases, wt)


def _adaln_kernel(c_ref, w_ref, b_ref, o_ref):
    c = _silu(c_ref[...]).astype(BF16)
    o_ref[...] = jnp.dot(c, w_ref[...].astype(BF16), preferred_element_type=F32) + b_ref[...]


def _adaln(cond8, w_all, layer, bias):
    _, d, n = w_all.shape
    tn = 512
    return pl.pallas_call(
        _adaln_kernel,
        out_shape=jax.ShapeDtypeStruct((8, n), F32),
        grid=(n // tn,),
        in_specs=[pl.BlockSpec((8, d), lambda j: (0, 0)),
                  pl.BlockSpec((None, d, tn), lambda j: (layer, 0, j)),
                  pl.BlockSpec((1, tn), lambda j: (0, j))],
        out_specs=pl.BlockSpec((8, tn), lambda j: (0, j)),
        compiler_params=_cparams(("parallel",)),
    )(cond8, w_all, bias.reshape(1, n))


def _ctx_rows(tile_idx, tm, tiles_per_batch, seq):
    rows = (tile_idx % tiles_per_batch) * tm + lax.broadcasted_iota(jnp.int32, (tm, 1), 0)
    return rows >= seq


def _norm_mod_kernel(x_ref, g_ref, shl_ref, scl_ref, shc_ref, scc_ref, o_ref, *, tm, tpb, seq):
    x = x_ref[...]
    y = x * lax.rsqrt(jnp.mean(x * x, axis=-1, keepdims=True) + NORM_EPS) * g_ref[...]
    ctx = _ctx_rows(pl.program_id(0), tm, tpb, seq)
    sh = jnp.where(ctx, shc_ref[...], shl_ref[...])
    sc = jnp.where(ctx, scc_ref[...], scl_ref[...])
    o_ref[...] = (y * (1.0 + sc) + sh).astype(o_ref.dtype)


def _norm_mod(x, g, sh, sc, nb, tt, seq, out_dtype=BF16):
    m, d = x.shape
    tm = _pick(tt, (272, 256, 128))
    tpb = tt // tm
    lat = pl.BlockSpec((None, 1, d), lambda i: (i // tpb, 0, 0))
    ctx = pl.BlockSpec((None, 1, d), lambda i: (nb, 0, 0))
    return pl.pallas_call(
        functools.partial(_norm_mod_kernel, tm=tm, tpb=tpb, seq=seq),
        out_shape=jax.ShapeDtypeStruct((m, d), out_dtype),
        grid=(m // tm,),
        in_specs=[pl.BlockSpec((tm, d), lambda i: (i, 0)), pl.BlockSpec((1, d), lambda i: (0, 0)),
                  lat, lat, ctx, ctx],
        out_specs=pl.BlockSpec((tm, d), lambda i: (i, 0)),
        compiler_params=_cparams(("parallel",)),
    )(x, g.reshape(1, d), sh, sc, sh, sc)


def _rope128(x, cos, sin):
    lane = lax.broadcasted_iota(jnp.int32, x.shape, 1)
    first = (lane % (2 * ROPE_PAIRS)) < ROPE_PAIRS
    swapped = jnp.where(first, pltpu.roll(x, LANE - ROPE_PAIRS, 1), pltpu.roll(x, ROPE_PAIRS, 1))
    return x * cos + swapped * sin


def _head_norm_rope(x_lo, x_hi, gain_lo, gain_hi, cos, sin, scale):
    ss = jnp.sum(x_lo * x_lo, axis=-1, keepdims=True) + jnp.sum(x_hi * x_hi, axis=-1, keepdims=True)
    inv = lax.rsqrt(ss * (1.0 / QK_HEAD) + NORM_EPS)
    lo = x_lo * inv * gain_lo
    hi = _rope128(x_hi * inv * gain_hi, cos, sin)
    return lo * scale, hi * scale


def _mla_q_kernel(a_ref, g_ref, w_ref, gain_ref, cos_ref, sin_ref, q_ref, *, heads):
    a = a_ref[...]
    an = a * lax.rsqrt(jnp.mean(a * a, axis=-1, keepdims=True) + NORM_EPS) * g_ref[...]
    acc = jnp.dot(an.astype(BF16), w_ref[...], preferred_element_type=F32)
    cos, sin = cos_ref[...], sin_ref[...]
    gain = gain_ref[...]
    for h in range(heads):
        c0 = h * HEAD_PAD
        lo, hi = _head_norm_rope(acc[:, c0:c0 + LANE], acc[:, c0 + LANE:c0 + HEAD_PAD], gain[:, :LANE],
                                 gain[:, LANE:], cos, sin, Q_SCALE)
        q_ref[:, c0:c0 + LANE] = lo.astype(q_ref.dtype)
        q_ref[:, c0 + LANE:c0 + HEAD_PAD] = hi.astype(q_ref.dtype)


def _mla_kv_kernel(a_ref, g_ref, w_ref, kr_ref, gain_ref, cos_ref, sin_ref, k_ref, v_ref, *, heads):
    a = a_ref[...]
    an = a * lax.rsqrt(jnp.mean(a * a, axis=-1, keepdims=True) + NORM_EPS) * g_ref[...]
    acc = jnp.dot(an.astype(BF16), w_ref[...], preferred_element_type=F32)
    cos, sin = cos_ref[...], sin_ref[...]
    gain = gain_ref[...]
    kr = kr_ref[...]
    for h in range(heads):
        c0 = h * HEAD_PAD
        lo, hi = _head_norm_rope(acc[:, c0:c0 + LANE], kr, gain[:, :LANE], gain[:, LANE:], cos, sin, 1.0)
        k_ref[:, c0:c0 + LANE] = lo.astype(k_ref.dtype)
        k_ref[:, c0 + LANE:c0 + HEAD_PAD] = hi.astype(k_ref.dtype)
        v_ref[:, h * V_HEAD:(h + 1) * V_HEAD] = acc[:, c0 + LANE:c0 + HEAD_PAD].astype(v_ref.dtype)


def _mla_up(mla_in, q_norm, w_uq, q_gain, kv_norm, w_ukv, k_gain, cos, sin):
    m = mla_in.shape[0]
    tm = _pick(m, (544, 512, 256))
    hpt = 4
    tn = hpt * HEAD_PAD
    grid = (m // tm, MLA_HEADS // hpt)
    row = lambda w: pl.BlockSpec((tm, w), lambda i, j: (i, 0))
    q = pl.pallas_call(
        functools.partial(_mla_q_kernel, heads=hpt),
        out_shape=jax.ShapeDtypeStruct((m, MLA_HEADS * HEAD_PAD), BF16),
        grid=grid,
        in_specs=[pl.BlockSpec((tm, Q_LORA), lambda i, j: (i, 0)),
                  pl.BlockSpec((1, Q_LORA), lambda i, j: (0, 0)),
                  pl.BlockSpec((Q_LORA, tn), lambda i, j: (0, j)),
                  pl.BlockSpec((1, HEAD_PAD), lambda i, j: (0, 0)),
                  row(LANE), row(LANE)],
        out_specs=pl.BlockSpec((tm, tn), lambda i, j: (i, j)),
        compiler_params=_cparams(("parallel", "parallel")),
    )(mla_in, q_norm.reshape(1, -1), w_uq, q_gain, cos, sin)
    k, v = pl.pallas_call(
        functools.partial(_mla_kv_kernel, heads=hpt),
        out_shape=[jax.ShapeDtypeStruct((m, MLA_HEADS * HEAD_PAD), BF16),
                   jax.ShapeDtypeStruct((m, MLA_HEADS * V_HEAD), BF16)],
        grid=grid,
        in_specs=[pl.BlockSpec((tm, KV_LORA), lambda i, j: (i, Q_LORA // KV_LORA)),
                  pl.BlockSpec((1, KV_LORA), lambda i, j: (0, 0)),
                  pl.BlockSpec((KV_LORA, tn), lambda i, j: (0, j)),
                  pl.BlockSpec((tm, LANE), lambda i, j: (i, (Q_LORA + KV_LORA) // LANE)),
                  pl.BlockSpec((1, HEAD_PAD), lambda i, j: (0, 0)),
                  row(LANE), row(LANE)],
        out_specs=[pl.BlockSpec((tm, tn), lambda i, j: (i, j)),
                   pl.BlockSpec((tm, hpt * V_HEAD), lambda i, j: (i, j))],
        compiler_params=_cparams(("parallel", "parallel")),
    )(mla_in, kv_norm.reshape(1, -1), w_ukv, mla_in, k_gain, cos, sin)
    return q, k, v


ATTN_ROWS = 256


def _softmax_pv(q, k, v, o_ref):
    nt = (((1,), (1,)), ((), ()))
    rows = min(ATTN_ROWS, q.shape[0])
    parts = [slice(h * rows, (h + 1) * rows) for h in range(q.shape[0] // rows)]
    scores = [lax.dot_general(q[rs], k, nt, preferred_element_type=F32) for rs in parts]
    v_ext = jnp.concatenate([v, jnp.ones_like(v)], axis=1)
    for rs, s in zip(parts, scores):
        p = jnp.exp2(s - jnp.max(s, axis=-1, keepdims=True))
        o = jnp.dot(p.astype(BF16), v_ext, preferred_element_type=F32)
        o_ref[rs, :] = (o[:, :V_HEAD] * (1.0 / o[:, V_HEAD:])).astype(o_ref.dtype)


def _attn_kernel(q_ref, k_ref, v_ref, o_ref):
    _softmax_pv(q_ref[...], k_ref[...], v_ref[...], o_ref)


def _attention(q, k, v, nb, tt, seq, with_ctx):
    q3 = q.reshape(nb, tt, MLA_HEADS * HEAD_PAD)
    k3 = k.reshape(nb, tt, MLA_HEADS * HEAD_PAD)
    v3 = v.reshape(nb, tt, MLA_HEADS * V_HEAD)
    tq = _pick(seq, (1024, 512, 256))
    o_lat = pl.pallas_call(
        _attn_kernel,
        out_shape=jax.ShapeDtypeStruct((nb, seq, MLA_HEADS * V_HEAD), BF16),
        grid=(nb, MLA_HEADS, seq // tq),
        in_specs=[pl.BlockSpec((None, tq, HEAD_PAD), lambda b, h, i: (b, i, h)),
                  pl.BlockSpec((None, tt, HEAD_PAD), lambda b, h, i: (b, 0, h)),
                  pl.BlockSpec((None, tt, V_HEAD), lambda b, h, i: (b, 0, h))],
        out_specs=pl.BlockSpec((None, tq, V_HEAD), lambda b, h, i: (b, i, h)),
        compiler_params=_cparams(("parallel", "parallel", "arbitrary")),
    )(q3, k3, v3)
    nctx = tt - seq
    if with_ctx:
        cb = seq // nctx
        o_ctx = pl.pallas_call(
            _attn_kernel,
            out_shape=jax.ShapeDtypeStruct((nb, nctx, MLA_HEADS * V_HEAD), BF16),
            grid=(nb, MLA_HEADS),
            in_specs=[pl.BlockSpec((None, nctx, HEAD_PAD), lambda b, h: (b, cb, h)),
                      pl.BlockSpec((None, nctx, HEAD_PAD), lambda b, h: (b, cb, h)),
                      pl.BlockSpec((None, nctx, V_HEAD), lambda b, h: (b, cb, h))],
            out_specs=pl.BlockSpec((None, nctx, V_HEAD), lambda b, h: (b, 0, h)),
            compiler_params=_cparams(("parallel", "parallel")),
        )(q3, k3, v3)
    else:
        o_ctx = jnp.zeros((nb, nctx, MLA_HEADS * V_HEAD), BF16)
    return jnp.concatenate([o_lat, o_ctx], axis=1).reshape(nb * tt, MLA_HEADS * V_HEAD)


def _dft_angles(n, cols):
    rows = jnp.arange(n, dtype=jnp.int32)
    ang = ((rows[:, None] * cols[None, :]) % n).astype(F32) * (2.0 * math.pi / n)
    return jnp.cos(ang), jnp.sin(ang)


def _dft_mats(n, scale):
    if n % GRID_W or n <= GRID_W:
        c, s = _dft_angles(n, jnp.arange(n, dtype=jnp.int32))
    else:
        ca, sa = _dft_angles(n, jnp.arange(n // GRID_W, dtype=jnp.int32) * GRID_W)
        cb, sb = _dft_angles(n, jnp.arange(GRID_W, dtype=jnp.int32))
        c = (ca[:, :, None] * cb[:, None, :] - sa[:, :, None] * sb[:, None, :]).reshape(n, n)
        s = (sa[:, :, None] * cb[:, None, :] + ca[:, :, None] * sb[:, None, :]).reshape(n, n)
    return (c * scale).astype(BF16), (s * scale).astype(BF16)


def _chan_dft_kernel(f_ref, c_ref, s_ref, zc_ref, zs_ref):
    for g in range(FNET_GROUPS):
        cols = slice(g * FNET_GROUP_DIM, (g + 1) * FNET_GROUP_DIM)
        fz = f_ref[:, cols]
        zc_ref[:, cols] = jnp.dot(fz, c_ref[...], preferred_element_type=F32).astype(zc_ref.dtype)
        zs_ref[:, cols] = jnp.dot(fz, s_ref[...], preferred_element_type=F32).astype(zs_ref.dtype)


def _fourier(f, nb, tt, seq, with_ctx, dft):
    c_t, sneg_t, c_c, s_c, c_x, s_x = dft
    m = f.shape[0]
    tmz = _pick(m, (1088, 512, 256))
    row = pl.BlockSpec((tmz, FNET_WIDTH), lambda i: (i, 0))
    mat = pl.BlockSpec((FNET_GROUP_DIM, FNET_GROUP_DIM), lambda i: (0, 0))
    shp = jax.ShapeDtypeStruct((m, FNET_WIDTH), BF16)
    zc, zs = pl.pallas_call(
        _chan_dft_kernel,
        out_shape=[shp, shp],
        grid=(m // tmz,),
        in_specs=[row, mat, mat],
        out_specs=[row, row],
        compiler_params=_cparams(("parallel",)),
    )(f, c_c, s_c)
    zc = zc.reshape(nb, tt, FNET_WIDTH)
    zs = zs.reshape(nb, tt, FNET_WIDTH)

    def ep_sum(p, ex, outs, pids):
        outs[0][...] = (p[0] + p[1]).astype(BF16)

    tm = _pick(seq, (1024, 512, 256))
    tk = tm
    tn = FNET_WIDTH
    nk = seq // tk
    aspec = pl.BlockSpec((tm, tk), lambda b, i, j, k: (i, k))
    bspec = pl.BlockSpec((None, tk, tn), lambda b, i, j, k: (b, k, j))
    y_lat = _fmm((nb, seq // tm, FNET_WIDTH // tn, nk), [c_t, sneg_t], [aspec, aspec], [zc, zs], [bspec, bspec],
                 [], [], [jax.ShapeDtypeStruct((nb, seq, FNET_WIDTH), BF16)],
                 [pl.BlockSpec((None, tm, tn), lambda b, i, j, k: (b, i, j))], (tm, tn), nk, ep_sum)[0]
    nctx = tt - seq
    if with_ctx:
        def ep_diff(p, ex, outs, pids):
            outs[0][...] = (p[0] - p[1]).astype(BF16)

        cb = seq // nctx
        aspec = pl.BlockSpec((nctx, nctx), lambda b, k: (0, 0))
        bspec = pl.BlockSpec((None, nctx, FNET_WIDTH), lambda b, k: (b, cb, 0))
        y_ctx = _fmm((nb, 1), [c_x, s_x], [aspec, aspec], [zc, zs], [bspec, bspec], [], [],
                     [jax.ShapeDtypeStruct((nb, nctx, FNET_WIDTH), BF16)],
                     [pl.BlockSpec((None, nctx, FNET_WIDTH), lambda b, k: (b, 0, 0))],
                     (nctx, FNET_WIDTH), 1, ep_diff)[0]
    else:
        y_ctx = jnp.zeros((nb, nctx, FNET_WIDTH), BF16)
    return jnp.concatenate([y_lat, y_ctx], axis=1).reshape(nb * tt, FNET_WIDTH)


def _head_sum(x, ones_bd):
    parts = []
    for s in range(x.shape[1] // LANE):
        hi, lo = _split2(x[:, s * LANE:(s + 1) * LANE])
        parts.append(jnp.dot(hi, ones_bd, preferred_element_type=F32)
                     + jnp.dot(lo, ones_bd, preferred_element_type=F32))
    return jnp.concatenate(parts, axis=-1)


def _feat_kernel(*refs, tf, tpb, lat_tiles, has_vres):
    (z_ref, zp_ref, zn_ref, mu_ref, w0_ref, w2_ref, a0_ref, a2_ref, kk_ref_p, ka_ref, bd_ref) = refs[:11]
    pos = 11
    if has_vres:
        vf_ref, v0_ref, v2_ref = refs[pos:pos + 3]
        pos += 3
    outs = refs[pos:]
    r_o, v_o, kk_o, kf_o, kb_o, lwf_o, lwb_o, af_o, ab_o, sg_o = outs[:10]
    j = pl.program_id(1)
    has_prev = jnp.logical_and(j != 0, j != lat_tiles)
    has_next = jnp.logical_and(j != lat_tiles - 1, j != tpb - 1)
    row = lax.broadcasted_iota(jnp.int32, (tf, 1), 0)

    def zf(lo, hi):
        z = z_ref[:, lo:hi]
        prev_row = jnp.where(has_prev, zp_ref[7:8, lo:hi], 0.0)
        next_row = jnp.where(has_next, zn_ref[0:1, lo:hi], 0.0)
        zp = jnp.where(row == 0, prev_row, pltpu.roll(z, 1, 0))
        zn = jnp.where(row == tf - 1, next_row, pltpu.roll(z, tf - 1, 0))
        return z + mu_ref[:, lo:hi] * (0.5 * (zp + zn) - z)

    w = RWKV_WIDTH
    r = zf(0, w)
    k = zf(w, 2 * w)
    v = zf(2 * w, 3 * w)
    r_o[...] = r
    if has_vres:
        vl = zf(Z_MAIN + GATE_PAD, Z_PAD)
        gate = _sigmoid(v0_ref[...] + _bdot(vl, v2_ref[...]))
        v_o[...] = v + (vf_ref[...] - v) * gate
    else:
        v_o[...] = v
        outs[10][...] = v
    kk = k * kk_ref_p[...]
    ss = _head_sum(kk * kk, bd_ref[...])
    kk_o[...] = kk * lax.rsqrt(ss + 1e-12)
    base = 3 * w
    for d, (lw_o, a_o, k_o) in enumerate(((lwf_o, af_o, kf_o), (lwb_o, ab_o, kb_o))):
        wd = zf(base + d * DECAY_LORA, base + (d + 1) * DECAY_LORA)
        ad = zf(base + 2 * DECAY_LORA + d * ICLR_LORA, base + 2 * DECAY_LORA + (d + 1) * ICLR_LORA)
        xw = w0_ref[d:d + 1, :] + _bdot(jnp.tanh(wd), w2_ref[d])
        lw_o[...] = -_sigmoid(xw) * math.exp(-0.5)
        a = _sigmoid(a0_ref[d:d + 1, :] + _bdot(ad, a2_ref[d]))
        a_o[...] = a
        k_o[...] = k * (1.0 + (a - 1.0) * ka_ref[...])
    sg_o[...] = _sigmoid(zf(Z_MAIN, Z_MAIN + GATE_PAD)).astype(sg_o.dtype)


def _rwkv_features(z, nb, tt, seq, mu, w0, w2, a0, a2, k_k, k_a, ones_bd, vres):
    tf = 128
    tpb = tt // tf
    lat_tiles = seq // tf
    z3 = z.reshape(nb, tt, Z_PAD)
    hb = tf // 8
    last8 = tt // 8 - 1
    w = RWKV_WIDTH
    full = lambda shape: pl.BlockSpec(shape, lambda b, j: (0,) * len(shape))
    tile = lambda width: pl.BlockSpec((None, tf, width), lambda b, j: (b, j, 0))
    in_specs = [tile(Z_PAD),
                pl.BlockSpec((None, 8, Z_PAD), lambda b, j: (b, jnp.maximum(j * hb - 1, 0), 0)),
                pl.BlockSpec((None, 8, Z_PAD), lambda b, j: (b, jnp.minimum((j + 1) * hb, last8), 0)),
                full((1, Z_PAD)), full((2, w)), full((2, DECAY_LORA, w)), full((2, w)), full((2, ICLR_LORA, w)),
                full((1, w)), full((1, w)), full((LANE, LANE))]
    args = [z3, z3, z3, mu.reshape(1, Z_PAD), w0, w2, a0, a2, k_k.reshape(1, w), k_a.reshape(1, w), ones_bd]
    has_vres = vres is not None
    if has_vres:
        v_first, v0, v2 = vres
        in_specs += [tile(w), full((1, w)), full((VRES_PAD, w))]
        args += [v_first.reshape(nb, tt, w), v0.reshape(1, w), v2]
    f3 = jax.ShapeDtypeStruct((nb, tt, w), F32)
    out_shape = [f3] * 9 + [jax.ShapeDtypeStruct((nb, tt, GATE_PAD), BF16)]
    out_specs = [tile(w)] * 9 + [tile(GATE_PAD)]
    if not has_vres:
        out_shape.append(f3)
        out_specs.append(tile(w))
    return pl.pallas_call(
        functools.partial(_feat_kernel, tf=tf, tpb=tpb, lat_tiles=lat_tiles, has_vres=has_vres),
        out_shape=out_shape,
        grid=(nb, tpb),
        in_specs=in_specs,
        out_specs=out_specs,
        compiler_params=_cparams(("parallel", "arbitrary")),
    )(*args)


def _stack2(x, lo_mask):
    return jnp.concatenate([jnp.where(lo_mask, x, 0.0), jnp.where(lo_mask, 0.0, x)], axis=0)


def _dot16(a, b):
    return jnp.dot(a.astype(BF16), b.astype(BF16), preferred_element_type=F32)


def _dg16(a, b, dims):
    return lax.dot_general(a.astype(BF16), b.astype(BF16), dims, preferred_element_type=F32)


def _bmm(a, b):
    return lax.dot_general(a.astype(BF16), b.astype(BF16), (((2,), (1,)), ((0,), (0,))),
                           preferred_element_type=F32)


def _bmm_nt(a, b):
    return lax.dot_general(a.astype(BF16), b.astype(BF16), (((2,), (2,)), ((0,), (0,))),
                           preferred_element_type=F32)


_GROUP_OPERANDS = ("kkt", "rt", "bh", "kh", "kb", "bb", "v")


def _group_operands(r, k, v, lw, kk, a, rev, consts, g):
    tri_cum, lo_mask = consts[0], consts[3]
    c = CHUNK
    s2 = lambda x: _stack2(x, lo_mask)
    cols = {name: [] for name in _GROUP_OPERANDS}
    tots = []
    for ch in range(g):
        sl = slice(ch * c, (ch + 1) * c)
        lw_c = lw[sl]
        cum = _dot_exact_rhs_left(tri_cum, lw_c)
        tot = cum[0:1, :] if rev else cum[c - 1:c, :]
        g_inv = jnp.exp(-cum)
        g_tail = jnp.exp(tot - cum)
        b_c = kk[sl] * a[sl]
        cols["kkt"].append(s2(kk[sl] * jnp.exp(cum - lw_c)))
        cols["rt"].append(s2(r[sl] * jnp.exp(cum)))
        cols["bh"].append(s2(b_c * g_inv))
        cols["kh"].append(s2(k[sl] * g_inv))
        cols["kb"].append(s2(k[sl] * g_tail))
        cols["bb"].append(s2(b_c * g_tail))
        cols["v"].append(s2(v[sl]))
        tots.append(tot)
    return {name: jnp.concatenate(cols[name], axis=0) for name in _GROUP_OPERANDS}, tots


def _solve_groups(ops, consts):
    _, strict, incl, _, eye_big, _, base_mask, level_masks = consts
    rt = ops["rt"]
    kkt16, rt16, bh16, kh16, v16 = (ops[nm].astype(BF16) for nm in ("kkt", "rt", "bh", "kh", "v"))
    nr = rt.shape[1]
    lhs = jnp.concatenate([kkt16, rt16], axis=1)
    big_b = _bmm_nt(lhs, bh16)
    big_k = _bmm_nt(lhs, kh16)
    m_k = jnp.where(strict, big_k[:, :nr], 0.0)
    a_qb = jnp.where(incl, big_b[:, nr:], 0.0)
    a_qk = jnp.where(incl, big_k[:, nr:], 0.0)
    m_b = jnp.where(strict, big_b[:, :nr], 0.0)
    n = jnp.where(base_mask, -m_b, 0.0)
    tinv = eye_big + n
    for _ in range(int(math.log2(INV_BASE)) - 1):
        n = _bmm(n, n)
        tinv = tinv + _bmm(tinv, n)
    for off_mask in level_masks:
        tinv = tinv - _bmm(_bmm(tinv, jnp.where(off_mask, m_b, 0.0)), tinv)
    mkv = _bmm(m_k, v16)
    wu16 = _bmm(tinv, jnp.concatenate([kkt16, mkv.astype(BF16)], axis=2)).astype(BF16)
    qy = _bmm(a_qb, wu16)
    q16 = (rt - qy[:, :, :LANE]).astype(BF16)
    y0 = _bmm(a_qk, v16) - qy[:, :, LANE:]
    return wu16, q16, y0


def _dot_exact_rhs_left(tri_bf16, x):
    h, m, l = _split3(x)
    d = functools.partial(jnp.dot, preferred_element_type=F32)
    return d(tri_bf16, h) + (d(tri_bf16, m) + d(tri_bf16, l))


def _scan_kernel(r_ref, k_ref, v_ref, lw_ref, kk_ref, a_ref, y_ref, h_ref, *, rev, n_groups, g):
    @pl.when(pl.program_id(2) == 0)
    def _():
        h_ref[...] = jnp.zeros_like(h_ref)

    c = CHUNK
    ri = lax.broadcasted_iota(jnp.int32, (c, c), 0)
    ci = lax.broadcasted_iota(jnp.int32, (c, c), 1)
    tri_cum = (ci >= ri if rev else ci <= ri).astype(BF16)
    nr = g * 2 * c
    r2 = lax.broadcasted_iota(jnp.int32, (nr, nr), 0)
    c2 = lax.broadcasted_iota(jnp.int32, (nr, nr), 1)
    same = (r2 // (2 * c)) == (c2 // (2 * c))
    strict = jnp.logical_and(same, c2 > r2 if rev else c2 < r2)
    incl = jnp.logical_and(same, c2 >= r2 if rev else c2 <= r2)
    eye_big = (r2 == c2).astype(F32)
    eye = eye_big[:LANE, :LANE]
    lo_mask = lax.broadcasted_iota(jnp.int32, (c, LANE), 1) < RWKV_HEAD_DIM
    base_mask = (r2 // INV_BASE) == (c2 // INV_BASE)
    level_masks = []
    s = 2 * INV_BASE
    while s <= c:
        level_masks.append(jnp.logical_and((r2 // s) == (c2 // s), (r2 // (s // 2)) != (c2 // (s // 2))))
        s *= 2
    consts = (tri_cum, strict, incl, lo_mask, eye_big, eye, base_mask, level_masks)
    gl = g * c
    rows = 2 * c
    units = [(pi, gi) for pi in range(SCAN_PAIRS) for gi in range(n_groups)]
    ops_list, tots_list = [], []
    for pi, gi in units:
        sl = slice(gi * gl, (gi + 1) * gl)
        ls = slice(pi * LANE, (pi + 1) * LANE)
        ops, tots = _group_operands(r_ref[sl, ls], k_ref[sl, ls], v_ref[sl, ls], lw_ref[sl, ls], kk_ref[sl, ls],
                                    a_ref[sl, ls], rev, consts, g)
        ops_list.append(ops)
        tots_list.append(tots)
    stacked = {name: jnp.stack([ops[name] for ops in ops_list]) for name in _GROUP_OPERANDS}
    wu16, q16, y0 = _solve_groups(stacked, consts)
    kb16 = stacked["kb"].astype(BF16)
    bb16 = stacked["bb"].astype(BF16)
    v16 = stacked["v"].astype(BF16)
    tn = (((0,), (0,)), ((), ()))
    for pi in range(SCAN_PAIRS):
        h = h_ref[pi]
        for gi in (range(n_groups - 1, -1, -1) if rev else range(n_groups)):
            u = units.index((pi, gi))
            for ch in (range(g - 1, -1, -1) if rev else range(g)):
                rs = slice(ch * rows, (ch + 1) * rows)
                pg = _dg16(bb16[u, rs], wu16[u, rs], tn)
                p_mat = eye * jnp.exp(tots_list[u][ch]) - pg[:, :LANE]
                g_mat = _dg16(kb16[u, rs], v16[u, rs], tn) - pg[:, LANE:]
                h16 = h.astype(BF16)
                y_st = _dot16(q16[u, rs], h16) + y0[u, rs]
                h = _dot16(p_mat, h16) + g_mat
                t0 = gi * gl + ch * c
                y_ref[t0:t0 + c, pi * LANE:(pi + 1) * LANE] = y_st[:c] + y_st[c:]
        h_ref[pi] = h


def _rwkv_scan(r, k, v, lw, kk, a, rev, seq):
    nb, tt, w = r.shape
    nblk = tt // SCAN_BLOCK
    lat_blk = seq // SCAN_BLOCK
    if rev:
        def blk(j):
            return nblk - 1 - j
    else:
        def blk(j):
            return jnp.where(j < nblk - lat_blk, lat_blk + j, j - (nblk - lat_blk))
    spec = pl.BlockSpec((None, SCAN_BLOCK, SCAN_PAIRS * LANE), lambda b, p, j: (b, blk(j), p))
    return pl.pallas_call(
        functools.partial(_scan_kernel, rev=rev, n_groups=SCAN_BLOCK // (CHUNK * SCAN_GROUP), g=SCAN_GROUP),
        out_shape=jax.ShapeDtypeStruct((nb, tt, w), F32),
        grid=(nb, w // (SCAN_PAIRS * LANE), nblk),
        in_specs=[spec] * 6,
        out_specs=spec,
        scratch_shapes=[pltpu.VMEM((SCAN_PAIRS, LANE, LANE), F32)],
        compiler_params=_cparams(("parallel", "parallel", "arbitrary")),
    )(r, k, v, lw, kk, a)


def _readout_kernel(yf_ref, yb_ref, r_ref, kf_ref, kb_ref, v_ref, sg_ref, g2_ref, rk_ref, lw_ref, lb_ref, bd_ref,
                    o_ref):
    bd = bd_ref[...]
    inv = 1.0 / RWKV_HEAD_DIM
    y = yf_ref[...] + yb_ref[...]
    mean = _head_sum(y, bd) * inv
    dlt = y - mean
    var = _head_sum(dlt * dlt, bd) * inv
    yn = dlt * lax.rsqrt(var + LNX_EPS) * lw_ref[...] + lb_ref[...]
    bonus = _head_sum(r_ref[...] * (kf_ref[...] + kb_ref[...]) * rk_ref[...], bd) * v_ref[...]
    g = jnp.dot(sg_ref[...], g2_ref[...], preferred_element_type=F32)
    o_ref[...] = ((yn + bonus) * g).astype(o_ref.dtype)


def _rwkv_readout(yf, yb, r, kf, kb, v, sg, g2, r_k, lnx_w, lnx_b, ones_bd):
    m, w = yf.shape
    tm = _pick(m, (272, 256, 128))
    row = lambda width: pl.BlockSpec((tm, width), lambda i: (i, 0))
    full = lambda shape: pl.BlockSpec(shape, lambda i: (0,) * len(shape))
    return pl.pallas_call(
        _readout_kernel,
        out_shape=jax.ShapeDtypeStruct((m, w), BF16),
        grid=(m // tm,),
        in_specs=[row(w)] * 6 + [row(GATE_PAD), full((GATE_PAD, w)), full((1, w)), full((1, w)), full((1, w)),
                                 full((LANE, LANE))],
        out_specs=row(w),
        compiler_params=_cparams(("parallel",)),
    )(yf, yb, r, kf, kb, v, sg, g2, r_k.reshape(1, w), lnx_w.reshape(1, w), lnx_b.reshape(1, w), ones_bd)


def _merge(gates, o_a, o_b, o_c, w_a, w_b, w_c):
    m = o_a.shape[0]
    d = w_a[0].shape[-1]
    tm = _pick(m, (1088, 512, 256))
    tn = 512

    def ep(p, ex, outs, pids):
        y = (_sigmoid(ex[0][...].astype(F32)) * p[0] + _sigmoid(ex[1][...].astype(F32)) * p[1]
             + _sigmoid(ex[2][...].astype(F32)) * p[2])
        outs[0][...] = y.astype(BF16)

    nbn = d // tn
    a_specs = [pl.BlockSpec((tm, o.shape[1]), lambda i, j, k: (i, 0)) for o in (o_a, o_b, o_c)]
    ws, b_specs = zip(*[_wspec(wt, wt[0].shape[-2], tn) for wt in (w_a, w_b, w_c)])
    g_specs = [pl.BlockSpec((tm, tn), functools.partial(lambda i, j, k, off: (i, off + j), off=br * nbn))
               for br in range(N_BRANCHES)]
    return _fmm((m // tm, nbn, 1), [o_a, o_b, o_c], a_specs, ws, b_specs, [gates] * 3, g_specs,
                [jax.ShapeDtypeStruct((m, d), BF16)], [pl.BlockSpec((tm, tn), lambda i, j, k: (i, j))],
                (tm, tn), 1, ep)[0]


def _mm_resid(a, w, x, gate, nb, tt, seq, tk, tn):
    m, kdim = a.shape
    tm = _pick(tt, (1088, 512, 256))
    tpb = tt // tm
    nk = kdim // tk
    w, wspec = _wspec(w, tk, tn)
    d = w.shape[-1]

    def ep(p, ex, outs, pids):
        ctx = _ctx_rows(pids[0], tm, tpb, seq)
        g = jnp.where(ctx, ex[2][...], ex[1][...])
        outs[0][...] = ex[0][...] + g * p[0]

    xspec = pl.BlockSpec((tm, tn), lambda i, j, k: (i, j))
    extra_specs = [xspec,
                   pl.BlockSpec((None, 1, tn), lambda i, j, k: (i // tpb, 0, j)),
                   pl.BlockSpec((None, 1, tn), lambda i, j, k: (nb, 0, j))]
    return _fmm((m // tm, d // tn, nk), [a], [pl.BlockSpec((tm, tk), lambda i, j, k: (i, k))],
                [w], [wspec], [x, gate, gate], extra_specs,
                [jax.ShapeDtypeStruct((m, d), F32)], [xspec], (tm, tn), nk, ep, aliases={2: 0})[0]


def _swiglu(u, w1, w3, tn, tk):
    m = u.shape[0]
    tm = _pick(m, (1088, 512, 256))

    def ep(p, ex, outs, pids):
        outs[0][...] = (_silu(p[0]) * p[1]).astype(BF16)

    return _flat_mm(u, None, BF16, tm, tn, tk, epilogue=ep, bs=[w1, w3])[0]


ROUTE_LANE = 8


def _router_kernel(u_ref, w_ref, o_ref):
    logits = _dot_x3(u_ref[...].astype(F32), w_ref[...])
    lane = lax.broadcasted_iota(jnp.int32, logits.shape, 1).astype(F32)
    valid = lane < N_EXPERTS
    neg = -1e30
    lg = jnp.where(valid, logits, neg)
    m1 = jnp.max(lg, axis=-1, keepdims=True)
    i1 = jnp.min(jnp.where(lg == m1, lane, float(LANE)), axis=-1, keepdims=True)
    lg2 = jnp.where(lane == i1, neg, lg)
    m2 = jnp.max(lg2, axis=-1, keepdims=True)
    i2 = jnp.min(jnp.where(lg2 == m2, lane, float(LANE)), axis=-1, keepdims=True)
    e2 = jnp.exp(m2 - m1)
    w1 = 1.0 / (1.0 + e2)
    w2 = e2 / (1.0 + e2)
    comb = jnp.where(lane == i1, w1, 0.0) + jnp.where(lane == i2, w2, 0.0)
    meta = (jnp.where(lane == ROUTE_LANE, i1, 0.0) + jnp.where(lane == ROUTE_LANE + 1, i2, 0.0)
            + jnp.where(lane == ROUTE_LANE + 2, w1, 0.0) + jnp.where(lane == ROUTE_LANE + 3, w2, 0.0))
    o_ref[...] = comb + meta


def _router(u, router_pad):
    m, d = u.shape
    tm = _pick(m, (544, 512, 256))
    return pl.pallas_call(
        _router_kernel,
        out_shape=jax.ShapeDtypeStruct((m, LANE), F32),
        grid=(m // tm,),
        in_specs=[pl.BlockSpec((tm, d), lambda i: (i, 0)), pl.BlockSpec((d, LANE), lambda i: (0, 0))],
        out_specs=pl.BlockSpec((tm, LANE), lambda i: (i, 0)),
        compiler_params=_cparams(("parallel",)),
    )(u, router_pad)


MOE_TILE = 512
GATHER_TILE = 256


def _route_plan(e1, e2, tm):
    n_tok = e1.shape[0]
    e_flat = jnp.stack([e1, e2], axis=1).reshape(-1)
    n_asg = e_flat.shape[0]
    onehot = (e_flat[:, None] == jnp.arange(N_EXPERTS, dtype=jnp.int32)[None, :]).astype(jnp.int32)
    rank = jnp.sum((jnp.cumsum(onehot, axis=0) - onehot) * onehot, axis=1)
    counts = jnp.sum(onehot, axis=0)
    padded = ((counts + tm - 1) // tm) * tm
    ends = jnp.cumsum(padded)
    slot = (ends - padded)[e_flat] + rank
    n_slots = n_asg + N_EXPERTS * tm
    src_tok = jnp.zeros((n_slots,), jnp.int32).at[slot].set(jnp.arange(n_asg, dtype=jnp.int32) // TOP_K)
    tile_start = jnp.arange(n_slots // tm, dtype=jnp.int32) * tm
    tile_e = jnp.sum((tile_start[:, None] >= ends[None, :]).astype(jnp.int32), axis=1)
    valid = (tile_start < ends[-1]).astype(jnp.int32)
    tile_e = jnp.where(valid == 1, tile_e, tile_e[jnp.maximum(ends[-1] // tm - 1, 0)])
    return src_tok, slot.reshape(n_tok, TOP_K), tile_e, valid


def _row_copy(src_hbm, row, buf, r, sem):
    return pltpu.make_async_copy(src_hbm.at[pl.ds(row, 1)], buf.at[pl.ds(r, 1)], sem)


def _gather_kernel(idx_ref, src_hbm, o_ref, buf, sem, *, tg, n_steps):
    i = pl.program_id(0)
    slot = i % 2

    def issue(step, s):
        def start(r, carry):
            _row_copy(src_hbm, idx_ref[step * tg + r], buf.at[s], r, sem.at[s]).start()
            return carry

        lax.fori_loop(0, tg, start, 0, unroll=8)

    @pl.when(i == 0)
    def _():
        issue(0, 0)

    @pl.when(i + 1 < n_steps)
    def _():
        issue(i + 1, 1 - slot)

    def wait(r, carry):
        _row_copy(src_hbm, 0, buf.at[slot], r, sem.at[slot]).wait()
        return carry

    lax.fori_loop(0, tg, wait, 0, unroll=8)
    o_ref[...] = buf[slot].astype(o_ref.dtype)


def _gather_rows(src, idx, out_dtype):
    n_slots = idx.shape[0]
    d = src.shape[1]
    tg = GATHER_TILE
    return pl.pallas_call(
        functools.partial(_gather_kernel, tg=tg, n_steps=n_slots // tg),
        out_shape=jax.ShapeDtypeStruct((n_slots, d), out_dtype),
        grid_spec=pltpu.PrefetchScalarGridSpec(
            num_scalar_prefetch=1,
            grid=(n_slots // tg,),
            in_specs=[pl.BlockSpec(memory_space=pl.ANY)],
            out_specs=pl.BlockSpec((tg, d), lambda i, idx_ref: (i, 0)),
            scratch_shapes=[pltpu.VMEM((2, tg, d), src.dtype), pltpu.SemaphoreType.DMA((2,))]),
        compiler_params=_cparams(("arbitrary",)),
    )(idx, src)


def _grouped_up_kernel(te_ref, tv_ref, a_ref, w1_ref, w3_ref, o_ref):
    valid = tv_ref[pl.program_id(1)] == 1

    @pl.when(valid)
    def _():
        a = a_ref[...]
        p1 = jnp.dot(a, w1_ref[...].astype(BF16), preferred_element_type=F32)
        p3 = jnp.dot(a, w3_ref[...].astype(BF16), preferred_element_type=F32)
        o_ref[...] = (_silu(p1) * p3).astype(o_ref.dtype)

    @pl.when(jnp.logical_not(valid))
    def _():
        o_ref[...] = jnp.zeros_like(o_ref)


def _grouped_down_kernel(te_ref, tv_ref, a_ref, w_ref, o_ref):
    valid = tv_ref[pl.program_id(1)] == 1

    @pl.when(valid)
    def _():
        o_ref[...] = jnp.dot(a_ref[...], w_ref[...].astype(BF16), preferred_element_type=F32)

    @pl.when(jnp.logical_not(valid))
    def _():
        o_ref[...] = jnp.zeros_like(o_ref)


def _grouped_mm(kern, a, ws, layer, tile_e, tile_valid, tn, out_dtype):
    n_slots, kdim = a.shape
    n = ws[0].shape[-1]
    tm = MOE_TILE
    wspec = pl.BlockSpec((None, None, kdim, tn), lambda j, i, te, tv: (layer, te[i], 0, j))
    return pl.pallas_call(
        kern,
        out_shape=jax.ShapeDtypeStruct((n_slots, n), out_dtype),
        grid_spec=pltpu.PrefetchScalarGridSpec(
            num_scalar_prefetch=2,
            grid=(n // tn, n_slots // tm),
            in_specs=[pl.BlockSpec((tm, kdim), lambda j, i, te, tv: (i, 0))] + [wspec] * len(ws),
            out_specs=pl.BlockSpec((tm, tn), lambda j, i, te, tv: (i, j))),
        compiler_params=_cparams(("parallel", "arbitrary")),
    )(tile_e, tile_valid, a, *ws)


def _combine_kernel(s1_ref, s2_ref, ys_hbm, x_ref, rw_ref, gate_ref, o_ref, buf1, buf2, sem, *, tc, spb, n_steps):
    i = pl.program_id(0) * spb + pl.program_id(1)
    slot = i % 2

    def issue(step, s):
        def start(r, carry):
            _row_copy(ys_hbm, s1_ref[step * tc + r], buf1.at[s], r, sem.at[s]).start()
            _row_copy(ys_hbm, s2_ref[step * tc + r], buf2.at[s], r, sem.at[s]).start()
            return carry

        lax.fori_loop(0, tc, start, 0, unroll=8)

    @pl.when(i == 0)
    def _():
        issue(0, 0)

    @pl.when(i + 1 < n_steps)
    def _():
        issue(i + 1, 1 - slot)

    def wait(r, carry):
        _row_copy(ys_hbm, 0, buf1.at[slot], r, sem.at[slot]).wait()
        _row_copy(ys_hbm, 0, buf2.at[slot], r, sem.at[slot]).wait()
        return carry

    lax.fori_loop(0, tc, wait, 0, unroll=8)
    rw = rw_ref[...]
    lane = lax.broadcasted_iota(jnp.int32, rw.shape, 1)
    w1 = jnp.sum(jnp.where(lane == ROUTE_LANE + 2, rw, 0.0), axis=-1, keepdims=True)
    w2 = jnp.sum(jnp.where(lane == ROUTE_LANE + 3, rw, 0.0), axis=-1, keepdims=True)
    o_ref[...] = x_ref[...] + gate_ref[...] * (w1 * buf1[slot] + w2 * buf2[slot])


def _moe_combine(x, ys, slots, route, gate, nb, tt, seq, latent_only):
    m, d = x.shape
    tc = GATHER_TILE
    spb = seq // tc
    bpb = tt // tc
    row = lambda b, i, s1, s2: (b * bpb + i, 0)
    out_row = (lambda b, i, s1, s2: (b * spb + i, 0)) if latent_only else row
    return pl.pallas_call(
        functools.partial(_combine_kernel, tc=tc, spb=spb, n_steps=nb * spb),
        out_shape=jax.ShapeDtypeStruct((nb * seq if latent_only else m, d), F32),
        grid_spec=pltpu.PrefetchScalarGridSpec(
            num_scalar_prefetch=2,
            grid=(nb, spb),
            in_specs=[pl.BlockSpec(memory_space=pl.ANY),
                      pl.BlockSpec((tc, d), row),
                      pl.BlockSpec((tc, LANE), row),
                      pl.BlockSpec((None, 1, d), lambda b, i, s1, s2: (b, 0, 0))],
            out_specs=pl.BlockSpec((tc, d), out_row),
            scratch_shapes=[pltpu.VMEM((2, tc, d), F32), pltpu.VMEM((2, tc, d), F32),
                            pltpu.SemaphoreType.DMA((2,))]),
        input_output_aliases={} if latent_only else {3: 0},
        compiler_params=_cparams(("arbitrary", "arbitrary")),
    )(slots[:, 0], slots[:, 1], ys, x, route, gate)


def _moe_routed(x, u, route, moe_w1, moe_w3, moe_w2, layer, gate, nb, tt, seq, latent_only):
    m, d = x.shape
    meta = route.reshape(nb, tt, LANE)[:, :seq].reshape(nb * seq, LANE)
    e1 = meta[:, ROUTE_LANE].astype(jnp.int32)
    e2 = meta[:, ROUTE_LANE + 1].astype(jnp.int32)
    src_tok, slots, tile_e, tile_valid = _route_plan(e1, e2, MOE_TILE)
    src_row = (src_tok // seq) * tt + src_tok % seq
    xs = _gather_rows(u, src_row, BF16)
    h = _grouped_mm(_grouped_up_kernel, xs, [moe_w1, moe_w3], layer, tile_e, tile_valid, 512, BF16)
    ys = _grouped_mm(_grouped_down_kernel, h, [moe_w2], layer, tile_e, tile_valid, 1024, F32)
    return _moe_combine(x, ys, slots, route, gate, nb, tt, seq, latent_only)


def _rope_lane_tables(seq, nctx, nb):
    rows = seq // GRID_W
    row = jnp.repeat(jnp.arange(rows, dtype=F32), GRID_W)
    col = jnp.tile(jnp.arange(GRID_W, dtype=F32), rows)
    inv_freq = ROPE_BASE ** (-2.0 * jnp.arange(ROPE_PAIRS, dtype=F32) / ROPE_AXIS_DIM)
    ar = row[:, None] * inv_freq
    ac = col[:, None] * inv_freq
    ones = jnp.ones((seq, LANE - QK_ROPE), F32)
    zeros = jnp.zeros((seq, LANE - QK_ROPE), F32)
    cos = jnp.concatenate([jnp.cos(ar), jnp.cos(ar), jnp.cos(ac), jnp.cos(ac), ones], axis=1)
    sin = jnp.concatenate([-jnp.sin(ar), jnp.sin(ar), -jnp.sin(ac), jnp.sin(ac), zeros], axis=1)
    cos = jnp.concatenate([cos, jnp.ones((nctx, LANE), F32)], axis=0)
    sin = jnp.concatenate([sin, jnp.zeros((nctx, LANE), F32)], axis=0)
    return jnp.tile(cos, (nb, 1)), jnp.tile(sin, (nb, 1))


def _pad_cols(w, n):
    return jnp.concatenate([w, jnp.zeros((w.shape[0], n - w.shape[1]), w.dtype)], axis=1)


def _pad_rows(w, n):
    return jnp.concatenate([w, jnp.zeros((n - w.shape[0], w.shape[1]), w.dtype)], axis=0)


def _z_layout(zsrc, vres_src, axis=-1):
    axis = axis % zsrc.ndim

    def zeros(n):
        shape = list(zsrc.shape)
        shape[axis] = n
        return jnp.zeros(shape, zsrc.dtype)

    parts = [lax.slice_in_dim(zsrc, 0, Z_MAIN, axis=axis), lax.slice_in_dim(zsrc, Z_MAIN, zsrc.shape[axis], axis=axis),
             zeros(GATE_PAD - GATE_LORA)]
    if vres_src is None:
        parts.append(zeros(VRES_PAD))
    else:
        parts += [vres_src, zeros(VRES_PAD - VRES_LORA)]
    return jnp.concatenate(parts, axis=axis)


def _pad_head(g):
    return jnp.pad(g, (0, HEAD_PAD - QK_HEAD)).reshape(1, HEAD_PAD)


def kernel(x, c, ctx, c_ctx, ada_w, ada_b, norm1, w_in, w_vres_down, q_norm, w_uq, kv_norm, w_ukv, q_gain, k_gain,
           rwkv_mu, vres_mu, w0, w2, a0, a2, k_k, k_a, v0, v2, r_k, lnx_w, lnx_b, g2, w_br_a, w_br_b, w_br_c,
           w_out, norm2, ffn_w1, ffn_w3, ffn_w2, router, moe_w1, moe_w3, moe_w2):
    nb, seq, d = x.shape
    nctx = ctx.shape[1]
    tt = seq + nctx
    m = nb * tt
    depth = ada_w.shape[0]
    xa = jnp.concatenate([x, ctx], axis=1).reshape(m, d)
    cond8 = jnp.concatenate([c, c_ctx[None], jnp.zeros((8 - nb - 1, d), F32)], axis=0)
    cos_t, sin_t = _rope_lane_tables(seq, nctx, nb)
    c_t, s_t = _dft_mats(seq, 1.0 / math.sqrt(seq))
    c_c, s_c = _dft_mats(FNET_GROUP_DIM, 1.0 / math.sqrt(FNET_GROUP_DIM))
    c_x, s_x = _dft_mats(nctx, 1.0 / math.sqrt(nctx))
    dft = (c_t, -s_t, c_c, s_c, c_x, s_x)
    li = jnp.arange(LANE)
    ones_bd = (li[:, None] // RWKV_HEAD_DIM == li[None, :] // RWKV_HEAD_DIM).astype(BF16)
    v_first = None
    gate_cols = N_BRANCHES * d
    w_in_t = jnp.swapaxes(w_in, 1, 2)
    for i in range(depth):
        last = i == depth - 1
        mod = _adaln(cond8, ada_w, i, ada_b[i])[:nb + 1].reshape(nb + 1, 6, 1, d)
        sh1, sc1, gm, sh2, sc2, gf = (mod[:, t] for t in range(6))
        wi = w_in_t[i]
        w_mla = _pad_rows(wi[gate_cols:gate_cols + MLA_IN], MLA_IN_PAD).astype(BF16)
        f0 = gate_cols + MLA_IN
        w_f = wi[f0:f0 + FNET_WIDTH].astype(BF16)
        zsrc = wi[f0 + FNET_WIDTH:]
        if i == 0:
            w_z = _z_layout(zsrc, None, 0).astype(BF16)
            mu_z = _z_layout(rwkv_mu[i], None)
        else:
            w_z = _z_layout(zsrc, jnp.swapaxes(w_vres_down[i - 1], 0, 1), 0).astype(BF16)
            mu_z = _z_layout(rwkv_mu[i], vres_mu[i - 1])

        u = _norm_mod(xa, norm1[i], sh1, sc1, nb, tt, seq)
        tm = _pick(m, (1088, 512, 256))
        gates = _flat_mm(u, (w_in_t, (i,)), BF16, tm, 512, d, n=gate_cols, wt=True)[0]
        mla_in = _flat_mm(u, w_mla, F32, _pick(m, (544, 512, 256)), MLA_IN_PAD, 2048, wt=True)[0]
        f = _flat_mm(u, w_f, BF16, tm, 512, d, wt=True)[0]
        z = _flat_mm(u, w_z, F32, _pick(m, (544, 512, 256)), Z_PAD // 3, d, wt=True)[0]

        w_uq_p = jnp.pad(w_uq[i].reshape(Q_LORA, MLA_HEADS, QK_HEAD),
                         ((0, 0), (0, 0), (0, HEAD_PAD - QK_HEAD))).reshape(Q_LORA, MLA_HEADS * HEAD_PAD)
        q, k, v = _mla_up(mla_in, q_norm[i], w_uq_p.astype(BF16), _pad_head(q_gain[i]), kv_norm[i],
                          w_ukv[i].astype(BF16), _pad_head(k_gain[i]), cos_t, sin_t)
        o_a = _attention(q, k, v, nb, tt, seq, not last)

        o_b = _fourier(f, nb, tt, seq, not last, dft)

        vres = None
        if i > 0:
            vres = (v_first, v0[i - 1], _pad_rows(v2[i - 1], VRES_PAD).astype(BF16))
        feat = _rwkv_features(z, nb, tt, seq, mu_z, w0[i], w2[i].astype(BF16), a0[i], a2[i].astype(BF16),
                              k_k[i], k_a[i], ones_bd, vres)
        r_, v_, kk_, kf_, kb_, lwf_, lwb_, af_, ab_, sg_ = feat[:10]
        if i == 0:
            v_first = feat[10]
        y_f = _rwkv_scan(r_, kf_, v_, lwf_, kk_, af_, False, seq)
        y_b = _rwkv_scan(r_, kb_, v_, lwb_, kk_, ab_, True, seq)
        flat = lambda t: t.reshape(m, t.shape[-1])
        o_c = _rwkv_readout(flat(y_f), flat(y_b), flat(r_), flat(kf_), flat(kb_), flat(v_), flat(sg_),
                            _pad_rows(g2[i], GATE_PAD).astype(BF16), r_k[i], lnx_w[i], lnx_b[i], ones_bd)

        mixed = _merge(gates, o_a, o_b, o_c, (w_br_a, (i,)), (w_br_b, (i,)), (w_br_c, (i,)))
        xa = _mm_resid(mixed, (w_out, (i,)), xa, gm, nb, tt, seq, d, 512)

        jj = i // 2
        u2 = _norm_mod(xa, norm2[i], sh2, sc2, nb, tt, seq, BF16 if i % 2 == 0 else F32)
        if i % 2 == 0:
            h = _swiglu(u2, _pad_cols(ffn_w1[jj].astype(BF16), D_FF_PAD),
                        _pad_cols(ffn_w3[jj].astype(BF16), D_FF_PAD), 512, d)
            xa = _mm_resid(h, _pad_rows(ffn_w2[jj].astype(BF16), D_FF_PAD), xa, gf, nb, tt, seq, D_FF_PAD // 4, 1024)
        else:
            route = _router(u2, _pad_cols(router[jj], LANE))
            xa = _moe_routed(xa, u2, route, moe_w1, moe_w3, moe_w2, jj, gf, nb, tt, seq, last)
            if last:
                return xa.reshape(nb, seq, d)
    return xa.reshape(nb, tt, d)[:, :seq]
```

```python
import functools
import math

import jax
import jax.numpy as jnp
import numpy as np
from jax import lax
from jax.experimental import pallas as pl
from jax.experimental.pallas import tpu as pltpu

F32 = jnp.float32
BF16 = jnp.bfloat16

D_MODEL = 4096
GRID_W = 64
NORM_EPS = 1e-6
MLA_HEADS = 16
Q_LORA = 1024
KV_LORA = 512
QK_NOPE = 128
QK_ROPE = 64
QK_HEAD = QK_NOPE + QK_ROPE
V_HEAD = 128
HEAD_PAD = 256
Q_SCALE = QK_HEAD ** -0.5 * math.log2(math.e)
ROPE_AXIS_DIM = QK_ROPE // 2
ROPE_PAIRS = ROPE_AXIS_DIM // 2
ROPE_BASE = 10000.0
FNET_GROUPS = 4
FNET_GROUP_DIM = 256
FNET_WIDTH = FNET_GROUPS * FNET_GROUP_DIM
RWKV_HEADS = 16
RWKV_HEAD_DIM = 64
RWKV_WIDTH = RWKV_HEADS * RWKV_HEAD_DIM
DECAY_LORA = 128
ICLR_LORA = 128
VRES_LORA = 96
GATE_LORA = 480
GATE_PAD = 512
VRES_PAD = 128
LNX_EPS = 64e-5
N_BRANCHES = 3
MLA_IN = Q_LORA + KV_LORA + QK_ROPE
MLA_IN_PAD = Q_LORA + KV_LORA + 128
Z_MAIN = 3 * RWKV_WIDTH + 2 * DECAY_LORA + 2 * ICLR_LORA
Z_PAD = Z_MAIN + GATE_PAD + VRES_PAD
D_FF = 11008
D_FF_PAD = 11264
N_EXPERTS = 8
TOP_K = 2
D_FF_EXPERT = 3072

LANE = 128
CHUNK = 64
SCAN_BLOCK = 256
SCAN_GROUP = 1
SCAN_PAIRS = 4
INV_BASE = 16
VMEM_LIMIT = 52 * 1024 * 1024


def _cparams(sem):
    return pltpu.CompilerParams(dimension_semantics=sem, vmem_limit_bytes=VMEM_LIMIT)


def _pick(n, cands):
    for c in cands:
        if n % c == 0:
            return c
    raise ValueError(f"no tile for {n} in {cands}")


def _silu(x):
    return x * (1.0 / (1.0 + jnp.exp(-x)))


def _sigmoid(x):
    return 1.0 / (1.0 + jnp.exp(-x))


def _bdot(a, b):
    return jnp.dot(a.astype(BF16), b.astype(BF16), preferred_element_type=F32)


def _split2(x):
    hi = x.astype(BF16)
    lo = (x - hi.astype(F32)).astype(BF16)
    return hi, lo


def _split3(x):
    hi = x.astype(BF16)
    r1 = x - hi.astype(F32)
    mid = r1.astype(BF16)
    lo = (r1 - mid.astype(F32)).astype(BF16)
    return hi, mid, lo


def _dot_x3(a, b):
    ah, al = _split2(a)
    bh, bl = _split2(b)
    d = functools.partial(jnp.dot, preferred_element_type=F32)
    return d(ah, bh) + (d(ah, bl) + d(al, bh))


def _dot_exact_rhs(a, b_bf16):
    h, m, l = _split3(a)
    d = functools.partial(jnp.dot, preferred_element_type=F32)
    return d(h, b_bf16) + (d(m, b_bf16) + d(l, b_bf16))


def _fmm_kernel(*refs, n_a, n_prod, n_extra, n_out, nk, kaxis, epilogue, wt):
    a = refs[:n_a]
    b = refs[n_a:n_a + n_prod]
    ex = refs[n_a + n_prod:n_a + n_prod + n_extra]
    outs = refs[n_a + n_prod + n_extra:n_a + n_prod + n_extra + n_out]
    accs = refs[n_a + n_prod + n_extra + n_out:]
    pids = [pl.program_id(ax) for ax in range(kaxis + 1)]
    dims = (((1,), (1 if wt else 0,)), ((), ()))
    prods = [lax.dot_general(a[i % n_a][...], b[i][...].astype(BF16), dims, preferred_element_type=F32)
             for i in range(n_prod)]
    if nk == 1:
        epilogue(prods, ex, outs, pids)
        return
    k = pids[kaxis]

    @pl.when(k == 0)
    def _():
        for i in range(n_prod):
            accs[i][...] = prods[i]

    @pl.when(k > 0)
    def _():
        for i in range(n_prod):
            accs[i][...] += prods[i]

    @pl.when(k == nk - 1)
    def _():
        epilogue([acc[...] for acc in accs], ex, outs, pids)


def _fmm(grid, a_list, a_specs, b_list, b_specs, extras, extra_specs, out_shapes, out_specs,
         acc_shape, nk, epilogue, aliases=None, wt=False):
    n_prod = len(b_list)
    kern = functools.partial(_fmm_kernel, n_a=len(a_list), n_prod=n_prod, n_extra=len(extras), n_out=len(out_shapes),
                             nk=nk, kaxis=len(grid) - 1, epilogue=epilogue, wt=wt)
    scratch = [pltpu.VMEM(acc_shape, F32) for _ in range(n_prod)] if nk > 1 else []
    sem = ("parallel",) * (len(grid) - 1) + ("arbitrary",)
    return pl.pallas_call(
        kern,
        out_shape=out_shapes,
        grid=grid,
        in_specs=list(a_specs) + list(b_specs) + list(extra_specs),
        out_specs=out_specs,
        scratch_shapes=scratch,
        input_output_aliases=aliases or {},
        compiler_params=_cparams(sem),
    )(*a_list, *b_list, *extras)


def _wspec(w, tk, tn, wt=False):
    arr, lead = w if isinstance(w, tuple) else (w, ())
    if wt:
        return arr, pl.BlockSpec((None,) * len(lead) + (tn, tk), lambda i, j, k: tuple(lead) + (j, k))
    return arr, pl.BlockSpec((None,) * len(lead) + (tk, tn), lambda i, j, k: tuple(lead) + (k, j))


def _flat_mm(a, b, out_dtype, tm, tn, tk, epilogue=None, extras=(), extra_specs=(), bs=None,
             out_shapes=None, out_specs=None, aliases=None, n=None, wt=False):
    m, kdim = a.shape
    bs = bs if bs is not None else [b]
    bs, b_specs = zip(*[_wspec(w, tk, tn, wt) for w in bs])
    n = n if n is not None else bs[0].shape[-2 if wt else -1]
    nk = kdim // tk
    grid = (m // tm, n // tn, nk)
    a_specs = [pl.BlockSpec((tm, tk), lambda i, j, k: (i, k))]
    if epilogue is None:
        def epilogue(p, ex, outs, pids):
            outs[0][...] = p[0].astype(outs[0].dtype)
    if out_shapes is None:
        out_shapes = [jax.ShapeDtypeStruct((m, n), out_dtype)]
        out_specs = [pl.BlockSpec((tm, tn), lambda i, j, k: (i, j))]
    return _fmm(grid, [a], a_specs, bs, b_specs, list(extras), list(extra_specs), out_shapes,
                out_specs, (tm, tn), nk, epilogue, aliases, wt)


def _adaln_kernel(c_ref, w_ref, b_ref, o_ref):
    c = _silu(c_ref[...]).astype(BF16)
    o_ref[...] = jnp.dot(c, w_ref[...].astype(BF16), preferred_element_type=F32) + b_ref[...]


def _adaln(cond8, w_all, layer, bias):
    _, d, n = w_all.shape
    tn = 512
    return pl.pallas_call(
        _adaln_kernel,
        out_shape=jax.ShapeDtypeStruct((8, n), F32),
        grid=(n // tn,),
        in_specs=[pl.BlockSpec((8, d), lambda j: (0, 0)),
                  pl.BlockSpec((None, d, tn), lambda j: (layer, 0, j)),
                  pl.BlockSpec((1, tn), lambda j: (0, j))],
        out_specs=pl.BlockSpec((8, tn), lambda j: (0, j)),
        compiler_params=_cparams(("parallel",)),
    )(cond8, w_all, bias.reshape(1, n))


def _ctx_rows(tile_idx, tm, tiles_per_batch, seq):
    rows = (tile_idx % tiles_per_batch) * tm + lax.broadcasted_iota(jnp.int32, (tm, 1), 0)
    return rows >= seq


def _norm_mod_kernel(x_ref, g_ref, shl_ref, scl_ref, shc_ref, scc_ref, o_ref, *, tm, tpb, seq):
    x = x_ref[...]
    y = x * lax.rsqrt(jnp.mean(x * x, axis=-1, keepdims=True) + NORM_EPS) * g_ref[...]
    ctx = _ctx_rows(pl.program_id(0), tm, tpb, seq)
    sh = jnp.where(ctx, shc_ref[...], shl_ref[...])
    sc = jnp.where(ctx, scc_ref[...], scl_ref[...])
    o_ref[...] = (y * (1.0 + sc) + sh).astype(o_ref.dtype)


def _norm_mod(x, g, sh, sc, nb, tt, seq, out_dtype=BF16):
    m, d = x.shape
    tm = _pick(tt, (544, 272, 256, 128) if out_dtype == BF16 else (272, 256, 128))
    tpb = tt // tm
    lat = pl.BlockSpec((None, 1, d), lambda i: (i // tpb, 0, 0))
    ctx = pl.BlockSpec((None, 1, d), lambda i: (nb, 0, 0))
    return pl.pallas_call(
        functools.partial(_norm_mod_kernel, tm=tm, tpb=tpb, seq=seq),
        out_shape=jax.ShapeDtypeStruct((m, d), out_dtype),
        grid=(m // tm,),
        in_specs=[pl.BlockSpec((tm, d), lambda i: (i, 0)), pl.BlockSpec((1, d), lambda i: (0, 0)),
                  lat, lat, ctx, ctx],
        out_specs=pl.BlockSpec((tm, d), lambda i: (i, 0)),
        compiler_params=_cparams(("parallel",)),
    )(x, g.reshape(1, d), sh, sc, sh, sc)


def _rope128(x, cos, sin):
    lane = lax.broadcasted_iota(jnp.int32, x.shape, 1)
    first = (lane % (2 * ROPE_PAIRS)) < ROPE_PAIRS
    swapped = jnp.where(first, pltpu.roll(x, LANE - ROPE_PAIRS, 1), pltpu.roll(x, ROPE_PAIRS, 1))
    return x * cos + swapped * sin


def _head_norm_rope(x_lo, x_hi, gain_lo, gain_hi, cos, sin, scale):
    ss = jnp.sum(x_lo * x_lo, axis=-1, keepdims=True) + jnp.sum(x_hi * x_hi, axis=-1, keepdims=True)
    inv = lax.rsqrt(ss * (1.0 / QK_HEAD) + NORM_EPS)
    lo = x_lo * inv * gain_lo
    hi = _rope128(x_hi * inv * gain_hi, cos, sin)
    return lo * scale, hi * scale


def _normed_products(a_ref, g_ref, w_ref):
    half = a_ref.shape[0] // 2
    parts = [slice(0, half), slice(half, 2 * half)]
    accs = []
    for rs in parts:
        a = a_ref[rs, :]
        an = a * lax.rsqrt(jnp.mean(a * a, axis=-1, keepdims=True) + NORM_EPS) * g_ref[...]
        accs.append(jnp.dot(an.astype(BF16), w_ref[...], preferred_element_type=F32))
    return parts, accs


def _mla_q_kernel(a_ref, g_ref, w_ref, gain_ref, cos_ref, sin_ref, q_ref, *, heads):
    parts, accs = _normed_products(a_ref, g_ref, w_ref)
    gain = gain_ref[...]
    for rs, acc in zip(parts, accs):
        cos, sin = cos_ref[rs, :], sin_ref[rs, :]
        for h in range(heads):
            c0 = h * HEAD_PAD
            lo, hi = _head_norm_rope(acc[:, c0:c0 + LANE], acc[:, c0 + LANE:c0 + HEAD_PAD], gain[:, :LANE],
                                     gain[:, LANE:], cos, sin, Q_SCALE)
            q_ref[rs, c0:c0 + LANE] = lo.astype(q_ref.dtype)
            q_ref[rs, c0 + LANE:c0 + HEAD_PAD] = hi.astype(q_ref.dtype)


def _mla_kv_kernel(a_ref, g_ref, w_ref, kr_ref, gain_ref, cos_ref, sin_ref, k_ref, v_ref, *, heads):
    parts, accs = _normed_products(a_ref, g_ref, w_ref)
    gain = gain_ref[...]
    for rs, acc in zip(parts, accs):
        kr = kr_ref[rs, :]
        kr_ss = jnp.sum(kr * kr, axis=-1, keepdims=True)
        kr_rot = _rope128(kr * gain[:, LANE:], cos_ref[rs, :], sin_ref[rs, :])
        for h in range(heads):
            c0 = h * HEAD_PAD
            kn = acc[:, c0:c0 + LANE]
            ss = jnp.sum(kn * kn, axis=-1, keepdims=True) + kr_ss
            inv = lax.rsqrt(ss * (1.0 / QK_HEAD) + NORM_EPS)
            k_ref[rs, c0:c0 + LANE] = (kn * inv * gain[:, :LANE]).astype(k_ref.dtype)
            k_ref[rs, c0 + LANE:c0 + HEAD_PAD] = (kr_rot * inv).astype(k_ref.dtype)
            v_ref[rs, h * V_HEAD:(h + 1) * V_HEAD] = acc[:, c0 + LANE:c0 + HEAD_PAD].astype(v_ref.dtype)


def _mla_up(mla_in, q_norm, w_uq, q_gain, kv_norm, w_ukv, k_gain, cos, sin):
    m = mla_in.shape[0]
    tm = _pick(m, (544, 512, 256))
    hpt = 4
    tn = hpt * HEAD_PAD
    grid = (m // tm, MLA_HEADS // hpt)
    row = lambda w: pl.BlockSpec((tm, w), lambda i, j: (i, 0))
    q = pl.pallas_call(
        functools.partial(_mla_q_kernel, heads=hpt),
        out_shape=jax.ShapeDtypeStruct((m, MLA_HEADS * HEAD_PAD), BF16),
        grid=grid,
        in_specs=[pl.BlockSpec((tm, Q_LORA), lambda i, j: (i, 0)),
                  pl.BlockSpec((1, Q_LORA), lambda i, j: (0, 0)),
                  pl.BlockSpec((Q_LORA, tn), lambda i, j: (0, j)),
                  pl.BlockSpec((1, HEAD_PAD), lambda i, j: (0, 0)),
                  row(LANE), row(LANE)],
        out_specs=pl.BlockSpec((tm, tn), lambda i, j: (i, j)),
        compiler_params=_cparams(("parallel", "parallel")),
    )(mla_in, q_norm.reshape(1, -1), w_uq, q_gain, cos, sin)
    k, v = pl.pallas_call(
        functools.partial(_mla_kv_kernel, heads=hpt),
        out_shape=[jax.ShapeDtypeStruct((m, MLA_HEADS * HEAD_PAD), BF16),
                   jax.ShapeDtypeStruct((m, MLA_HEADS * V_HEAD), BF16)],
        grid=grid,
        in_specs=[pl.BlockSpec((tm, KV_LORA), lambda i, j: (i, Q_LORA // KV_LORA)),
                  pl.BlockSpec((1, KV_LORA), lambda i, j: (0, 0)),
                  pl.BlockSpec((KV_LORA, tn), lambda i, j: (0, j)),
                  pl.BlockSpec((tm, LANE), lambda i, j: (i, (Q_LORA + KV_LORA) // LANE)),
                  pl.BlockSpec((1, HEAD_PAD), lambda i, j: (0, 0)),
                  row(LANE), row(LANE)],
        out_specs=[pl.BlockSpec((tm, tn), lambda i, j: (i, j)),
                   pl.BlockSpec((tm, hpt * V_HEAD), lambda i, j: (i, j))],
        compiler_params=_cparams(("parallel", "parallel")),
    )(mla_in, kv_norm.reshape(1, -1), w_ukv, mla_in, k_gain, cos, sin)
    return q, k, v


ATTN_ROWS = 256


def _softmax_pv(q, k, v, o_ref):
    nt = (((1,), (1,)), ((), ()))
    rows = min(ATTN_ROWS, q.shape[0])
    parts = [slice(h * rows, (h + 1) * rows) for h in range(q.shape[0] // rows)]
    scores = [lax.dot_general(q[rs], k, nt, preferred_element_type=F32) for rs in parts]
    v_ext = jnp.concatenate([v, jnp.ones_like(v)], axis=1)
    for rs, s in zip(parts, scores):
        p = jnp.exp2(s - jnp.max(s, axis=-1, keepdims=True))
        o = jnp.dot(p.astype(BF16), v_ext, preferred_element_type=F32)
        o_ref[rs, :] = (o[:, :V_HEAD] * (1.0 / o[:, V_HEAD:])).astype(o_ref.dtype)


def _attn_kernel(q_ref, k_ref, v_ref, o_ref):
    _softmax_pv(q_ref[...], k_ref[...], v_ref[...], o_ref)


def _attention(q, k, v, nb, tt, seq, with_ctx):
    q3 = q.reshape(nb, tt, MLA_HEADS * HEAD_PAD)
    k3 = k.reshape(nb, tt, MLA_HEADS * HEAD_PAD)
    v3 = v.reshape(nb, tt, MLA_HEADS * V_HEAD)
    tq = _pick(seq, (1024, 512, 256))
    o_lat = pl.pallas_call(
        _attn_kernel,
        out_shape=jax.ShapeDtypeStruct((nb, seq, MLA_HEADS * V_HEAD), BF16),
        grid=(nb, MLA_HEADS, seq // tq),
        in_specs=[pl.BlockSpec((None, tq, HEAD_PAD), lambda b, h, i: (b, i, h)),
                  pl.BlockSpec((None, tt, HEAD_PAD), lambda b, h, i: (b, 0, h)),
                  pl.BlockSpec((None, tt, V_HEAD), lambda b, h, i: (b, 0, h))],
        out_specs=pl.BlockSpec((None, tq, V_HEAD), lambda b, h, i: (b, i, h)),
        compiler_params=_cparams(("parallel", "parallel", "arbitrary")),
    )(q3, k3, v3)
    nctx = tt - seq
    if with_ctx:
        cb = seq // nctx
        o_ctx = pl.pallas_call(
            _attn_kernel,
            out_shape=jax.ShapeDtypeStruct((nb, nctx, MLA_HEADS * V_HEAD), BF16),
            grid=(nb, MLA_HEADS),
            in_specs=[pl.BlockSpec((None, nctx, HEAD_PAD), lambda b, h: (b, cb, h)),
                      pl.BlockSpec((None, nctx, HEAD_PAD), lambda b, h: (b, cb, h)),
                      pl.BlockSpec((None, nctx, V_HEAD), lambda b, h: (b, cb, h))],
            out_specs=pl.BlockSpec((None, nctx, V_HEAD), lambda b, h: (b, 0, h)),
            compiler_params=_cparams(("parallel", "parallel")),
        )(q3, k3, v3)
    else:
        o_ctx = jnp.zeros((nb, nctx, MLA_HEADS * V_HEAD), BF16)
    return jnp.concatenate([o_lat, o_ctx], axis=1).reshape(nb * tt, MLA_HEADS * V_HEAD)


def _dft_angles(n, cols):
    rows = jnp.arange(n, dtype=jnp.int32)
    ang = ((rows[:, None] * cols[None, :]) % n).astype(F32) * (2.0 * math.pi / n)
    return jnp.cos(ang), jnp.sin(ang)


def _dft_mats(n, scale):
    if n % GRID_W or n <= GRID_W:
        c, s = _dft_angles(n, jnp.arange(n, dtype=jnp.int32))
    else:
        ca, sa = _dft_angles(n, jnp.arange(n // GRID_W, dtype=jnp.int32) * GRID_W)
        cb, sb = _dft_angles(n, jnp.arange(GRID_W, dtype=jnp.int32))
        c = (ca[:, :, None] * cb[:, None, :] - sa[:, :, None] * sb[:, None, :]).reshape(n, n)
        s = (sa[:, :, None] * cb[:, None, :] + ca[:, :, None] * sb[:, None, :]).reshape(n, n)
    return (c * scale).astype(BF16), (s * scale).astype(BF16)


def _chan_dft_kernel(f_ref, c_ref, s_ref, zc_ref, zs_ref):
    for g in range(FNET_GROUPS):
        cols = slice(g * FNET_GROUP_DIM, (g + 1) * FNET_GROUP_DIM)
        fz = f_ref[:, cols]
        zc_ref[:, cols] = jnp.dot(fz, c_ref[...], preferred_element_type=F32).astype(zc_ref.dtype)
        zs_ref[:, cols] = jnp.dot(fz, s_ref[...], preferred_element_type=F32).astype(zs_ref.dtype)


def _fourier(f, nb, tt, seq, with_ctx, dft):
    c_t, sneg_t, c_c, s_c, c_x, s_x = dft
    m = f.shape[0]
    tmz = _pick(m, (1088, 512, 256))
    row = pl.BlockSpec((tmz, FNET_WIDTH), lambda i: (i, 0))
    mat = pl.BlockSpec((FNET_GROUP_DIM, FNET_GROUP_DIM), lambda i: (0, 0))
    shp = jax.ShapeDtypeStruct((m, FNET_WIDTH), BF16)
    zc, zs = pl.pallas_call(
        _chan_dft_kernel,
        out_shape=[shp, shp],
        grid=(m // tmz,),
        in_specs=[row, mat, mat],
        out_specs=[row, row],
        compiler_params=_cparams(("parallel",)),
    )(f, c_c, s_c)
    zc = zc.reshape(nb, tt, FNET_WIDTH)
    zs = zs.reshape(nb, tt, FNET_WIDTH)

    def ep_sum(p, ex, outs, pids):
        outs[0][...] = (p[0] + p[1]).astype(BF16)

    tm = _pick(seq, (1024, 512, 256))
    tk = tm
    tn = FNET_WIDTH
    nk = seq // tk
    aspec = pl.BlockSpec((tm, tk), lambda b, i, j, k: (i, k))
    bspec = pl.BlockSpec((None, tk, tn), lambda b, i, j, k: (b, k, j))
    y_lat = _fmm((nb, seq // tm, FNET_WIDTH // tn, nk), [c_t, sneg_t], [aspec, aspec], [zc, zs], [bspec, bspec],
                 [], [], [jax.ShapeDtypeStruct((nb, seq, FNET_WIDTH), BF16)],
                 [pl.BlockSpec((None, tm, tn), lambda b, i, j, k: (b, i, j))], (tm, tn), nk, ep_sum)[0]
    nctx = tt - seq
    if with_ctx:
        def ep_diff(p, ex, outs, pids):
            outs[0][...] = (p[0] - p[1]).astype(BF16)

        cb = seq // nctx
        aspec = pl.BlockSpec((nctx, nctx), lambda b, k: (0, 0))
        bspec = pl.BlockSpec((None, nctx, FNET_WIDTH), lambda b, k: (b, cb, 0))
        y_ctx = _fmm((nb, 1), [c_x, s_x], [aspec, aspec], [zc, zs], [bspec, bspec], [], [],
                     [jax.ShapeDtypeStruct((nb, nctx, FNET_WIDTH), BF16)],
                     [pl.BlockSpec((None, nctx, FNET_WIDTH), lambda b, k: (b, 0, 0))],
                     (nctx, FNET_WIDTH), 1, ep_diff)[0]
    else:
        y_ctx = jnp.zeros((nb, nctx, FNET_WIDTH), BF16)
    return jnp.concatenate([y_lat, y_ctx], axis=1).reshape(nb * tt, FNET_WIDTH)


def _head_sum(x, ones_bd):
    parts = []
    for s in range(x.shape[1] // LANE):
        hi, lo = _split2(x[:, s * LANE:(s + 1) * LANE])
        parts.append(jnp.dot(hi, ones_bd, preferred_element_type=F32)
                     + jnp.dot(lo, ones_bd, preferred_element_type=F32))
    return jnp.concatenate(parts, axis=-1)


def _feat_kernel(*refs, tf, tpb, lat_tiles, has_vres):
    (z_ref, zp_ref, zn_ref, mu_ref, w0_ref, w2_ref, a0_ref, a2_ref, kk_ref_p, ka_ref, bd_ref) = refs[:11]
    pos = 11
    if has_vres:
        vf_ref, v0_ref, v2_ref = refs[pos:pos + 3]
        pos += 3
    outs = refs[pos:]
    r_o, v_o, kk_o, kf_o, kb_o, lwf_o, lwb_o, af_o, ab_o, sg_o = outs[:10]
    j = pl.program_id(1)
    has_prev = jnp.logical_and(j != 0, j != lat_tiles)
    has_next = jnp.logical_and(j != lat_tiles - 1, j != tpb - 1)
    row = lax.broadcasted_iota(jnp.int32, (tf, 1), 0)

    def zf(lo, hi):
        z = z_ref[:, lo:hi]
        prev_row = jnp.where(has_prev, zp_ref[7:8, lo:hi], 0.0)
        next_row = jnp.where(has_next, zn_ref[0:1, lo:hi], 0.0)
        zp = jnp.where(row == 0, prev_row, pltpu.roll(z, 1, 0))
        zn = jnp.where(row == tf - 1, next_row, pltpu.roll(z, tf - 1, 0))
        return z + mu_ref[:, lo:hi] * (0.5 * (zp + zn) - z)

    w = RWKV_WIDTH
    r = zf(0, w)
    k = zf(w, 2 * w)
    v = zf(2 * w, 3 * w)
    r_o[...] = r
    if has_vres:
        vl = zf(Z_MAIN + GATE_PAD, Z_PAD)
        gate = _sigmoid(v0_ref[...] + _bdot(vl, v2_ref[...]))
        v_o[...] = v + (vf_ref[...] - v) * gate
    else:
        v_o[...] = v
        outs[10][...] = v
    kk = k * kk_ref_p[...]
    ss = _head_sum(kk * kk, bd_ref[...])
    kk_o[...] = kk * lax.rsqrt(ss + 1e-12)
    base = 3 * w
    for d, (lw_o, a_o, k_o) in enumerate(((lwf_o, af_o, kf_o), (lwb_o, ab_o, kb_o))):
        wd = zf(base + d * DECAY_LORA, base + (d + 1) * DECAY_LORA)
        ad = zf(base + 2 * DECAY_LORA + d * ICLR_LORA, base + 2 * DECAY_LORA + (d + 1) * ICLR_LORA)
        xw = w0_ref[d:d + 1, :] + _bdot(jnp.tanh(wd), w2_ref[d])
        lw_o[...] = -_sigmoid(xw) * math.exp(-0.5)
        a = _sigmoid(a0_ref[d:d + 1, :] + _bdot(ad, a2_ref[d]))
        a_o[...] = a
        k_o[...] = k * (1.0 + (a - 1.0) * ka_ref[...])
    sg_o[...] = _sigmoid(zf(Z_MAIN, Z_MAIN + GATE_PAD)).astype(sg_o.dtype)


def _rwkv_features(z, nb, tt, seq, mu, w0, w2, a0, a2, k_k, k_a, ones_bd, vres):
    tf = 128
    tpb = tt // tf
    lat_tiles = seq // tf
    z3 = z.reshape(nb, tt, Z_PAD)
    hb = tf // 8
    last8 = tt // 8 - 1
    w = RWKV_WIDTH
    full = lambda shape: pl.BlockSpec(shape, lambda b, j: (0,) * len(shape))
    tile = lambda width: pl.BlockSpec((None, tf, width), lambda b, j: (b, j, 0))
    in_specs = [tile(Z_PAD),
                pl.BlockSpec((None, 8, Z_PAD), lambda b, j: (b, jnp.maximum(j * hb - 1, 0), 0)),
                pl.BlockSpec((None, 8, Z_PAD), lambda b, j: (b, jnp.minimum((j + 1) * hb, last8), 0)),
                full((1, Z_PAD)), full((2, w)), full((2, DECAY_LORA, w)), full((2, w)), full((2, ICLR_LORA, w)),
                full((1, w)), full((1, w)), full((LANE, LANE))]
    args = [z3, z3, z3, mu.reshape(1, Z_PAD), w0, w2, a0, a2, k_k.reshape(1, w), k_a.reshape(1, w), ones_bd]
    has_vres = vres is not None
    if has_vres:
        v_first, v0, v2 = vres
        in_specs += [tile(w), full((1, w)), full((VRES_PAD, w))]
        args += [v_first.reshape(nb, tt, w), v0.reshape(1, w), v2]
    f3 = jax.ShapeDtypeStruct((nb, tt, w), F32)
    out_shape = [f3] * 9 + [jax.ShapeDtypeStruct((nb, tt, GATE_PAD), BF16)]
    out_specs = [tile(w)] * 9 + [tile(GATE_PAD)]
    if not has_vres:
        out_shape.append(f3)
        out_specs.append(tile(w))
    return pl.pallas_call(
        functools.partial(_feat_kernel, tf=tf, tpb=tpb, lat_tiles=lat_tiles, has_vres=has_vres),
        out_shape=out_shape,
        grid=(nb, tpb),
        in_specs=in_specs,
        out_specs=out_specs,
        compiler_params=_cparams(("parallel", "arbitrary")),
    )(*args)


def _stack2(x, lo_mask):
    return jnp.concatenate([jnp.where(lo_mask, x, 0.0), jnp.where(lo_mask, 0.0, x)], axis=0)


def _dot16(a, b):
    return jnp.dot(a.astype(BF16), b.astype(BF16), preferred_element_type=F32)


def _dg16(a, b, dims):
    return lax.dot_general(a.astype(BF16), b.astype(BF16), dims, preferred_element_type=F32)


def _bmm(a, b):
    return lax.dot_general(a.astype(BF16), b.astype(BF16), (((2,), (1,)), ((0,), (0,))),
                           preferred_element_type=F32)


def _bmm_nt(a, b):
    return lax.dot_general(a.astype(BF16), b.astype(BF16), (((2,), (2,)), ((0,), (0,))),
                           preferred_element_type=F32)


_GROUP_OPERANDS = ("kkt", "rt", "bh", "kh", "kb", "bb", "v")


def _group_operands(r, k, v, lw, kk, a, rev, consts, g):
    tri_cum, lo_mask = consts[0], consts[3]
    c = CHUNK
    s2 = lambda x: _stack2(x, lo_mask)
    cols = {name: [] for name in _GROUP_OPERANDS}
    tots = []
    for ch in range(g):
        sl = slice(ch * c, (ch + 1) * c)
        lw_c = lw[sl]
        cum = _dot_exact_rhs_left(tri_cum, lw_c)
        tot = cum[0:1, :] if rev else cum[c - 1:c, :]
        g_inv = jnp.exp(-cum)
        g_tail = jnp.exp(tot - cum)
        b_c = kk[sl] * a[sl]
        cols["kkt"].append(s2(kk[sl] * jnp.exp(cum - lw_c)))
        cols["rt"].append(s2(r[sl] * jnp.exp(cum)))
        cols["bh"].append(s2(b_c * g_inv))
        cols["kh"].append(s2(k[sl] * g_inv))
        cols["kb"].append(s2(k[sl] * g_tail))
        cols["bb"].append(s2(b_c * g_tail))
        cols["v"].append(s2(v[sl]))
        tots.append(tot)
    return {name: jnp.concatenate(cols[name], axis=0) for name in _GROUP_OPERANDS}, tots


def _solve_groups(ops, consts):
    _, strict, incl, _, eye_big, _, base_mask, level_masks = consts
    rt = ops["rt"]
    kkt16, rt16, bh16, kh16, v16 = (ops[nm].astype(BF16) for nm in ("kkt", "rt", "bh", "kh", "v"))
    nr = rt.shape[1]
    lhs = jnp.concatenate([kkt16, rt16], axis=1)
    big_b = _bmm_nt(lhs, bh16)
    big_k = _bmm_nt(lhs, kh16)
    m_k = jnp.where(strict, big_k[:, :nr], 0.0)
    a_qb = jnp.where(incl, big_b[:, nr:], 0.0)
    a_qk = jnp.where(incl, big_k[:, nr:], 0.0)
    m_b = jnp.where(strict, big_b[:, :nr], 0.0)
    n = jnp.where(base_mask, -m_b, 0.0)
    tinv = eye_big + n
    for _ in range(int(math.log2(INV_BASE)) - 1):
        n = _bmm(n, n)
        tinv = tinv + _bmm(tinv, n)
    for off_mask in level_masks:
        tinv = tinv - _bmm(_bmm(tinv, jnp.where(off_mask, m_b, 0.0)), tinv)
    mkv = _bmm(m_k, v16)
    wu16 = _bmm(tinv, jnp.concatenate([kkt16, mkv.astype(BF16)], axis=2)).astype(BF16)
    qy = _bmm(a_qb, wu16)
    q16 = (rt - qy[:, :, :LANE]).astype(BF16)
    y0 = _bmm(a_qk, v16) - qy[:, :, LANE:]
    return wu16, q16, y0


def _dot_exact_rhs_left(tri_bf16, x):
    h, m, l = _split3(x)
    d = functools.partial(jnp.dot, preferred_element_type=F32)
    return d(tri_bf16, h) + (d(tri_bf16, m) + d(tri_bf16, l))


def _scan_kernel(r_ref, k_ref, v_ref, lw_ref, kk_ref, a_ref, y_ref, h_ref, *, rev, n_groups, g):
    @pl.when(pl.program_id(2) == 0)
    def _():
        h_ref[...] = jnp.zeros_like(h_ref)

    c = CHUNK
    ri = lax.broadcasted_iota(jnp.int32, (c, c), 0)
    ci = lax.broadcasted_iota(jnp.int32, (c, c), 1)
    tri_cum = (ci >= ri if rev else ci <= ri).astype(BF16)
    nr = g * 2 * c
    r2 = lax.broadcasted_iota(jnp.int32, (nr, nr), 0)
    c2 = lax.broadcasted_iota(jnp.int32, (nr, nr), 1)
    same = (r2 // (2 * c)) == (c2 // (2 * c))
    strict = jnp.logical_and(same, c2 > r2 if rev else c2 < r2)
    incl = jnp.logical_and(same, c2 >= r2 if rev else c2 <= r2)
    eye_big = (r2 == c2).astype(F32)
    eye = eye_big[:LANE, :LANE]
    lo_mask = lax.broadcasted_iota(jnp.int32, (c, LANE), 1) < RWKV_HEAD_DIM
    base_mask = (r2 // INV_BASE) == (c2 // INV_BASE)
    level_masks = []
    s = 2 * INV_BASE
    while s <= c:
        level_masks.append(jnp.logical_and((r2 // s) == (c2 // s), (r2 // (s // 2)) != (c2 // (s // 2))))
        s *= 2
    consts = (tri_cum, strict, incl, lo_mask, eye_big, eye, base_mask, level_masks)
    gl = g * c
    rows = 2 * c
    units = [(pi, gi) for pi in range(SCAN_PAIRS) for gi in range(n_groups)]
    ops_list, tots_list = [], []
    for pi, gi in units:
        sl = slice(gi * gl, (gi + 1) * gl)
        ls = slice(pi * LANE, (pi + 1) * LANE)
        ops, tots = _group_operands(r_ref[sl, ls], k_ref[sl, ls], v_ref[sl, ls], lw_ref[sl, ls], kk_ref[sl, ls],
                                    a_ref[sl, ls], rev, consts, g)
        ops_list.append(ops)
        tots_list.append(tots)
    stacked = {name: jnp.stack([ops[name] for ops in ops_list]) for name in _GROUP_OPERANDS}
    wu16, q16, y0 = _solve_groups(stacked, consts)
    kb16 = stacked["kb"].astype(BF16)
    bb16 = stacked["bb"].astype(BF16)
    v16 = stacked["v"].astype(BF16)
    tn = (((0,), (0,)), ((), ()))
    for pi in range(SCAN_PAIRS):
        h = h_ref[pi]
        for gi in (range(n_groups - 1, -1, -1) if rev else range(n_groups)):
            u = units.index((pi, gi))
            for ch in (range(g - 1, -1, -1) if rev else range(g)):
                rs = slice(ch * rows, (ch + 1) * rows)
                pg = _dg16(bb16[u, rs], wu16[u, rs], tn)
                p_mat = eye * jnp.exp(tots_list[u][ch]) - pg[:, :LANE]
                g_mat = _dg16(kb16[u, rs], v16[u, rs], tn) - pg[:, LANE:]
                h16 = h.astype(BF16)
                y_st = _dot16(q16[u, rs], h16) + y0[u, rs]
                h = _dot16(p_mat, h16) + g_mat
                t0 = gi * gl + ch * c
                y_ref[t0:t0 + c, pi * LANE:(pi + 1) * LANE] = y_st[:c] + y_st[c:]
        h_ref[pi] = h


def _rwkv_scan(r, k, v, lw, kk, a, rev, seq):
    nb, tt, w = r.shape
    nblk = tt // SCAN_BLOCK
    lat_blk = seq // SCAN_BLOCK
    if rev:
        def blk(j):
            return nblk - 1 - j
    else:
        def blk(j):
            return jnp.where(j < nblk - lat_blk, lat_blk + j, j - (nblk - lat_blk))
    spec = pl.BlockSpec((None, SCAN_BLOCK, SCAN_PAIRS * LANE), lambda b, p, j: (b, blk(j), p))
    return pl.pallas_call(
        functools.partial(_scan_kernel, rev=rev, n_groups=SCAN_BLOCK // (CHUNK * SCAN_GROUP), g=SCAN_GROUP),
        out_shape=jax.ShapeDtypeStruct((nb, tt, w), F32),
        grid=(nb, w // (SCAN_PAIRS * LANE), nblk),
        in_specs=[spec] * 6,
        out_specs=spec,
        scratch_shapes=[pltpu.VMEM((SCAN_PAIRS, LANE, LANE), F32)],
        compiler_params=_cparams(("parallel", "parallel", "arbitrary")),
    )(r, k, v, lw, kk, a)


def _readout_kernel(yf_ref, yb_ref, r_ref, kf_ref, kb_ref, v_ref, sg_ref, g2_ref, rk_ref, lw_ref, lb_ref, bd_ref,
                    o_ref):
    bd = bd_ref[...]
    inv = 1.0 / RWKV_HEAD_DIM
    y = yf_ref[...] + yb_ref[...]
    mean = _head_sum(y, bd) * inv
    dlt = y - mean
    var = _head_sum(dlt * dlt, bd) * inv
    yn = dlt * lax.rsqrt(var + LNX_EPS) * lw_ref[...] + lb_ref[...]
    bonus = _head_sum(r_ref[...] * (kf_ref[...] + kb_ref[...]) * rk_ref[...], bd) * v_ref[...]
    g = jnp.dot(sg_ref[...], g2_ref[...], preferred_element_type=F32)
    o_ref[...] = ((yn + bonus) * g).astype(o_ref.dtype)


def _rwkv_readout(yf, yb, r, kf, kb, v, sg, g2, r_k, lnx_w, lnx_b, ones_bd):
    m, w = yf.shape
    tm = _pick(m, (272, 256, 128))
    row = lambda width: pl.BlockSpec((tm, width), lambda i: (i, 0))
    full = lambda shape: pl.BlockSpec(shape, lambda i: (0,) * len(shape))
    return pl.pallas_call(
        _readout_kernel,
        out_shape=jax.ShapeDtypeStruct((m, w), BF16),
        grid=(m // tm,),
        in_specs=[row(w)] * 6 + [row(GATE_PAD), full((GATE_PAD, w)), full((1, w)), full((1, w)), full((1, w)),
                                 full((LANE, LANE))],
        out_specs=row(w),
        compiler_params=_cparams(("parallel",)),
    )(yf, yb, r, kf, kb, v, sg, g2, r_k.reshape(1, w), lnx_w.reshape(1, w), lnx_b.reshape(1, w), ones_bd)


def _merge(gates, o_a, o_b, o_c, w_a, w_b, w_c):
    m = o_a.shape[0]
    d = w_a[0].shape[-1]
    tm = _pick(m, (1088, 512, 256))
    tn = 512

    def ep(p, ex, outs, pids):
        y = (_sigmoid(ex[0][...].astype(F32)) * p[0] + _sigmoid(ex[1][...].astype(F32)) * p[1]
             + _sigmoid(ex[2][...].astype(F32)) * p[2])
        outs[0][...] = y.astype(BF16)

    nbn = d // tn
    a_specs = [pl.BlockSpec((tm, o.shape[1]), lambda i, j, k: (i, 0)) for o in (o_a, o_b, o_c)]
    ws, b_specs = zip(*[_wspec(wt, wt[0].shape[-2], tn) for wt in (w_a, w_b, w_c)])
    g_specs = [pl.BlockSpec((tm, tn), functools.partial(lambda i, j, k, off: (i, off + j), off=br * nbn))
               for br in range(N_BRANCHES)]
    return _fmm((m // tm, nbn, 1), [o_a, o_b, o_c], a_specs, ws, b_specs, [gates] * 3, g_specs,
                [jax.ShapeDtypeStruct((m, d), BF16)], [pl.BlockSpec((tm, tn), lambda i, j, k: (i, j))],
                (tm, tn), 1, ep)[0]


def _mm_resid(a, w, x, gate, nb, tt, seq, tk, tn):
    m, kdim = a.shape
    tm = _pick(tt, (1088, 512, 256))
    tpb = tt // tm
    nk = kdim // tk
    w, wspec = _wspec(w, tk, tn)
    d = w.shape[-1]

    def ep(p, ex, outs, pids):
        ctx = _ctx_rows(pids[0], tm, tpb, seq)
        g = jnp.where(ctx, ex[2][...], ex[1][...])
        outs[0][...] = ex[0][...] + g * p[0]

    xspec = pl.BlockSpec((tm, tn), lambda i, j, k: (i, j))
    extra_specs = [xspec,
                   pl.BlockSpec((None, 1, tn), lambda i, j, k: (i // tpb, 0, j)),
                   pl.BlockSpec((None, 1, tn), lambda i, j, k: (nb, 0, j))]
    return _fmm((m // tm, d // tn, nk), [a], [pl.BlockSpec((tm, tk), lambda i, j, k: (i, k))],
                [w], [wspec], [x, gate, gate], extra_specs,
                [jax.ShapeDtypeStruct((m, d), F32)], [xspec], (tm, tn), nk, ep, aliases={2: 0})[0]


def _swiglu(u, w1, w3, tn, tk):
    m = u.shape[0]
    tm = _pick(m, (1088, 512, 256))

    def ep(p, ex, outs, pids):
        outs[0][...] = (_silu(p[0]) * p[1]).astype(BF16)

    return _flat_mm(u, None, BF16, tm, tn, tk, epilogue=ep, bs=[w1, w3])[0]


ROUTE_LANE = 8


def _router_kernel(u_ref, w_ref, o_ref):
    logits = _dot_x3(u_ref[...].astype(F32), w_ref[...])
    lane = lax.broadcasted_iota(jnp.int32, logits.shape, 1).astype(F32)
    valid = lane < N_EXPERTS
    neg = -1e30
    lg = jnp.where(valid, logits, neg)
    m1 = jnp.max(lg, axis=-1, keepdims=True)
    i1 = jnp.min(jnp.where(lg == m1, lane, float(LANE)), axis=-1, keepdims=True)
    lg2 = jnp.where(lane == i1, neg, lg)
    m2 = jnp.max(lg2, axis=-1, keepdims=True)
    i2 = jnp.min(jnp.where(lg2 == m2, lane, float(LANE)), axis=-1, keepdims=True)
    e2 = jnp.exp(m2 - m1)
    w1 = 1.0 / (1.0 + e2)
    w2 = e2 / (1.0 + e2)
    comb = jnp.where(lane == i1, w1, 0.0) + jnp.where(lane == i2, w2, 0.0)
    meta = (jnp.where(lane == ROUTE_LANE, i1, 0.0) + jnp.where(lane == ROUTE_LANE + 1, i2, 0.0)
            + jnp.where(lane == ROUTE_LANE + 2, w1, 0.0) + jnp.where(lane == ROUTE_LANE + 3, w2, 0.0))
    o_ref[...] = comb + meta


def _router(u, router_pad):
    m, d = u.shape
    tm = _pick(m, (544, 512, 256))
    return pl.pallas_call(
        _router_kernel,
        out_shape=jax.ShapeDtypeStruct((m, LANE), F32),
        grid=(m // tm,),
        in_specs=[pl.BlockSpec((tm, d), lambda i: (i, 0)), pl.BlockSpec((d, LANE), lambda i: (0, 0))],
        out_specs=pl.BlockSpec((tm, LANE), lambda i: (i, 0)),
        compiler_params=_cparams(("parallel",)),
    )(u, router_pad)


MOE_TILE = 512
GATHER_TILE = 256


def _route_plan(e1, e2, tm):
    n_tok = e1.shape[0]
    e_flat = jnp.stack([e1, e2], axis=1).reshape(-1)
    n_asg = e_flat.shape[0]
    onehot = (e_flat[:, None] == jnp.arange(N_EXPERTS, dtype=jnp.int32)[None, :]).astype(jnp.int32)
    rank = jnp.sum((jnp.cumsum(onehot, axis=0) - onehot) * onehot, axis=1)
    counts = jnp.sum(onehot, axis=0)
    padded = ((counts + tm - 1) // tm) * tm
    ends = jnp.cumsum(padded)
    slot = (ends - padded)[e_flat] + rank
    n_slots = n_asg + N_EXPERTS * tm
    src_tok = jnp.zeros((n_slots,), jnp.int32).at[slot].set(jnp.arange(n_asg, dtype=jnp.int32) // TOP_K)
    tile_start = jnp.arange(n_slots // tm, dtype=jnp.int32) * tm
    tile_e = jnp.sum((tile_start[:, None] >= ends[None, :]).astype(jnp.int32), axis=1)
    valid = (tile_start < ends[-1]).astype(jnp.int32)
    tile_e = jnp.where(valid == 1, tile_e, tile_e[jnp.maximum(ends[-1] // tm - 1, 0)])
    return src_tok, slot.reshape(n_tok, TOP_K), tile_e, valid


def _row_copy(src_hbm, row, buf, r, sem):
    return pltpu.make_async_copy(src_hbm.at[pl.ds(row, 1)], buf.at[pl.ds(r, 1)], sem)


def _gather_kernel(idx_ref, src_hbm, o_ref, buf, sem, *, tg, n_steps):
    i = pl.program_id(0)
    slot = i % 2

    def issue(step, s):
        def start(r2, carry):
            for prio in range(2):
                r = 2 * r2 + prio
                _row_copy(src_hbm, idx_ref[step * tg + r], buf.at[s], r, sem.at[s]).start(priority=prio)
            return carry

        lax.fori_loop(0, tg // 2, start, 0, unroll=4)

    @pl.when(i == 0)
    def _():
        issue(0, 0)

    @pl.when(i + 1 < n_steps)
    def _():
        issue(i + 1, 1 - slot)

    def wait(r, carry):
        _row_copy(src_hbm, 0, buf.at[slot], r, sem.at[slot]).wait()
        return carry

    lax.fori_loop(0, tg, wait, 0, unroll=8)
    o_ref[...] = buf[slot].astype(o_ref.dtype)


def _gather_rows(src, idx, out_dtype):
    n_slots = idx.shape[0]
    d = src.shape[1]
    tg = GATHER_TILE
    return pl.pallas_call(
        functools.partial(_gather_kernel, tg=tg, n_steps=n_slots // tg),
        out_shape=jax.ShapeDtypeStruct((n_slots, d), out_dtype),
        grid_spec=pltpu.PrefetchScalarGridSpec(
            num_scalar_prefetch=1,
            grid=(n_slots // tg,),
            in_specs=[pl.BlockSpec(memory_space=pl.ANY)],
            out_specs=pl.BlockSpec((tg, d), lambda i, idx_ref: (i, 0)),
            scratch_shapes=[pltpu.VMEM((2, tg, d), src.dtype), pltpu.SemaphoreType.DMA((2,))]),
        compiler_params=_cparams(("arbitrary",)),
    )(idx, src)


def _grouped_up_kernel(te_ref, tv_ref, a_ref, w1_ref, w3_ref, o_ref):
    valid = tv_ref[pl.program_id(1)] == 1

    @pl.when(valid)
    def _():
        a = a_ref[...]
        p1 = jnp.dot(a, w1_ref[...].astype(BF16), preferred_element_type=F32)
        p3 = jnp.dot(a, w3_ref[...].astype(BF16), preferred_element_type=F32)
        o_ref[...] = (_silu(p1) * p3).astype(o_ref.dtype)

    @pl.when(jnp.logical_not(valid))
    def _():
        o_ref[...] = jnp.zeros_like(o_ref)


def _grouped_down_kernel(te_ref, tv_ref, a_ref, w_ref, o_ref):
    valid = tv_ref[pl.program_id(1)] == 1

    @pl.when(valid)
    def _():
        o_ref[...] = jnp.dot(a_ref[...], w_ref[...].astype(BF16), preferred_element_type=F32)

    @pl.when(jnp.logical_not(valid))
    def _():
        o_ref[...] = jnp.zeros_like(o_ref)


def _grouped_mm(kern, a, ws, layer, tile_e, tile_valid, tn, out_dtype):
    n_slots, kdim = a.shape
    n = ws[0].shape[-1]
    tm = MOE_TILE
    wspec = pl.BlockSpec((None, None, kdim, tn), lambda j, i, te, tv: (layer, te[i], 0, j))
    return pl.pallas_call(
        kern,
        out_shape=jax.ShapeDtypeStruct((n_slots, n), out_dtype),
        grid_spec=pltpu.PrefetchScalarGridSpec(
            num_scalar_prefetch=2,
            grid=(n // tn, n_slots // tm),
            in_specs=[pl.BlockSpec((tm, kdim), lambda j, i, te, tv: (i, 0))] + [wspec] * len(ws),
            out_specs=pl.BlockSpec((tm, tn), lambda j, i, te, tv: (i, j))),
        compiler_params=_cparams(("parallel", "arbitrary")),
    )(tile_e, tile_valid, a, *ws)


def _combine_kernel(s1_ref, s2_ref, ys_hbm, x_ref, rw_ref, gate_ref, o_ref, buf1, buf2, sem, *, tc, spb, n_steps):
    i = pl.program_id(0) * spb + pl.program_id(1)
    slot = i % 2

    def issue(step, s):
        def start(r, carry):
            _row_copy(ys_hbm, s1_ref[step * tc + r], buf1.at[s], r, sem.at[s]).start(priority=0)
            _row_copy(ys_hbm, s2_ref[step * tc + r], buf2.at[s], r, sem.at[s]).start(priority=1)
            return carry

        lax.fori_loop(0, tc, start, 0, unroll=8)

    @pl.when(i == 0)
    def _():
        issue(0, 0)

    @pl.when(i + 1 < n_steps)
    def _():
        issue(i + 1, 1 - slot)

    def wait(r, carry):
        _row_copy(ys_hbm, 0, buf1.at[slot], r, sem.at[slot]).wait()
        _row_copy(ys_hbm, 0, buf2.at[slot], r, sem.at[slot]).wait()
        return carry

    lax.fori_loop(0, tc, wait, 0, unroll=8)
    rw = rw_ref[...]
    lane = lax.broadcasted_iota(jnp.int32, rw.shape, 1)
    w1 = jnp.sum(jnp.where(lane == ROUTE_LANE + 2, rw, 0.0), axis=-1, keepdims=True)
    w2 = jnp.sum(jnp.where(lane == ROUTE_LANE + 3, rw, 0.0), axis=-1, keepdims=True)
    o_ref[...] = x_ref[...] + gate_ref[...] * (w1 * buf1[slot] + w2 * buf2[slot])


def _moe_combine(x, ys, slots, route, gate, nb, tt, seq, latent_only):
    m, d = x.shape
    tc = GATHER_TILE
    spb = seq // tc
    bpb = tt // tc
    row = lambda b, i, s1, s2: (b * bpb + i, 0)
    out_row = (lambda b, i, s1, s2: (b * spb + i, 0)) if latent_only else row
    return pl.pallas_call(
        functools.partial(_combine_kernel, tc=tc, spb=spb, n_steps=nb * spb),
        out_shape=jax.ShapeDtypeStruct((nb * seq if latent_only else m, d), F32),
        grid_spec=pltpu.PrefetchScalarGridSpec(
            num_scalar_prefetch=2,
            grid=(nb, spb),
            in_specs=[pl.BlockSpec(memory_space=pl.ANY),
                      pl.BlockSpec((tc, d), row),
                      pl.BlockSpec((tc, LANE), row),
                      pl.BlockSpec((None, 1, d), lambda b, i, s1, s2: (b, 0, 0))],
            out_specs=pl.BlockSpec((tc, d), out_row),
            scratch_shapes=[pltpu.VMEM((2, tc, d), F32), pltpu.VMEM((2, tc, d), F32),
                            pltpu.SemaphoreType.DMA((2,))]),
        input_output_aliases={} if latent_only else {3: 0},
        compiler_params=_cparams(("arbitrary", "arbitrary")),
    )(slots[:, 0], slots[:, 1], ys, x, route, gate)


def _moe_routed(x, u, route, moe_w1, moe_w3, moe_w2, layer, gate, nb, tt, seq, latent_only):
    m, d = x.shape
    meta = route.reshape(nb, tt, LANE)[:, :seq].reshape(nb * seq, LANE)
    e1 = meta[:, ROUTE_LANE].astype(jnp.int32)
    e2 = meta[:, ROUTE_LANE + 1].astype(jnp.int32)
    src_tok, slots, tile_e, tile_valid = _route_plan(e1, e2, MOE_TILE)
    src_row = (src_tok // seq) * tt + src_tok % seq
    xs = _gather_rows(u, src_row, BF16)
    h = _grouped_mm(_grouped_up_kernel, xs, [moe_w1, moe_w3], layer, tile_e, tile_valid, 512, BF16)
    ys = _grouped_mm(_grouped_down_kernel, h, [moe_w2], layer, tile_e, tile_valid, 1024, F32)
    return _moe_combine(x, ys, slots, route, gate, nb, tt, seq, latent_only)


def _rope_lane_tables(seq, nctx, nb):
    rows = seq // GRID_W
    row = jnp.repeat(jnp.arange(rows, dtype=F32), GRID_W)
    col = jnp.tile(jnp.arange(GRID_W, dtype=F32), rows)
    inv_freq = ROPE_BASE ** (-2.0 * jnp.arange(ROPE_PAIRS, dtype=F32) / ROPE_AXIS_DIM)
    ar = row[:, None] * inv_freq
    ac = col[:, None] * inv_freq
    ones = jnp.ones((seq, LANE - QK_ROPE), F32)
    zeros = jnp.zeros((seq, LANE - QK_ROPE), F32)
    cos = jnp.concatenate([jnp.cos(ar), jnp.cos(ar), jnp.cos(ac), jnp.cos(ac), ones], axis=1)
    sin = jnp.concatenate([-jnp.sin(ar), jnp.sin(ar), -jnp.sin(ac), jnp.sin(ac), zeros], axis=1)
    cos = jnp.concatenate([cos, jnp.ones((nctx, LANE), F32)], axis=0)
    sin = jnp.concatenate([sin, jnp.zeros((nctx, LANE), F32)], axis=0)
    return jnp.tile(cos, (nb, 1)), jnp.tile(sin, (nb, 1))


def _pad_cols(w, n):
    return jnp.concatenate([w, jnp.zeros((w.shape[0], n - w.shape[1]), w.dtype)], axis=1)


def _pad_rows(w, n):
    return jnp.concatenate([w, jnp.zeros((n - w.shape[0], w.shape[1]), w.dtype)], axis=0)


def _z_layout(zsrc, vres_src, axis=-1):
    axis = axis % zsrc.ndim

    def zeros(n):
        shape = list(zsrc.shape)
        shape[axis] = n
        return jnp.zeros(shape, zsrc.dtype)

    parts = [lax.slice_in_dim(zsrc, 0, Z_MAIN, axis=axis), lax.slice_in_dim(zsrc, Z_MAIN, zsrc.shape[axis], axis=axis),
             zeros(GATE_PAD - GATE_LORA)]
    if vres_src is None:
        parts.append(zeros(VRES_PAD))
    else:
        parts += [vres_src, zeros(VRES_PAD - VRES_LORA)]
    return jnp.concatenate(parts, axis=axis)


def _pad_head(g):
    return jnp.pad(g, (0, HEAD_PAD - QK_HEAD)).reshape(1, HEAD_PAD)


def kernel(x, c, ctx, c_ctx, ada_w, ada_b, norm1, w_in, w_vres_down, q_norm, w_uq, kv_norm, w_ukv, q_gain, k_gain,
           rwkv_mu, vres_mu, w0, w2, a0, a2, k_k, k_a, v0, v2, r_k, lnx_w, lnx_b, g2, w_br_a, w_br_b, w_br_c,
           w_out, norm2, ffn_w1, ffn_w3, ffn_w2, router, moe_w1, moe_w3, moe_w2):
    nb, seq, d = x.shape
    nctx = ctx.shape[1]
    tt = seq + nctx
    m = nb * tt
    depth = ada_w.shape[0]
    xa = jnp.concatenate([x, ctx], axis=1).reshape(m, d)
    cond8 = jnp.concatenate([c, c_ctx[None], jnp.zeros((8 - nb - 1, d), F32)], axis=0)
    cos_t, sin_t = _rope_lane_tables(seq, nctx, nb)
    c_t, s_t = _dft_mats(seq, 1.0 / math.sqrt(seq))
    c_c, s_c = _dft_mats(FNET_GROUP_DIM, 1.0 / math.sqrt(FNET_GROUP_DIM))
    c_x, s_x = _dft_mats(nctx, 1.0 / math.sqrt(nctx))
    dft = (c_t, -s_t, c_c, s_c, c_x, s_x)
    li = jnp.arange(LANE)
    ones_bd = (li[:, None] // RWKV_HEAD_DIM == li[None, :] // RWKV_HEAD_DIM).astype(BF16)
    v_first = None
    gate_cols = N_BRANCHES * d
    w_in_t = jnp.swapaxes(w_in, 1, 2)
    for i in range(depth):
        last = i == depth - 1
        mod = _adaln(cond8, ada_w, i, ada_b[i])[:nb + 1].reshape(nb + 1, 6, 1, d)
        sh1, sc1, gm, sh2, sc2, gf = (mod[:, t] for t in range(6))
        wi = w_in_t[i]
        w_mla = _pad_rows(wi[gate_cols:gate_cols + MLA_IN], MLA_IN_PAD).astype(BF16)
        f0 = gate_cols + MLA_IN
        w_f = wi[f0:f0 + FNET_WIDTH].astype(BF16)
        zsrc = wi[f0 + FNET_WIDTH:]
        if i == 0:
            w_z = _z_layout(zsrc, None, 0).astype(BF16)
            mu_z = _z_layout(rwkv_mu[i], None)
        else:
            w_z = _z_layout(zsrc, jnp.swapaxes(w_vres_down[i - 1], 0, 1), 0).astype(BF16)
            mu_z = _z_layout(rwkv_mu[i], vres_mu[i - 1])

        u = _norm_mod(xa, norm1[i], sh1, sc1, nb, tt, seq)
        tm = _pick(m, (1088, 512, 256))
        gates = _flat_mm(u, (w_in_t, (i,)), BF16, tm, 512, d, n=gate_cols, wt=True)[0]
        mla_in = _flat_mm(u, w_mla, F32, _pick(m, (544, 512, 256)), MLA_IN_PAD, 2048, wt=True)[0]
        f = _flat_mm(u, w_f, BF16, tm, 512, d, wt=True)[0]
        z = _flat_mm(u, w_z, F32, _pick(m, (544, 512, 256)), Z_PAD // 3, d, wt=True)[0]

        w_uq_p = jnp.pad(w_uq[i].reshape(Q_LORA, MLA_HEADS, QK_HEAD),
                         ((0, 0), (0, 0), (0, HEAD_PAD - QK_HEAD))).reshape(Q_LORA, MLA_HEADS * HEAD_PAD)
        q, k, v = _mla_up(mla_in, q_norm[i], w_uq_p.astype(BF16), _pad_head(q_gain[i]), kv_norm[i],
                          w_ukv[i].astype(BF16), _pad_head(k_gain[i]), cos_t, sin_t)
        o_a = _attention(q, k, v, nb, tt, seq, not last)

        o_b = _fourier(f, nb, tt, seq, not last, dft)

        vres = None
        if i > 0:
            vres = (v_first, v0[i - 1], _pad_rows(v2[i - 1], VRES_PAD).astype(BF16))
        feat = _rwkv_features(z, nb, tt, seq, mu_z, w0[i], w2[i].astype(BF16), a0[i], a2[i].astype(BF16),
                              k_k[i], k_a[i], ones_bd, vres)
        r_, v_, kk_, kf_, kb_, lwf_, lwb_, af_, ab_, sg_ = feat[:10]
        if i == 0:
            v_first = feat[10]
        y_f = _rwkv_scan(r_, kf_, v_, lwf_, kk_, af_, False, seq)
        y_b = _rwkv_scan(r_, kb_, v_, lwb_, kk_, ab_, True, seq)
        flat = lambda t: t.reshape(m, t.shape[-1])
        o_c = _rwkv_readout(flat(y_f), flat(y_b), flat(r_), flat(kf_), flat(kb_), flat(v_), flat(sg_),
                            _pad_rows(g2[i], GATE_PAD).astype(BF16), r_k[i], lnx_w[i], lnx_b[i], ones_bd)

        mixed = _merge(gates, o_a, o_b, o_c, (w_br_a, (i,)), (w_br_b, (i,)), (w_br_c, (i,)))
        xa = _mm_resid(mixed, (w_out, (i,)), xa, gm, nb, tt, seq, d, 512)

        jj = i // 2
        u2 = _norm_mod(xa, norm2[i], sh2, sc2, nb, tt, seq, BF16 if i % 2 == 0 else F32)
        if i % 2 == 0:
            h = _swiglu(u2, _pad_cols(ffn_w1[jj].astype(BF16), D_FF_PAD),
                        _pad_cols(ffn_w3[jj].astype(BF16), D_FF_PAD), 512, d)
            xa = _mm_resid(h, _pad_rows(ffn_w2[jj].astype(BF16), D_FF_PAD), xa, gf, nb, tt, seq, D_FF_PAD // 4, 1024)
        else:
            route = _router(u2, _pad_cols(router[jj], LANE))
            xa = _moe_routed(xa, u2, route, moe_w1, moe_w3, moe_w2, jj, gf, nb, tt, seq, last)
            if last:
                return xa.reshape(nb, seq, d)
    return xa.reshape(nb, tt, d)[:, :seq]
```

```python
import functools
import math

import jax
import jax.numpy as jnp
from jax import lax
from jax.experimental import pallas as pl
from jax.experimental.pallas import tpu as pltpu

F32 = jnp.float32
BF16 = jnp.bfloat16

D_MODEL = 4096
GRID_W = 64
NORM_EPS = 1e-6
MLA_HEADS = 16
Q_LORA = 1024
KV_LORA = 512
QK_NOPE = 128
QK_ROPE = 64
QK_HEAD = QK_NOPE + QK_ROPE
V_HEAD = 128
HEAD_PAD = 256
Q_SCALE = QK_HEAD ** -0.5 * math.log2(math.e)
ROPE_AXIS_DIM = QK_ROPE // 2
ROPE_PAIRS = ROPE_AXIS_DIM // 2
ROPE_BASE = 10000.0
FNET_GROUPS = 4
FNET_GROUP_DIM = 256
FNET_WIDTH = FNET_GROUPS * FNET_GROUP_DIM
RWKV_HEADS = 16
RWKV_HEAD_DIM = 64
RWKV_WIDTH = RWKV_HEADS * RWKV_HEAD_DIM
DECAY_LORA = 128
ICLR_LORA = 128
VRES_LORA = 96
GATE_LORA = 480
GATE_PAD = 512
VRES_PAD = 128
LNX_EPS = 64e-5
N_BRANCHES = 3
MLA_IN = Q_LORA + KV_LORA + QK_ROPE
MLA_IN_PAD = Q_LORA + KV_LORA + 128
Z_MAIN = 3 * RWKV_WIDTH + 2 * DECAY_LORA + 2 * ICLR_LORA
Z_PAD = Z_MAIN + GATE_PAD + VRES_PAD
D_FF = 11008
D_FF_PAD = 11264
N_EXPERTS = 8
TOP_K = 2
D_FF_EXPERT = 3072

LANE = 128
CHUNK = 64
SCAN_BLOCK = 256
SCAN_GROUP = 1
SCAN_PAIRS = 4
INV_BASE = 16
VMEM_LIMIT = 52 * 1024 * 1024


def _cparams(sem):
    return pltpu.CompilerParams(dimension_semantics=sem, vmem_limit_bytes=VMEM_LIMIT)


def _pick(n, cands):
    for c in cands:
        if n % c == 0:
            return c
    raise ValueError(f"no tile for {n} in {cands}")


def _silu(x):
    return x * (1.0 / (1.0 + jnp.exp(-x)))


def _sigmoid(x):
    return 1.0 / (1.0 + jnp.exp(-x))


def _bdot(a, b):
    return jnp.dot(a.astype(BF16), b.astype(BF16), preferred_element_type=F32)


def _split2(x):
    hi = x.astype(BF16)
    lo = (x - hi.astype(F32)).astype(BF16)
    return hi, lo


def _split3(x):
    hi = x.astype(BF16)
    r1 = x - hi.astype(F32)
    mid = r1.astype(BF16)
    lo = (r1 - mid.astype(F32)).astype(BF16)
    return hi, mid, lo


def _dot_x3(a, b):
    ah, al = _split2(a)
    bh, bl = _split2(b)
    d = functools.partial(jnp.dot, preferred_element_type=F32)
    return d(ah, bh) + (d(ah, bl) + d(al, bh))


def _fmm_kernel(*refs, n_a, n_prod, n_extra, n_out, nk, kaxis, epilogue, wt):
    a = refs[:n_a]
    b = refs[n_a:n_a + n_prod]
    ex = refs[n_a + n_prod:n_a + n_prod + n_extra]
    outs = refs[n_a + n_prod + n_extra:n_a + n_prod + n_extra + n_out]
    accs = refs[n_a + n_prod + n_extra + n_out:]
    pids = [pl.program_id(ax) for ax in range(kaxis + 1)]
    dims = (((1,), (1 if wt else 0,)), ((), ()))
    prods = [lax.dot_general(a[i % n_a][...], b[i][...].astype(BF16), dims, preferred_element_type=F32)
             for i in range(n_prod)]
    if nk == 1:
        epilogue(prods, ex, outs, pids)
        return
    k = pids[kaxis]

    @pl.when(k == 0)
    def _():
        for i in range(n_prod):
            accs[i][...] = prods[i]

    @pl.when(k > 0)
    def _():
        for i in range(n_prod):
            accs[i][...] += prods[i]

    @pl.when(k == nk - 1)
    def _():
        epilogue([acc[...] for acc in accs], ex, outs, pids)


def _fmm(grid, a_list, a_specs, b_list, b_specs, extras, extra_specs, out_shapes, out_specs,
         acc_shape, nk, epilogue, aliases=None, wt=False):
    n_prod = len(b_list)
    kern = functools.partial(_fmm_kernel, n_a=len(a_list), n_prod=n_prod, n_extra=len(extras), n_out=len(out_shapes),
                             nk=nk, kaxis=len(grid) - 1, epilogue=epilogue, wt=wt)
    scratch = [pltpu.VMEM(acc_shape, F32) for _ in range(n_prod)] if nk > 1 else []
    sem = ("parallel",) * (len(grid) - 1) + ("arbitrary",)
    return pl.pallas_call(
        kern,
        out_shape=out_shapes,
        grid=grid,
        in_specs=list(a_specs) + list(b_specs) + list(extra_specs),
        out_specs=out_specs,
        scratch_shapes=scratch,
        input_output_aliases=aliases or {},
        compiler_params=_cparams(sem),
    )(*a_list, *b_list, *extras)


def _wspec(w, tk, tn, wt=False):
    arr, lead = w if isinstance(w, tuple) else (w, ())
    if wt:
        return arr, pl.BlockSpec((None,) * len(lead) + (tn, tk), lambda i, j, k: tuple(lead) + (j, k))
    return arr, pl.BlockSpec((None,) * len(lead) + (tk, tn), lambda i, j, k: tuple(lead) + (k, j))


def _flat_mm(a, b, out_dtype, tm, tn, tk, epilogue=None, extras=(), extra_specs=(), bs=None,
             out_shapes=None, out_specs=None, aliases=None, n=None, wt=False):
    m, kdim = a.shape
    bs = bs if bs is not None else [b]
    bs, b_specs = zip(*[_wspec(w, tk, tn, wt) for w in bs])
    n = n if n is not None else bs[0].shape[-2 if wt else -1]
    nk = kdim // tk
    grid = (m // tm, n // tn, nk)
    a_specs = [pl.BlockSpec((tm, tk), lambda i, j, k: (i, k))]
    if epilogue is None:
        def epilogue(p, ex, outs, pids):
            outs[0][...] = p[0].astype(outs[0].dtype)
    if out_shapes is None:
        out_shapes = [jax.ShapeDtypeStruct((m, n), out_dtype)]
        out_specs = [pl.BlockSpec((tm, tn), lambda i, j, k: (i, j))]
    return _fmm(grid, [a], a_specs, bs, b_specs, list(extras), list(extra_specs), out_shapes,
                out_specs, (tm, tn), nk, epilogue, aliases, wt)


def _adaln_kernel(c_ref, w_ref, b_ref, o_ref):
    c = _silu(c_ref[...]).astype(BF16)
    o_ref[...] = jnp.dot(c, w_ref[...].astype(BF16), preferred_element_type=F32) + b_ref[...]


def _adaln(cond8, w_all, layer, bias):
    _, d, n = w_all.shape
    tn = 512
    return pl.pallas_call(
        _adaln_kernel,
        out_shape=jax.ShapeDtypeStruct((8, n), F32),
        grid=(n // tn,),
        in_specs=[pl.BlockSpec((8, d), lambda j: (0, 0)),
                  pl.BlockSpec((None, d, tn), lambda j: (layer, 0, j)),
                  pl.BlockSpec((1, tn), lambda j: (0, j))],
        out_specs=pl.BlockSpec((8, tn), lambda j: (0, j)),
        compiler_params=_cparams(("parallel",)),
    )(cond8, w_all, bias.reshape(1, n))


def _ctx_rows(tile_idx, tm, tiles_per_batch, seq):
    rows = (tile_idx % tiles_per_batch) * tm + lax.broadcasted_iota(jnp.int32, (tm, 1), 0)
    return rows >= seq


NORM_ROWS = 16
NORM_COLS = 512


def _norm_mod_kernel(x_ref, g_ref, shl_ref, scl_ref, shc_ref, scc_ref, o_ref, mod_ref, *, tm, tpb, seq):
    row0 = (pl.program_id(0) % tpb) * tm
    d = x_ref.shape[1]
    g = g_ref[...]
    mod_ref[0:1, :] = g * (1.0 + scl_ref[...])
    mod_ref[1:2, :] = g * (1.0 + scc_ref[...])
    mod_ref[2:3, :] = shl_ref[...]
    mod_ref[3:4, :] = shc_ref[...]
    chunks = [slice(c, c + NORM_COLS) for c in range(0, d, NORM_COLS)]

    def block(r, carry):
        rows = pl.ds(pl.multiple_of(r * NORM_ROWS, NORM_ROWS), NORM_ROWS)
        seg = (row0 + r * NORM_ROWS >= seq).astype(jnp.int32)
        part = jnp.zeros((NORM_ROWS, LANE), F32)
        for cols in chunks:
            xc = x_ref[rows, cols]
            sq = xc * xc
            for j in range(NORM_COLS // LANE):
                part = part + sq[:, j * LANE:(j + 1) * LANE]
        inv = lax.rsqrt(jnp.sum(part, axis=-1, keepdims=True) * (1.0 / d) + NORM_EPS)
        for cols in chunks:
            scale = mod_ref[pl.ds(seg, 1), cols]
            shift = mod_ref[pl.ds(2 + seg, 1), cols]
            o_ref[rows, cols] = (x_ref[rows, cols] * inv * scale + shift).astype(o_ref.dtype)
        return carry

    lax.fori_loop(0, tm // NORM_ROWS, block, 0, unroll=4)


def _norm_mod(x, g, sh, sc, nb, tt, seq, out_dtype=BF16):
    m, d = x.shape
    assert seq % NORM_ROWS == 0
    tm = _pick(tt, (256, 128))
    tpb = tt // tm
    lat = pl.BlockSpec((None, 1, d), lambda i: (i // tpb, 0, 0))
    ctx = pl.BlockSpec((None, 1, d), lambda i: (nb, 0, 0))
    return pl.pallas_call(
        functools.partial(_norm_mod_kernel, tm=tm, tpb=tpb, seq=seq),
        out_shape=jax.ShapeDtypeStruct((m, d), out_dtype),
        grid=(m // tm,),
        in_specs=[pl.BlockSpec((tm, d), lambda i: (i, 0)), pl.BlockSpec((1, d), lambda i: (0, 0)),
                  lat, lat, ctx, ctx],
        out_specs=pl.BlockSpec((tm, d), lambda i: (i, 0)),
        scratch_shapes=[pltpu.VMEM((8, d), F32)],
        compiler_params=_cparams(("parallel",)),
    )(x, g.reshape(1, d), sh, sc, sh, sc)


def _rope128(x, cos, sin):
    lane = lax.broadcasted_iota(jnp.int32, x.shape, 1)
    first = (lane % (2 * ROPE_PAIRS)) < ROPE_PAIRS
    swapped = jnp.where(first, pltpu.roll(x, LANE - ROPE_PAIRS, 1), pltpu.roll(x, ROPE_PAIRS, 1))
    return x * cos + swapped * sin


def _head_norm_rope(x_lo, x_hi, gain_lo, gain_hi, cos, sin, scale):
    ss = jnp.sum(x_lo * x_lo, axis=-1, keepdims=True) + jnp.sum(x_hi * x_hi, axis=-1, keepdims=True)
    inv = lax.rsqrt(ss * (1.0 / QK_HEAD) + NORM_EPS)
    lo = x_lo * inv * gain_lo
    hi = _rope128(x_hi * inv * gain_hi, cos, sin)
    return lo * scale, hi * scale


def _normed_products(a_ref, g_ref, w_ref):
    half = a_ref.shape[0] // 2
    parts = [slice(0, half), slice(half, 2 * half)]
    accs = []
    for rs in parts:
        a = a_ref[rs, :]
        an = a * lax.rsqrt(jnp.mean(a * a, axis=-1, keepdims=True) + NORM_EPS) * g_ref[...]
        accs.append(jnp.dot(an.astype(BF16), w_ref[...], preferred_element_type=F32))
    return parts, accs


def _mla_q_kernel(a_ref, g_ref, w_ref, gain_ref, cos_ref, sin_ref, q_ref, *, heads):
    parts, accs = _normed_products(a_ref, g_ref, w_ref)
    gain = gain_ref[...]
    for rs, acc in zip(parts, accs):
        cos, sin = cos_ref[rs, :], sin_ref[rs, :]
        for h in range(heads):
            c0 = h * HEAD_PAD
            lo, hi = _head_norm_rope(acc[:, c0:c0 + LANE], acc[:, c0 + LANE:c0 + HEAD_PAD], gain[:, :LANE],
                                     gain[:, LANE:], cos, sin, Q_SCALE)
            q_ref[rs, c0:c0 + LANE] = lo.astype(q_ref.dtype)
            q_ref[rs, c0 + LANE:c0 + HEAD_PAD] = hi.astype(q_ref.dtype)


def _mla_kv_kernel(a_ref, g_ref, w_ref, kr_ref, gain_ref, cos_ref, sin_ref, k_ref, v_ref, *, heads):
    parts, accs = _normed_products(a_ref, g_ref, w_ref)
    gain = gain_ref[...]
    for rs, acc in zip(parts, accs):
        kr = kr_ref[rs, :]
        kr_ss = jnp.sum(kr * kr, axis=-1, keepdims=True)
        kr_rot = _rope128(kr * gain[:, LANE:], cos_ref[rs, :], sin_ref[rs, :])
        for h in range(heads):
            c0 = h * HEAD_PAD
            kn = acc[:, c0:c0 + LANE]
            ss = jnp.sum(kn * kn, axis=-1, keepdims=True) + kr_ss
            inv = lax.rsqrt(ss * (1.0 / QK_HEAD) + NORM_EPS)
            k_ref[rs, c0:c0 + LANE] = (kn * inv * gain[:, :LANE]).astype(k_ref.dtype)
            k_ref[rs, c0 + LANE:c0 + HEAD_PAD] = (kr_rot * inv).astype(k_ref.dtype)
            v_ref[rs, h * V_HEAD:(h + 1) * V_HEAD] = acc[:, c0 + LANE:c0 + HEAD_PAD].astype(v_ref.dtype)


def _mla_up(mla_in, q_norm, w_uq, q_gain, kv_norm, w_ukv, k_gain, cos, sin):
    m = mla_in.shape[0]
    tm = _pick(m, (544, 512, 256))
    hpt = 4
    tn = hpt * HEAD_PAD
    grid = (m // tm, MLA_HEADS // hpt)
    row = lambda w: pl.BlockSpec((tm, w), lambda i, j: (i, 0))
    q = pl.pallas_call(
        functools.partial(_mla_q_kernel, heads=hpt),
        out_shape=jax.ShapeDtypeStruct((m, MLA_HEADS * HEAD_PAD), BF16),
        grid=grid,
        in_specs=[pl.BlockSpec((tm, Q_LORA), lambda i, j: (i, 0)),
                  pl.BlockSpec((1, Q_LORA), lambda i, j: (0, 0)),
                  pl.BlockSpec((Q_LORA, tn), lambda i, j: (0, j)),
                  pl.BlockSpec((1, HEAD_PAD), lambda i, j: (0, 0)),
                  row(LANE), row(LANE)],
        out_specs=pl.BlockSpec((tm, tn), lambda i, j: (i, j)),
        compiler_params=_cparams(("parallel", "parallel")),
    )(mla_in, q_norm.reshape(1, -1), w_uq, q_gain, cos, sin)
    k, v = pl.pallas_call(
        functools.partial(_mla_kv_kernel, heads=hpt),
        out_shape=[jax.ShapeDtypeStruct((m, MLA_HEADS * HEAD_PAD), BF16),
                   jax.ShapeDtypeStruct((m, MLA_HEADS * V_HEAD), BF16)],
        grid=grid,
        in_specs=[pl.BlockSpec((tm, KV_LORA), lambda i, j: (i, Q_LORA // KV_LORA)),
                  pl.BlockSpec((1, KV_LORA), lambda i, j: (0, 0)),
                  pl.BlockSpec((KV_LORA, tn), lambda i, j: (0, j)),
                  pl.BlockSpec((tm, LANE), lambda i, j: (i, (Q_LORA + KV_LORA) // LANE)),
                  pl.BlockSpec((1, HEAD_PAD), lambda i, j: (0, 0)),
                  row(LANE), row(LANE)],
        out_specs=[pl.BlockSpec((tm, tn), lambda i, j: (i, j)),
                   pl.BlockSpec((tm, hpt * V_HEAD), lambda i, j: (i, j))],
        compiler_params=_cparams(("parallel", "parallel")),
    )(mla_in, kv_norm.reshape(1, -1), w_ukv, mla_in, k_gain, cos, sin)
    return q, k, v


ATTN_ROWS = 256


def _softmax_pv(q, k, v, o_ref):
    nt = (((1,), (1,)), ((), ()))
    rows = min(ATTN_ROWS, q.shape[0])
    parts = [slice(h * rows, (h + 1) * rows) for h in range(q.shape[0] // rows)]
    scores = [lax.dot_general(q[rs], k, nt, preferred_element_type=F32) for rs in parts]
    v_ext = jnp.concatenate([v, jnp.ones_like(v)], axis=1)
    for rs, s in zip(parts, scores):
        p = jnp.exp2(s - jnp.max(s, axis=-1, keepdims=True))
        o = jnp.dot(p.astype(BF16), v_ext, preferred_element_type=F32)
        o_ref[rs, :] = (o[:, :V_HEAD] * (1.0 / o[:, V_HEAD:])).astype(o_ref.dtype)


def _attn_kernel(q_ref, k_ref, v_ref, o_ref):
    _softmax_pv(q_ref[...], k_ref[...], v_ref[...], o_ref)


def _attention(q, k, v, nb, tt, seq, with_ctx):
    q3 = q.reshape(nb, tt, MLA_HEADS * HEAD_PAD)
    k3 = k.reshape(nb, tt, MLA_HEADS * HEAD_PAD)
    v3 = v.reshape(nb, tt, MLA_HEADS * V_HEAD)
    tq = _pick(seq, (1024, 512, 256))
    o_lat = pl.pallas_call(
        _attn_kernel,
        out_shape=jax.ShapeDtypeStruct((nb, seq, MLA_HEADS * V_HEAD), BF16),
        grid=(nb, MLA_HEADS, seq // tq),
        in_specs=[pl.BlockSpec((None, tq, HEAD_PAD), lambda b, h, i: (b, i, h)),
                  pl.BlockSpec((None, tt, HEAD_PAD), lambda b, h, i: (b, 0, h)),
                  pl.BlockSpec((None, tt, V_HEAD), lambda b, h, i: (b, 0, h))],
        out_specs=pl.BlockSpec((None, tq, V_HEAD), lambda b, h, i: (b, i, h)),
        compiler_params=_cparams(("parallel", "parallel", "arbitrary")),
    )(q3, k3, v3)
    nctx = tt - seq
    if with_ctx:
        cb = seq // nctx
        o_ctx = pl.pallas_call(
            _attn_kernel,
            out_shape=jax.ShapeDtypeStruct((nb, nctx, MLA_HEADS * V_HEAD), BF16),
            grid=(nb, MLA_HEADS),
            in_specs=[pl.BlockSpec((None, nctx, HEAD_PAD), lambda b, h: (b, cb, h)),
                      pl.BlockSpec((None, nctx, HEAD_PAD), lambda b, h: (b, cb, h)),
                      pl.BlockSpec((None, nctx, V_HEAD), lambda b, h: (b, cb, h))],
            out_specs=pl.BlockSpec((None, nctx, V_HEAD), lambda b, h: (b, 0, h)),
            compiler_params=_cparams(("parallel", "parallel")),
        )(q3, k3, v3)
    else:
        o_ctx = jnp.zeros((nb, nctx, MLA_HEADS * V_HEAD), BF16)
    return jnp.concatenate([o_lat, o_ctx], axis=1).reshape(nb * tt, MLA_HEADS * V_HEAD)


def _dft_angles(n, cols):
    rows = jnp.arange(n, dtype=jnp.int32)
    ang = ((rows[:, None] * cols[None, :]) % n).astype(F32) * (2.0 * math.pi / n)
    return jnp.cos(ang), jnp.sin(ang)


def _dft_mats(n, scale):
    if n % GRID_W or n <= GRID_W:
        c, s = _dft_angles(n, jnp.arange(n, dtype=jnp.int32))
    else:
        ca, sa = _dft_angles(n, jnp.arange(n // GRID_W, dtype=jnp.int32) * GRID_W)
        cb, sb = _dft_angles(n, jnp.arange(GRID_W, dtype=jnp.int32))
        c = (ca[:, :, None] * cb[:, None, :] - sa[:, :, None] * sb[:, None, :]).reshape(n, n)
        s = (sa[:, :, None] * cb[:, None, :] + ca[:, :, None] * sb[:, None, :]).reshape(n, n)
    return (c * scale).astype(BF16), (s * scale).astype(BF16)


def _chan_dft_kernel(f_ref, c_ref, s_ref, zc_ref, zs_ref):
    for g in range(FNET_GROUPS):
        cols = slice(g * FNET_GROUP_DIM, (g + 1) * FNET_GROUP_DIM)
        fz = f_ref[:, cols]
        zc_ref[:, cols] = jnp.dot(fz, c_ref[...], preferred_element_type=F32).astype(zc_ref.dtype)
        zs_ref[:, cols] = jnp.dot(fz, s_ref[...], preferred_element_type=F32).astype(zs_ref.dtype)


def _fourier(f, nb, tt, seq, with_ctx, dft):
    c_t, sneg_t, c_c, s_c, c_x, s_x = dft
    m = f.shape[0]
    tmz = _pick(m, (1088, 512, 256))
    row = pl.BlockSpec((tmz, FNET_WIDTH), lambda i: (i, 0))
    mat = pl.BlockSpec((FNET_GROUP_DIM, FNET_GROUP_DIM), lambda i: (0, 0))
    shp = jax.ShapeDtypeStruct((m, FNET_WIDTH), BF16)
    zc, zs = pl.pallas_call(
        _chan_dft_kernel,
        out_shape=[shp, shp],
        grid=(m // tmz,),
        in_specs=[row, mat, mat],
        out_specs=[row, row],
        compiler_params=_cparams(("parallel",)),
    )(f, c_c, s_c)
    zc = zc.reshape(nb, tt, FNET_WIDTH)
    zs = zs.reshape(nb, tt, FNET_WIDTH)

    def ep_sum(p, ex, outs, pids):
        outs[0][...] = (p[0] + p[1]).astype(BF16)

    tm = _pick(seq, (1024, 512, 256))
    tk = tm
    tn = FNET_WIDTH
    nk = seq // tk
    aspec = pl.BlockSpec((tm, tk), lambda b, i, j, k: (i, k))
    bspec = pl.BlockSpec((None, tk, tn), lambda b, i, j, k: (b, k, j))
    y_lat = _fmm((nb, seq // tm, FNET_WIDTH // tn, nk), [c_t, sneg_t], [aspec, aspec], [zc, zs], [bspec, bspec],
                 [], [], [jax.ShapeDtypeStruct((nb, seq, FNET_WIDTH), BF16)],
                 [pl.BlockSpec((None, tm, tn), lambda b, i, j, k: (b, i, j))], (tm, tn), nk, ep_sum)[0]
    nctx = tt - seq
    if with_ctx:
        def ep_diff(p, ex, outs, pids):
            outs[0][...] = (p[0] - p[1]).astype(BF16)

        cb = seq // nctx
        aspec = pl.BlockSpec((nctx, nctx), lambda b, k: (0, 0))
        bspec = pl.BlockSpec((None, nctx, FNET_WIDTH), lambda b, k: (b, cb, 0))
        y_ctx = _fmm((nb, 1), [c_x, s_x], [aspec, aspec], [zc, zs], [bspec, bspec], [], [],
                     [jax.ShapeDtypeStruct((nb, nctx, FNET_WIDTH), BF16)],
                     [pl.BlockSpec((None, nctx, FNET_WIDTH), lambda b, k: (b, 0, 0))],
                     (nctx, FNET_WIDTH), 1, ep_diff)[0]
    else:
        y_ctx = jnp.zeros((nb, nctx, FNET_WIDTH), BF16)
    return jnp.concatenate([y_lat, y_ctx], axis=1).reshape(nb * tt, FNET_WIDTH)


def _head_sum(x, ones_bd):
    parts = []
    for s in range(x.shape[1] // LANE):
        hi, lo = _split2(x[:, s * LANE:(s + 1) * LANE])
        parts.append(jnp.dot(hi, ones_bd, preferred_element_type=F32)
                     + jnp.dot(lo, ones_bd, preferred_element_type=F32))
    return jnp.concatenate(parts, axis=-1)


def _feat_kernel(*refs, tf, tpb, lat_tiles, has_vres):
    (z_ref, zp_ref, zn_ref, mu_ref, w0_ref, w2_ref, a0_ref, a2_ref, kk_ref_p, ka_ref, bd_ref) = refs[:11]
    pos = 11
    if has_vres:
        vf_ref, v0_ref, v2_ref = refs[pos:pos + 3]
        pos += 3
    outs = refs[pos:]
    r_o, v_o, kk_o, kf_o, kb_o, lwf_o, lwb_o, af_o, ab_o, sg_o = outs[:10]
    j = pl.program_id(1)
    has_prev = jnp.logical_and(j != 0, j != lat_tiles)
    has_next = jnp.logical_and(j != lat_tiles - 1, j != tpb - 1)
    row = lax.broadcasted_iota(jnp.int32, (tf, 1), 0)

    def zf(lo, hi):
        z = z_ref[:, lo:hi]
        prev_row = jnp.where(has_prev, zp_ref[7:8, lo:hi], 0.0)
        next_row = jnp.where(has_next, zn_ref[0:1, lo:hi], 0.0)
        zp = jnp.where(row == 0, prev_row, pltpu.roll(z, 1, 0))
        zn = jnp.where(row == tf - 1, next_row, pltpu.roll(z, tf - 1, 0))
        return z + mu_ref[:, lo:hi] * (0.5 * (zp + zn) - z)

    w = RWKV_WIDTH
    r = zf(0, w)
    k = zf(w, 2 * w)
    v = zf(2 * w, 3 * w)
    r_o[...] = r
    if has_vres:
        vl = zf(Z_MAIN + GATE_PAD, Z_PAD)
        gate = _sigmoid(v0_ref[...] + _bdot(vl, v2_ref[...]))
        v_o[...] = v + (vf_ref[...] - v) * gate
    else:
        v_o[...] = v
        outs[10][...] = v
    kk = k * kk_ref_p[...]
    ss = _head_sum(kk * kk, bd_ref[...])
    kk_o[...] = kk * lax.rsqrt(ss + 1e-12)
    base = 3 * w
    for d, (lw_o, a_o, k_o) in enumerate(((lwf_o, af_o, kf_o), (lwb_o, ab_o, kb_o))):
        wd = zf(base + d * DECAY_LORA, base + (d + 1) * DECAY_LORA)
        ad = zf(base + 2 * DECAY_LORA + d * ICLR_LORA, base + 2 * DECAY_LORA + (d + 1) * ICLR_LORA)
        xw = w0_ref[d:d + 1, :] + _bdot(jnp.tanh(wd), w2_ref[d])
        lw_o[...] = -_sigmoid(xw) * math.exp(-0.5)
        a = _sigmoid(a0_ref[d:d + 1, :] + _bdot(ad, a2_ref[d]))
        a_o[...] = a
        k_o[...] = k * (1.0 + (a - 1.0) * ka_ref[...])
    sg_o[...] = _sigmoid(zf(Z_MAIN, Z_MAIN + GATE_PAD)).astype(sg_o.dtype)


def _rwkv_features(z, nb, tt, seq, mu, w0, w2, a0, a2, k_k, k_a, ones_bd, vres):
    tf = 128
    tpb = tt // tf
    lat_tiles = seq // tf
    z3 = z.reshape(nb, tt, Z_PAD)
    hb = tf // 8
    last8 = tt // 8 - 1
    w = RWKV_WIDTH
    full = lambda shape: pl.BlockSpec(shape, lambda b, j: (0,) * len(shape))
    tile = lambda width: pl.BlockSpec((None, tf, width), lambda b, j: (b, j, 0))
    in_specs = [tile(Z_PAD),
                pl.BlockSpec((None, 8, Z_PAD), lambda b, j: (b, jnp.maximum(j * hb - 1, 0), 0)),
                pl.BlockSpec((None, 8, Z_PAD), lambda b, j: (b, jnp.minimum((j + 1) * hb, last8), 0)),
                full((1, Z_PAD)), full((2, w)), full((2, DECAY_LORA, w)), full((2, w)), full((2, ICLR_LORA, w)),
                full((1, w)), full((1, w)), full((LANE, LANE))]
    args = [z3, z3, z3, mu.reshape(1, Z_PAD), w0, w2, a0, a2, k_k.reshape(1, w), k_a.reshape(1, w), ones_bd]
    has_vres = vres is not None
    if has_vres:
        v_first, v0, v2 = vres
        in_specs += [tile(w), full((1, w)), full((VRES_PAD, w))]
        args += [v_first.reshape(nb, tt, w), v0.reshape(1, w), v2]
    f3 = jax.ShapeDtypeStruct((nb, tt, w), F32)
    out_shape = [f3] * 9 + [jax.ShapeDtypeStruct((nb, tt, GATE_PAD), BF16)]
    out_specs = [tile(w)] * 9 + [tile(GATE_PAD)]
    if not has_vres:
        out_shape.append(f3)
        out_specs.append(tile(w))
    return pl.pallas_call(
        functools.partial(_feat_kernel, tf=tf, tpb=tpb, lat_tiles=lat_tiles, has_vres=has_vres),
        out_shape=out_shape,
        grid=(nb, tpb),
        in_specs=in_specs,
        out_specs=out_specs,
        compiler_params=_cparams(("parallel", "arbitrary")),
    )(*args)


def _stack2(x, lo_mask):
    return jnp.concatenate([jnp.where(lo_mask, x, 0.0), jnp.where(lo_mask, 0.0, x)], axis=0)


def _dot16(a, b):
    return jnp.dot(a.astype(BF16), b.astype(BF16), preferred_element_type=F32)


def _dg16(a, b, dims):
    return lax.dot_general(a.astype(BF16), b.astype(BF16), dims, preferred_element_type=F32)


def _bmm(a, b):
    return lax.dot_general(a.astype(BF16), b.astype(BF16), (((2,), (1,)), ((0,), (0,))),
                           preferred_element_type=F32)


def _bmm_nt(a, b):
    return lax.dot_general(a.astype(BF16), b.astype(BF16), (((2,), (2,)), ((0,), (0,))),
                           preferred_element_type=F32)


_GROUP_OPERANDS = ("kkt", "rt", "bh", "kh", "kb", "bb", "v")


def _group_operands(r, k, v, lw, kk, a, rev, consts, g):
    tri_cum, lo_mask = consts[0], consts[3]
    c = CHUNK
    s2 = lambda x: _stack2(x, lo_mask)
    cols = {name: [] for name in _GROUP_OPERANDS}
    tots = []
    for ch in range(g):
        sl = slice(ch * c, (ch + 1) * c)
        lw_c = lw[sl]
        cum = _dot_exact_rhs_left(tri_cum, lw_c)
        tot = cum[0:1, :] if rev else cum[c - 1:c, :]
        g_inv = jnp.exp(-cum)
        g_tail = jnp.exp(tot - cum)
        b_c = kk[sl] * a[sl]
        cols["kkt"].append(s2(kk[sl] * jnp.exp(cum - lw_c)))
        cols["rt"].append(s2(r[sl] * jnp.exp(cum)))
        cols["bh"].append(s2(b_c * g_inv))
        cols["kh"].append(s2(k[sl] * g_inv))
        cols["kb"].append(s2(k[sl] * g_tail))
        cols["bb"].append(s2(b_c * g_tail))
        cols["v"].append(s2(v[sl]))
        tots.append(tot)
    return {name: jnp.concatenate(cols[name], axis=0) for name in _GROUP_OPERANDS}, tots


def _solve_groups(ops, consts):
    _, strict, incl, _, eye_big, _, base_mask, level_masks = consts
    rt = ops["rt"]
    kkt16, rt16, bh16, kh16, v16 = (ops[nm].astype(BF16) for nm in ("kkt", "rt", "bh", "kh", "v"))
    nr = rt.shape[1]
    lhs = jnp.concatenate([kkt16, rt16], axis=1)
    big_b = _bmm_nt(lhs, bh16)
    big_k = _bmm_nt(lhs, kh16)
    m_k = jnp.where(strict, big_k[:, :nr], 0.0)
    a_qb = jnp.where(incl, big_b[:, nr:], 0.0)
    a_qk = jnp.where(incl, big_k[:, nr:], 0.0)
    m_b = jnp.where(strict, big_b[:, :nr], 0.0)
    n = jnp.where(base_mask, -m_b, 0.0)
    tinv = eye_big + n
    for _ in range(int(math.log2(INV_BASE)) - 1):
        n = _bmm(n, n)
        tinv = tinv + _bmm(tinv, n)
    for off_mask in level_masks:
        tinv = tinv - _bmm(_bmm(tinv, jnp.where(off_mask, m_b, 0.0)), tinv)
    mkv = _bmm(m_k, v16)
    wu16 = _bmm(tinv, jnp.concatenate([kkt16, mkv.astype(BF16)], axis=2)).astype(BF16)
    qy = _bmm(a_qb, wu16)
    q16 = (rt - qy[:, :, :LANE]).astype(BF16)
    y0 = _bmm(a_qk, v16) - qy[:, :, LANE:]
    return wu16, q16, y0


def _dot_exact_rhs_left(tri_bf16, x):
    h, m, l = _split3(x)
    d = functools.partial(jnp.dot, preferred_element_type=F32)
    return d(tri_bf16, h) + (d(tri_bf16, m) + d(tri_bf16, l))


def _scan_kernel(r_ref, k_ref, v_ref, lw_ref, kk_ref, a_ref, y_ref, h_ref, *, rev, n_groups, g):
    @pl.when(pl.program_id(2) == 0)
    def _():
        h_ref[...] = jnp.zeros_like(h_ref)

    c = CHUNK
    ri = lax.broadcasted_iota(jnp.int32, (c, c), 0)
    ci = lax.broadcasted_iota(jnp.int32, (c, c), 1)
    tri_cum = (ci >= ri if rev else ci <= ri).astype(BF16)
    nr = g * 2 * c
    r2 = lax.broadcasted_iota(jnp.int32, (nr, nr), 0)
    c2 = lax.broadcasted_iota(jnp.int32, (nr, nr), 1)
    same = (r2 // (2 * c)) == (c2 // (2 * c))
    strict = jnp.logical_and(same, c2 > r2 if rev else c2 < r2)
    incl = jnp.logical_and(same, c2 >= r2 if rev else c2 <= r2)
    eye_big = (r2 == c2).astype(F32)
    eye = eye_big[:LANE, :LANE]
    lo_mask = lax.broadcasted_iota(jnp.int32, (c, LANE), 1) < RWKV_HEAD_DIM
    base_mask = (r2 // INV_BASE) == (c2 // INV_BASE)
    level_masks = []
    s = 2 * INV_BASE
    while s <= c:
        level_masks.append(jnp.logical_and((r2 // s) == (c2 // s), (r2 // (s // 2)) != (c2 // (s // 2))))
        s *= 2
    consts = (tri_cum, strict, incl, lo_mask, eye_big, eye, base_mask, level_masks)
    gl = g * c
    rows = 2 * c
    units = [(pi, gi) for pi in range(SCAN_PAIRS) for gi in range(n_groups)]
    ops_list, tots_list = [], []
    for pi, gi in units:
        sl = slice(gi * gl, (gi + 1) * gl)
        ls = slice(pi * LANE, (pi + 1) * LANE)
        ops, tots = _group_operands(r_ref[sl, ls], k_ref[sl, ls], v_ref[sl, ls], lw_ref[sl, ls], kk_ref[sl, ls],
                                    a_ref[sl, ls], rev, consts, g)
        ops_list.append(ops)
        tots_list.append(tots)
    stacked = {name: jnp.stack([ops[name] for ops in ops_list]) for name in _GROUP_OPERANDS}
    wu16, q16, y0 = _solve_groups(stacked, consts)
    kb16 = stacked["kb"].astype(BF16)
    bb16 = stacked["bb"].astype(BF16)
    v16 = stacked["v"].astype(BF16)
    tn = (((0,), (0,)), ((), ()))
    for pi in range(SCAN_PAIRS):
        h = h_ref[pi]
        for gi in (range(n_groups - 1, -1, -1) if rev else range(n_groups)):
            u = units.index((pi, gi))
            for ch in (range(g - 1, -1, -1) if rev else range(g)):
                rs = slice(ch * rows, (ch + 1) * rows)
                pg = _dg16(bb16[u, rs], wu16[u, rs], tn)
                p_mat = eye * jnp.exp(tots_list[u][ch]) - pg[:, :LANE]
                g_mat = _dg16(kb16[u, rs], v16[u, rs], tn) - pg[:, LANE:]
                h16 = h.astype(BF16)
                y_st = _dot16(q16[u, rs], h16) + y0[u, rs]
                h = _dot16(p_mat, h16) + g_mat
                t0 = gi * gl + ch * c
                y_ref[t0:t0 + c, pi * LANE:(pi + 1) * LANE] = y_st[:c] + y_st[c:]
        h_ref[pi] = h


def _rwkv_scan(r, k, v, lw, kk, a, rev, seq):
    nb, tt, w = r.shape
    nblk = tt // SCAN_BLOCK
    lat_blk = seq // SCAN_BLOCK
    if rev:
        def blk(j):
            return nblk - 1 - j
    else:
        def blk(j):
            return jnp.where(j < nblk - lat_blk, lat_blk + j, j - (nblk - lat_blk))
    spec = pl.BlockSpec((None, SCAN_BLOCK, SCAN_PAIRS * LANE), lambda b, p, j: (b, blk(j), p))
    return pl.pallas_call(
        functools.partial(_scan_kernel, rev=rev, n_groups=SCAN_BLOCK // (CHUNK * SCAN_GROUP), g=SCAN_GROUP),
        out_shape=jax.ShapeDtypeStruct((nb, tt, w), F32),
        grid=(nb, w // (SCAN_PAIRS * LANE), nblk),
        in_specs=[spec] * 6,
        out_specs=spec,
        scratch_shapes=[pltpu.VMEM((SCAN_PAIRS, LANE, LANE), F32)],
        compiler_params=_cparams(("parallel", "parallel", "arbitrary")),
    )(r, k, v, lw, kk, a)


def _readout_kernel(yf_ref, yb_ref, r_ref, kf_ref, kb_ref, v_ref, sg_ref, g2_ref, rk_ref, lw_ref, lb_ref, bd_ref,
                    o_ref):
    bd = bd_ref[...]
    inv = 1.0 / RWKV_HEAD_DIM
    y = yf_ref[...] + yb_ref[...]
    mean = _head_sum(y, bd) * inv
    dlt = y - mean
    var = _head_sum(dlt * dlt, bd) * inv
    yn = dlt * lax.rsqrt(var + LNX_EPS) * lw_ref[...] + lb_ref[...]
    bonus = _head_sum(r_ref[...] * (kf_ref[...] + kb_ref[...]) * rk_ref[...], bd) * v_ref[...]
    g = jnp.dot(sg_ref[...], g2_ref[...], preferred_element_type=F32)
    o_ref[...] = ((yn + bonus) * g).astype(o_ref.dtype)


def _rwkv_readout(yf, yb, r, kf, kb, v, sg, g2, r_k, lnx_w, lnx_b, ones_bd):
    m, w = yf.shape
    tm = _pick(m, (272, 256, 128))
    row = lambda width: pl.BlockSpec((tm, width), lambda i: (i, 0))
    full = lambda shape: pl.BlockSpec(shape, lambda i: (0,) * len(shape))
    return pl.pallas_call(
        _readout_kernel,
        out_shape=jax.ShapeDtypeStruct((m, w), BF16),
        grid=(m // tm,),
        in_specs=[row(w)] * 6 + [row(GATE_PAD), full((GATE_PAD, w)), full((1, w)), full((1, w)), full((1, w)),
                                 full((LANE, LANE))],
        out_specs=row(w),
        compiler_params=_cparams(("parallel",)),
    )(yf, yb, r, kf, kb, v, sg, g2, r_k.reshape(1, w), lnx_w.reshape(1, w), lnx_b.reshape(1, w), ones_bd)


def _merge(gates, o_a, o_b, o_c, w_a, w_b, w_c):
    m = o_a.shape[0]
    d = w_a[0].shape[-1]
    tm = _pick(m, (1088, 512, 256))
    tn = 512

    def ep(p, ex, outs, pids):
        y = (_sigmoid(ex[0][...].astype(F32)) * p[0] + _sigmoid(ex[1][...].astype(F32)) * p[1]
             + _sigmoid(ex[2][...].astype(F32)) * p[2])
        outs[0][...] = y.astype(BF16)

    nbn = d // tn
    a_specs = [pl.BlockSpec((tm, o.shape[1]), lambda i, j, k: (i, 0)) for o in (o_a, o_b, o_c)]
    ws, b_specs = zip(*[_wspec(wt, wt[0].shape[-2], tn) for wt in (w_a, w_b, w_c)])
    g_specs = [pl.BlockSpec((tm, tn), functools.partial(lambda i, j, k, off: (i, off + j), off=br * nbn))
               for br in range(N_BRANCHES)]
    return _fmm((m // tm, nbn, 1), [o_a, o_b, o_c], a_specs, ws, b_specs, [gates] * 3, g_specs,
                [jax.ShapeDtypeStruct((m, d), BF16)], [pl.BlockSpec((tm, tn), lambda i, j, k: (i, j))],
                (tm, tn), 1, ep)[0]


def _mm_resid(a, w, x, gate, nb, tt, seq, tk, tn):
    m, kdim = a.shape
    tm = _pick(tt, (1088, 512, 256))
    tpb = tt // tm
    nk = kdim // tk
    w, wspec = _wspec(w, tk, tn)
    d = w.shape[-1]

    def ep(p, ex, outs, pids):
        ctx = _ctx_rows(pids[0], tm, tpb, seq)
        g = jnp.where(ctx, ex[2][...], ex[1][...])
        outs[0][...] = ex[0][...] + g * p[0]

    xspec = pl.BlockSpec((tm, tn), lambda i, j, k: (i, j))
    extra_specs = [xspec,
                   pl.BlockSpec((None, 1, tn), lambda i, j, k: (i // tpb, 0, j)),
                   pl.BlockSpec((None, 1, tn), lambda i, j, k: (nb, 0, j))]
    return _fmm((m // tm, d // tn, nk), [a], [pl.BlockSpec((tm, tk), lambda i, j, k: (i, k))],
                [w], [wspec], [x, gate, gate], extra_specs,
                [jax.ShapeDtypeStruct((m, d), F32)], [xspec], (tm, tn), nk, ep, aliases={2: 0})[0]


def _swiglu(u, w1, w3, tn, tk):
    m = u.shape[0]
    tm = _pick(m, (1088, 512, 256))

    def ep(p, ex, outs, pids):
        outs[0][...] = (_silu(p[0]) * p[1]).astype(BF16)

    return _flat_mm(u, None, BF16, tm, tn, tk, epilogue=ep, bs=[w1, w3])[0]


ROUTE_LANE = 8


def _router_kernel(u_ref, w_ref, o_ref):
    logits = _dot_x3(u_ref[...].astype(F32), w_ref[...])
    lane = lax.broadcasted_iota(jnp.int32, logits.shape, 1).astype(F32)
    valid = lane < N_EXPERTS
    neg = -1e30
    lg = jnp.where(valid, logits, neg)
    m1 = jnp.max(lg, axis=-1, keepdims=True)
    i1 = jnp.min(jnp.where(lg == m1, lane, float(LANE)), axis=-1, keepdims=True)
    lg2 = jnp.where(lane == i1, neg, lg)
    m2 = jnp.max(lg2, axis=-1, keepdims=True)
    i2 = jnp.min(jnp.where(lg2 == m2, lane, float(LANE)), axis=-1, keepdims=True)
    e2 = jnp.exp(m2 - m1)
    w1 = 1.0 / (1.0 + e2)
    w2 = e2 / (1.0 + e2)
    comb = jnp.where(lane == i1, w1, 0.0) + jnp.where(lane == i2, w2, 0.0)
    meta = (jnp.where(lane == ROUTE_LANE, i1, 0.0) + jnp.where(lane == ROUTE_LANE + 1, i2, 0.0)
            + jnp.where(lane == ROUTE_LANE + 2, w1, 0.0) + jnp.where(lane == ROUTE_LANE + 3, w2, 0.0))
    o_ref[...] = comb + meta


def _router(u, router_pad):
    m, d = u.shape
    tm = _pick(m, (544, 512, 256))
    return pl.pallas_call(
        _router_kernel,
        out_shape=jax.ShapeDtypeStruct((m, LANE), F32),
        grid=(m // tm,),
        in_specs=[pl.BlockSpec((tm, d), lambda i: (i, 0)), pl.BlockSpec((d, LANE), lambda i: (0, 0))],
        out_specs=pl.BlockSpec((tm, LANE), lambda i: (i, 0)),
        compiler_params=_cparams(("parallel",)),
    )(u, router_pad)


MOE_TILE = 512
GATHER_TILE = 256


def _route_plan(e1, e2, tm):
    n_tok = e1.shape[0]
    e_flat = jnp.stack([e1, e2], axis=1).reshape(-1)
    n_asg = e_flat.shape[0]
    onehot = (e_flat[:, None] == jnp.arange(N_EXPERTS, dtype=jnp.int32)[None, :]).astype(jnp.int32)
    rank = jnp.sum((jnp.cumsum(onehot, axis=0) - onehot) * onehot, axis=1)
    counts = jnp.sum(onehot, axis=0)
    padded = ((counts + tm - 1) // tm) * tm
    ends = jnp.cumsum(padded)
    slot = (ends - padded)[e_flat] + rank
    n_slots = n_asg + N_EXPERTS * tm
    src_tok = jnp.zeros((n_slots,), jnp.int32).at[slot].set(jnp.arange(n_asg, dtype=jnp.int32) // TOP_K)
    tile_start = jnp.arange(n_slots // tm, dtype=jnp.int32) * tm
    tile_e = jnp.sum((tile_start[:, None] >= ends[None, :]).astype(jnp.int32), axis=1)
    valid = (tile_start < ends[-1]).astype(jnp.int32)
    tile_e = jnp.where(valid == 1, tile_e, tile_e[jnp.maximum(ends[-1] // tm - 1, 0)])
    return src_tok, slot.reshape(n_tok, TOP_K), tile_e, valid


def _row_copy(src_hbm, row, buf, r, sem):
    return pltpu.make_async_copy(src_hbm.at[pl.ds(row, 1)], buf.at[pl.ds(r, 1)], sem)


def _gather_kernel(idx_ref, src_hbm, o_ref, buf, sem, *, tg, n_steps):
    i = pl.program_id(0)
    slot = i % 2

    def issue(step, s):
        def start(r, carry):
            _row_copy(src_hbm, idx_ref[step * tg + r], buf.at[s], r, sem.at[s]).start()
            return carry

        lax.fori_loop(0, tg, start, 0, unroll=8)

    @pl.when(i == 0)
    def _():
        issue(0, 0)

    @pl.when(i + 1 < n_steps)
    def _():
        issue(i + 1, 1 - slot)

    def wait(r, carry):
        _row_copy(src_hbm, 0, buf.at[slot], r, sem.at[slot]).wait()
        return carry

    lax.fori_loop(0, tg, wait, 0, unroll=8)
    o_ref[...] = buf[slot].astype(o_ref.dtype)


def _gather_rows(src, idx, out_dtype):
    n_slots = idx.shape[0]
    d = src.shape[1]
    tg = GATHER_TILE
    return pl.pallas_call(
        functools.partial(_gather_kernel, tg=tg, n_steps=n_slots // tg),
        out_shape=jax.ShapeDtypeStruct((n_slots, d), out_dtype),
        grid_spec=pltpu.PrefetchScalarGridSpec(
            num_scalar_prefetch=1,
            grid=(n_slots // tg,),
            in_specs=[pl.BlockSpec(memory_space=pl.ANY)],
            out_specs=pl.BlockSpec((tg, d), lambda i, idx_ref: (i, 0)),
            scratch_shapes=[pltpu.VMEM((2, tg, d), src.dtype), pltpu.SemaphoreType.DMA((2,))]),
        compiler_params=_cparams(("arbitrary",)),
    )(idx, src)


def _grouped_up_kernel(te_ref, tv_ref, a_ref, w1_ref, w3_ref, o_ref):
    valid = tv_ref[pl.program_id(1)] == 1

    @pl.when(valid)
    def _():
        a = a_ref[...]
        p1 = jnp.dot(a, w1_ref[...].astype(BF16), preferred_element_type=F32)
        p3 = jnp.dot(a, w3_ref[...].astype(BF16), preferred_element_type=F32)
        o_ref[...] = (_silu(p1) * p3).astype(o_ref.dtype)

    @pl.when(jnp.logical_not(valid))
    def _():
        o_ref[...] = jnp.zeros_like(o_ref)


def _grouped_down_kernel(te_ref, tv_ref, a_ref, w_ref, o_ref):
    valid = tv_ref[pl.program_id(1)] == 1

    @pl.when(valid)
    def _():
        o_ref[...] = jnp.dot(a_ref[...], w_ref[...].astype(BF16), preferred_element_type=F32)

    @pl.when(jnp.logical_not(valid))
    def _():
        o_ref[...] = jnp.zeros_like(o_ref)


def _grouped_mm(kern, a, ws, layer, tile_e, tile_valid, tn, out_dtype):
    n_slots, kdim = a.shape
    n = ws[0].shape[-1]
    tm = MOE_TILE
    wspec = pl.BlockSpec((None, None, kdim, tn), lambda j, i, te, tv: (layer, te[i], 0, j))
    return pl.pallas_call(
        kern,
        out_shape=jax.ShapeDtypeStruct((n_slots, n), out_dtype),
        grid_spec=pltpu.PrefetchScalarGridSpec(
            num_scalar_prefetch=2,
            grid=(n // tn, n_slots // tm),
            in_specs=[pl.BlockSpec((tm, kdim), lambda j, i, te, tv: (i, 0))] + [wspec] * len(ws),
            out_specs=pl.BlockSpec((tm, tn), lambda j, i, te, tv: (i, j))),
        compiler_params=_cparams(("parallel", "arbitrary")),
    )(tile_e, tile_valid, a, *ws)


def _combine_kernel(s1_ref, s2_ref, ys_hbm, x_ref, rw_ref, gate_ref, o_ref, buf1, buf2, sem, *, tc, spb, n_steps):
    i = pl.program_id(0) * spb + pl.program_id(1)
    slot = i % 2

    def issue(step, s):
        def start(r, carry):
            _row_copy(ys_hbm, s1_ref[step * tc + r], buf1.at[s], r, sem.at[s]).start()
            _row_copy(ys_hbm, s2_ref[step * tc + r], buf2.at[s], r, sem.at[s]).start()
            return carry

        lax.fori_loop(0, tc, start, 0, unroll=8)

    @pl.when(i == 0)
    def _():
        issue(0, 0)

    @pl.when(i + 1 < n_steps)
    def _():
        issue(i + 1, 1 - slot)

    def wait(r, carry):
        _row_copy(ys_hbm, 0, buf1.at[slot], r, sem.at[slot]).wait()
        _row_copy(ys_hbm, 0, buf2.at[slot], r, sem.at[slot]).wait()
        return carry

    lax.fori_loop(0, tc, wait, 0, unroll=8)
    rw = rw_ref[...]
    lane = lax.broadcasted_iota(jnp.int32, rw.shape, 1)
    w1 = jnp.sum(jnp.where(lane == ROUTE_LANE + 2, rw, 0.0), axis=-1, keepdims=True)
    w2 = jnp.sum(jnp.where(lane == ROUTE_LANE + 3, rw, 0.0), axis=-1, keepdims=True)
    o_ref[...] = x_ref[...] + gate_ref[...] * (w1 * buf1[slot] + w2 * buf2[slot])


def _moe_combine(x, ys, slots, route, gate, nb, tt, seq, latent_only):
    m, d = x.shape
    tc = GATHER_TILE
    spb = seq // tc
    bpb = tt // tc
    row = lambda b, i, s1, s2: (b * bpb + i, 0)
    out_row = (lambda b, i, s1, s2: (b * spb + i, 0)) if latent_only else row
    return pl.pallas_call(
        functools.partial(_combine_kernel, tc=tc, spb=spb, n_steps=nb * spb),
        out_shape=jax.ShapeDtypeStruct((nb * seq if latent_only else m, d), F32),
        grid_spec=pltpu.PrefetchScalarGridSpec(
            num_scalar_prefetch=2,
            grid=(nb, spb),
            in_specs=[pl.BlockSpec(memory_space=pl.ANY),
                      pl.BlockSpec((tc, d), row),
                      pl.BlockSpec((tc, LANE), row),
                      pl.BlockSpec((None, 1, d), lambda b, i, s1, s2: (b, 0, 0))],
            out_specs=pl.BlockSpec((tc, d), out_row),
            scratch_shapes=[pltpu.VMEM((2, tc, d), F32), pltpu.VMEM((2, tc, d), F32),
                            pltpu.SemaphoreType.DMA((2,))]),
        input_output_aliases={} if latent_only else {3: 0},
        compiler_params=_cparams(("arbitrary", "arbitrary")),
    )(slots[:, 0], slots[:, 1], ys, x, route, gate)


def _moe_routed(x, u, route, moe_w1, moe_w3, moe_w2, layer, gate, nb, tt, seq, latent_only):
    m, d = x.shape
    meta = route.reshape(nb, tt, LANE)[:, :seq].reshape(nb * seq, LANE)
    e1 = meta[:, ROUTE_LANE].astype(jnp.int32)
    e2 = meta[:, ROUTE_LANE + 1].astype(jnp.int32)
    src_tok, slots, tile_e, tile_valid = _route_plan(e1, e2, MOE_TILE)
    src_row = (src_tok // seq) * tt + src_tok % seq
    xs = _gather_rows(u, src_row, BF16)
    h = _grouped_mm(_grouped_up_kernel, xs, [moe_w1, moe_w3], layer, tile_e, tile_valid, 512, BF16)
    ys = _grouped_mm(_grouped_down_kernel, h, [moe_w2], layer, tile_e, tile_valid, 1024, F32)
    return _moe_combine(x, ys, slots, route, gate, nb, tt, seq, latent_only)


def _rope_lane_tables(seq, nctx, nb):
    rows = seq // GRID_W
    row = jnp.repeat(jnp.arange(rows, dtype=F32), GRID_W)
    col = jnp.tile(jnp.arange(GRID_W, dtype=F32), rows)
    inv_freq = ROPE_BASE ** (-2.0 * jnp.arange(ROPE_PAIRS, dtype=F32) / ROPE_AXIS_DIM)
    ar = row[:, None] * inv_freq
    ac = col[:, None] * inv_freq
    ones = jnp.ones((seq, LANE - QK_ROPE), F32)
    zeros = jnp.zeros((seq, LANE - QK_ROPE), F32)
    cos = jnp.concatenate([jnp.cos(ar), jnp.cos(ar), jnp.cos(ac), jnp.cos(ac), ones], axis=1)
    sin = jnp.concatenate([-jnp.sin(ar), jnp.sin(ar), -jnp.sin(ac), jnp.sin(ac), zeros], axis=1)
    cos = jnp.concatenate([cos, jnp.ones((nctx, LANE), F32)], axis=0)
    sin = jnp.concatenate([sin, jnp.zeros((nctx, LANE), F32)], axis=0)
    return jnp.tile(cos, (nb, 1)), jnp.tile(sin, (nb, 1))


def _pad_cols(w, n):
    return jnp.concatenate([w, jnp.zeros((w.shape[0], n - w.shape[1]), w.dtype)], axis=1)


def _pad_rows(w, n):
    return jnp.concatenate([w, jnp.zeros((n - w.shape[0], w.shape[1]), w.dtype)], axis=0)


def _z_layout(zsrc, vres_src, axis=-1):
    axis = axis % zsrc.ndim

    def zeros(n):
        shape = list(zsrc.shape)
        shape[axis] = n
        return jnp.zeros(shape, zsrc.dtype)

    parts = [lax.slice_in_dim(zsrc, 0, Z_MAIN, axis=axis), lax.slice_in_dim(zsrc, Z_MAIN, zsrc.shape[axis], axis=axis),
             zeros(GATE_PAD - GATE_LORA)]
    if vres_src is None:
        parts.append(zeros(VRES_PAD))
    else:
        parts += [vres_src, zeros(VRES_PAD - VRES_LORA)]
    return jnp.concatenate(parts, axis=axis)


def _pad_head(g):
    return jnp.pad(g, (0, HEAD_PAD - QK_HEAD)).reshape(1, HEAD_PAD)


def kernel(x, c, ctx, c_ctx, ada_w, ada_b, norm1, w_in, w_vres_down, q_norm, w_uq, kv_norm, w_ukv, q_gain, k_gain,
           rwkv_mu, vres_mu, w0, w2, a0, a2, k_k, k_a, v0, v2, r_k, lnx_w, lnx_b, g2, w_br_a, w_br_b, w_br_c,
           w_out, norm2, ffn_w1, ffn_w3, ffn_w2, router, moe_w1, moe_w3, moe_w2):
    nb, seq, d = x.shape
    nctx = ctx.shape[1]
    tt = seq + nctx
    m = nb * tt
    depth = ada_w.shape[0]
    xa = jnp.concatenate([x, ctx], axis=1).reshape(m, d)
    cond8 = jnp.concatenate([c, c_ctx[None], jnp.zeros((8 - nb - 1, d), F32)], axis=0)
    cos_t, sin_t = _rope_lane_tables(seq, nctx, nb)
    c_t, s_t = _dft_mats(seq, 1.0 / math.sqrt(seq))
    c_c, s_c = _dft_mats(FNET_GROUP_DIM, 1.0 / math.sqrt(FNET_GROUP_DIM))
    c_x, s_x = _dft_mats(nctx, 1.0 / math.sqrt(nctx))
    dft = (c_t, -s_t, c_c, s_c, c_x, s_x)
    li = jnp.arange(LANE)
    ones_bd = (li[:, None] // RWKV_HEAD_DIM == li[None, :] // RWKV_HEAD_DIM).astype(BF16)
    v_first = None
    gate_cols = N_BRANCHES * d
    w_in_t = jnp.swapaxes(w_in, 1, 2)
    for i in range(depth):
        last = i == depth - 1
        mod = _adaln(cond8, ada_w, i, ada_b[i])[:nb + 1].reshape(nb + 1, 6, 1, d)
        sh1, sc1, gm, sh2, sc2, gf = (mod[:, t] for t in range(6))
        wi = w_in_t[i]
        w_mla = _pad_rows(wi[gate_cols:gate_cols + MLA_IN], MLA_IN_PAD).astype(BF16)
        f0 = gate_cols + MLA_IN
        w_f = wi[f0:f0 + FNET_WIDTH].astype(BF16)
        zsrc = wi[f0 + FNET_WIDTH:]
        if i == 0:
            w_z = _z_layout(zsrc, None, 0).astype(BF16)
            mu_z = _z_layout(rwkv_mu[i], None)
        else:
            w_z = _z_layout(zsrc, jnp.swapaxes(w_vres_down[i - 1], 0, 1), 0).astype(BF16)
            mu_z = _z_layout(rwkv_mu[i], vres_mu[i - 1])

        u = _norm_mod(xa, norm1[i], sh1, sc1, nb, tt, seq)
        tm = _pick(m, (1088, 512, 256))
        gates = _flat_mm(u, (w_in_t, (i,)), BF16, tm, 512, d, n=gate_cols, wt=True)[0]
        mla_in = _flat_mm(u, w_mla, F32, _pick(m, (544, 512, 256)), MLA_IN_PAD, 2048, wt=True)[0]
        f = _flat_mm(u, w_f, BF16, tm, 512, d, wt=True)[0]
        z = _flat_mm(u, w_z, F32, _pick(m, (544, 512, 256)), Z_PAD // 3, d, wt=True)[0]

        w_uq_p = jnp.pad(w_uq[i].reshape(Q_LORA, MLA_HEADS, QK_HEAD),
                         ((0, 0), (0, 0), (0, HEAD_PAD - QK_HEAD))).reshape(Q_LORA, MLA_HEADS * HEAD_PAD)
        q, k, v = _mla_up(mla_in, q_norm[i], w_uq_p.astype(BF16), _pad_head(q_gain[i]), kv_norm[i],
                          w_ukv[i].astype(BF16), _pad_head(k_gain[i]), cos_t, sin_t)
        o_a = _attention(q, k, v, nb, tt, seq, not last)

        o_b = _fourier(f, nb, tt, seq, not last, dft)

        vres = None
        if i > 0:
            vres = (v_first, v0[i - 1], _pad_rows(v2[i - 1], VRES_PAD).astype(BF16))
        feat = _rwkv_features(z, nb, tt, seq, mu_z, w0[i], w2[i].astype(BF16), a0[i], a2[i].astype(BF16),
                              k_k[i], k_a[i], ones_bd, vres)
        r_, v_, kk_, kf_, kb_, lwf_, lwb_, af_, ab_, sg_ = feat[:10]
        if i == 0:
            v_first = feat[10]
        y_f = _rwkv_scan(r_, kf_, v_, lwf_, kk_, af_, False, seq)
        y_b = _rwkv_scan(r_, kb_, v_, lwb_, kk_, ab_, True, seq)
        flat = lambda t: t.reshape(m, t.shape[-1])
        o_c = _rwkv_readout(flat(y_f), flat(y_b), flat(r_), flat(kf_), flat(kb_), flat(v_), flat(sg_),
                            _pad_rows(g2[i], GATE_PAD).astype(BF16), r_k[i], lnx_w[i], lnx_b[i], ones_bd)

        mixed = _merge(gates, o_a, o_b, o_c, (w_br_a, (i,)), (w_br_b, (i,)), (w_br_c, (i,)))
        xa = _mm_resid(mixed, (w_out, (i,)), xa, gm, nb, tt, seq, d, 512)

        jj = i // 2
        u2 = _norm_mod(xa, norm2[i], sh2, sc2, nb, tt, seq, BF16 if i % 2 == 0 else F32)
        if i % 2 == 0:
            h = _swiglu(u2, _pad_cols(ffn_w1[jj].astype(BF16), D_FF_PAD),
                        _pad_cols(ffn_w3[jj].astype(BF16), D_FF_PAD), 512, d)
            xa = _mm_resid(h, _pad_rows(ffn_w2[jj].astype(BF16), D_FF_PAD), xa, gf, nb, tt, seq, D_FF_PAD // 4, 1024)
        else:
            route = _router(u2, _pad_cols(router[jj], LANE))
            xa = _moe_routed(xa, u2, route, moe_w1, moe_w3, moe_w2, jj, gf, nb, tt, seq, last)
            if last:
                return xa.reshape(nb, seq, d)
    return xa.reshape(nb, tt, d)[:, :seq]
```
